```python
import math
import jax
import jax.numpy as jnp
from jax import lax
import numpy as np

D_MODEL = 2048
BATCH = 4
SEQ = 4096
DEPTH = 4

GRID_W = 64
HEAD_DIM = 128
MIX_WIDTH = D_MODEL
EPS = 1e-6
A_WIDTH = MIX_WIDTH // 2
A_GROUPS = 8
A_GROUP_DIM = A_WIDTH // A_GROUPS
CHUNK = 128
B_WIDTH = MIX_WIDTH - A_WIDTH
HYENA_ORDER = 2
HYENA_EMB = 33
HYENA_BANDS = (HYENA_EMB - 1) // 2
HYENA_FFN = 64
HYENA_TARGET = 1e-2
HYENA_FAST = 0.3
HYENA_SLOW = 1.5
AB_IN = 2 * A_WIDTH + (HYENA_ORDER + 1) * B_WIDTH
C_WIDTH = MIX_WIDTH // 2
C_HEADS = C_WIDTH // HEAD_DIM
NA_KH = 8
NA_KW = 16
D_WIDTH = MIX_WIDTH - C_WIDTH
D_HEADS = 8
D_HALF = D_WIDTH // (2 * D_HEADS)
Q_BLOCK = 128
ROPE_THETA = 10000.0
CD_IN = 3 * C_WIDTH + 3 * D_WIDTH
D_FF = 256 * ((8 * D_MODEL // 3 + 255) // 256)
N_EVEN = (DEPTH + 1) // 2
N_ODD = DEPTH // 2

kernel_name = 'hybrid_gmlp_hyena_natten_diffattn_encoder'


def rms_norm(x, g):
    xf = x.astype(jnp.float32)
    y = xf * lax.rsqrt(jnp.mean(xf * xf, axis=-1, keepdims=True) + EPS)
    return (y * g.astype(jnp.float32)).astype(x.dtype)


def dwconv3(h, w, b):
    hp = jnp.pad(h, ((0, 0), (1, 1), (0, 0)))
    return hp[:, :-2] * w[0] + hp[:, 1:-1] * w[1] + hp[:, 2:] * w[2] + b


def rotary(x):
    L, dh = x.shape[1], x.shape[-1]
    half = dh // 2
    inv = 1.0 / (ROPE_THETA ** (jnp.arange(half, dtype=jnp.float32) * 2.0 / dh))
    ang = jnp.arange(L, dtype=jnp.float32)[:, None] * inv[None, :]
    cos = jnp.cos(ang)[None, :, None, None, :]
    sin = jnp.sin(ang)[None, :, None, None, :]
    xf = x.astype(jnp.float32)
    x1, x2 = xf[..., :half], xf[..., half:]
    return jnp.concatenate([x1 * cos - x2 * sin, x1 * sin + x2 * cos], axis=-1).astype(x.dtype)


def chunked_spatial_gating(p, v_gain, w_s, b_s):
    Bn, L, _ = p.shape
    u, v = jnp.split(jax.nn.gelu(p, approximate=False), 2, axis=-1)
    v = rms_norm(v, v_gain).reshape(Bn, L // CHUNK, CHUNK, A_GROUPS, A_GROUP_DIM)
    s = jnp.einsum('gpq,bcqgd->bcpgd', w_s, v) + b_s.T[None, None, :, :, None]
    return u * s.reshape(Bn, L, A_WIDTH)


def hyena_filter_spectrum(L, w1, b1, w2, b2, w3, freq):
    f32 = jnp.float32
    t = jnp.linspace(0.0, 1.0, L, dtype=f32)[:, None]
    w = 2.0 * math.pi * jnp.arange(L, dtype=f32)[:, None] / L
    bands = jnp.linspace(1e-4, HYENA_BANDS - 1, HYENA_BANDS, dtype=f32)[None, :]
    z = jnp.concatenate([t, jnp.cos(w * bands), -jnp.sin(w * bands)], axis=-1)
    fr = freq.astype(f32)
    h = jnp.sin(fr[0] * (z @ w1.astype(f32) + b1.astype(f32)))
    h = jnp.sin(fr[1] * (h @ w2.astype(f32) + b2.astype(f32)))
    h = (h @ w3.astype(f32)).reshape(L, HYENA_ORDER, 2, B_WIDTH)
    max_decay = math.log(HYENA_TARGET) / HYENA_FAST
    min_decay = math.log(HYENA_TARGET) / HYENA_SLOW
    deltas = jnp.abs(jnp.linspace(min_decay, max_decay, B_WIDTH, dtype=f32))
    h = h * jnp.exp(-t[:, :, None, None] * deltas)
    k = jnp.concatenate([h[:, :, 0], jnp.zeros((1, HYENA_ORDER, B_WIDTH), f32), h[:0:-1, :, 1]], axis=0)
    k = k / jnp.sum(jnp.abs(k), axis=0, keepdims=True)
    return jnp.fft.rfft(k, axis=0)


def long_conv(u, kf, skip):
    L = u.shape[1]
    uf32 = u.astype(jnp.float32)
    uf = jnp.fft.rfft(uf32, n=2 * L, axis=1)
    y = jnp.fft.irfft(uf * kf[None], n=2 * L, axis=1)[:, :L]
    return (y + uf32 * skip.astype(jnp.float32)).astype(u.dtype)


def hyena_mixer(p, conv_w, conv_b, kf, skip):
    p = dwconv3(p, conv_w, conv_b)
    v, x1, x2 = jnp.split(p, 3, axis=-1)
    z = x1 * long_conv(v, kf[:, 0], skip[0])
    return x2 * long_conv(z, kf[:, 1], skip[1])


def neighborhood_attention(q, k, v, rpb):
    Bn, L, H, dh = q.shape
    rows = L // GRID_W
    kh = min(NA_KH, rows)
    kw = NA_KW
    col = jnp.arange(GRID_W)
    col_start = jnp.clip(col - kw // 2, 0, GRID_W - kw)
    key_col = col_start[:, None] + jnp.arange(kw)[None, :]
    rel_col = key_col - col[:, None] + (NA_KW - 1)
    scale = dh ** -0.5

    def one_row(r):
        row_start = jnp.clip(r - kh // 2, 0, rows - kh)
        key_row = row_start + jnp.arange(kh)
        idx = (key_row[None, :, None] * GRID_W + key_col[:, None, :]).reshape(GRID_W, kh * kw)
        rel_row = key_row - r + (NA_KH - 1)
        bias = rpb[:, rel_row[None, :, None], rel_col[:, None, :]].reshape(H, GRID_W, kh * kw)
        qr = lax.dynamic_slice_in_dim(q, r * GRID_W, GRID_W, axis=1)
        kr = jnp.take(k, idx, axis=1)
        vr = jnp.take(v, idx, axis=1)
        s = jnp.einsum('bqhd,bqkhd->bhqk', qr, kr, preferred_element_type=jnp.float32) * scale
        pr = jax.nn.softmax(s + bias.astype(jnp.float32), axis=-1).astype(v.dtype)
        return jnp.einsum('bhqk,bqkhd->bqhd', pr, vr)

    out = lax.map(one_row, jnp.arange(rows))
    return out.transpose(1, 0, 2, 3, 4).reshape(Bn, L, H, dh)


def diff_attention(q, k, v, lam_params, subln, lam_init):
    Bn, L, H, _, dh = q.shape
    lp = lam_params.astype(jnp.float32)
    lam = jnp.exp(jnp.sum(lp[0] * lp[1])) - jnp.exp(jnp.sum(lp[2] * lp[3])) + lam_init
    scale = dh ** -0.5
    nblk = L // Q_BLOCK
    qb = q.reshape(Bn, nblk, Q_BLOCK, H, 2, dh).transpose(1, 0, 2, 3, 4, 5)

    def one_block(qblk):
        s = jnp.einsum('bqhmd,bkhmd->bhmqk', qblk, k, preferred_element_type=jnp.float32) * scale
        pr = jax.nn.softmax(s, axis=-1)
        a = pr[:, :, 0] - lam * pr[:, :, 1]
        return jnp.einsum('bhqk,bkhd->bqhd', a.astype(v.dtype), v)

    o = lax.map(one_block, qb)
    o = o.transpose(1, 0, 2, 3, 4).reshape(Bn, L, H, 2 * dh)
    return rms_norm(o, subln) * (1.0 - lam_init)


def conv_glu_ffn(h, w_up, conv_w, conv_b, w_down):
    a = dwconv3(h @ w_up, conv_w, conv_b)
    g, val = jnp.split(a, 2, axis=-1)
    return (jax.nn.silu(g) * val) @ w_down


def setup_inputs(seed: int = 0) -> dict:
    key = jax.random.key(seed)
    ks = iter(jax.random.split(key, 32))

    def nrm(shape, scale):
        return jax.random.normal(next(ks), shape, jnp.float32) * scale

    def gain(shape):
        return 1.0 + nrm(shape, 0.02)

    F = D_FF
    return {
        'x': nrm((BATCH, SEQ, D_MODEL), 1.0),
        'norm_mix': gain((DEPTH, D_MODEL)),
        'norm_ffn': gain((DEPTH, D_MODEL)),
        'w_out': nrm((DEPTH, MIX_WIDTH, D_MODEL), MIX_WIDTH ** -0.5),
        'ffn_up': nrm((DEPTH, D_MODEL, 2 * F), D_MODEL ** -0.5),
        'ffn_conv_w': nrm((DEPTH, 3, 2 * F), 3 ** -0.5),
        'ffn_conv_b': nrm((DEPTH, 2 * F), 0.02),
        'ffn_down': nrm((DEPTH, F, D_MODEL), F ** -0.5),
        'final_norm': gain((D_MODEL,)),
        'ab_w_in': nrm((N_EVEN, D_MODEL, AB_IN), D_MODEL ** -0.5),
        'a_vnorm': gain((N_EVEN, A_WIDTH)),
        'a_ws': nrm((N_EVEN, A_GROUPS, CHUNK, CHUNK), CHUNK ** -0.5),
        'a_bs': 1.0 + nrm((N_EVEN, A_GROUPS, CHUNK), 0.02),
        'b_conv_w': nrm((N_EVEN, 3, 3 * B_WIDTH), 3 ** -0.5),
        'b_conv_b': nrm((N_EVEN, 3 * B_WIDTH), 0.02),
        'b_filt_w1': nrm((N_EVEN, HYENA_EMB, HYENA_FFN), HYENA_EMB ** -0.5),
        'b_filt_b1': nrm((N_EVEN, HYENA_FFN), 0.02),
        'b_filt_w2': nrm((N_EVEN, HYENA_FFN, HYENA_FFN), HYENA_FFN ** -0.5),
        'b_filt_b2': nrm((N_EVEN, HYENA_FFN), 0.02),
        'b_filt_w3': nrm((N_EVEN, HYENA_FFN, HYENA_ORDER * 2 * B_WIDTH), HYENA_FFN ** -0.5),
        'b_filt_freq': gain((N_EVEN, 2, HYENA_FFN)),
        'b_skip': nrm((N_EVEN, HYENA_ORDER, B_WIDTH), 0.1),
        'cd_w_in': nrm((N_ODD, D_MODEL, CD_IN), D_MODEL ** -0.5),
        'c_rpb': nrm((N_ODD, C_HEADS, 2 * NA_KH - 1, 2 * NA_KW - 1), 0.02),
        'd_lambda': nrm((N_ODD, 4, D_HALF), 0.1),
        'd_subln': gain((N_ODD, 2 * D_HALF)),
    }


def reference(x, norm_mix, norm_ffn, w_out, ffn_up, ffn_conv_w, ffn_conv_b, ffn_down, final_norm,
              ab_w_in, a_vnorm, a_ws, a_bs, b_conv_w, b_conv_b, b_filt_w1, b_filt_b1, b_filt_w2,
              b_filt_b2, b_filt_w3, b_filt_freq, b_skip, cd_w_in, c_rpb, d_lambda, d_subln):
    Bn, L, _ = x.shape
    for l in range(DEPTH):
        i = l // 2
        h = rms_norm(x, norm_mix[l])
        if l % 2 == 0:
            p = h @ ab_w_in[i]
            ya = chunked_spatial_gating(p[..., :2 * A_WIDTH], a_vnorm[i], a_ws[i], a_bs[i])
            kf = hyena_filter_spectrum(L, b_filt_w1[i], b_filt_b1[i], b_filt_w2[i], b_filt_b2[i],
                                       b_filt_w3[i], b_filt_freq[i])
            yb = hyena_mixer(p[..., 2 * A_WIDTH:], b_conv_w[i], b_conv_b[i], kf, b_skip[i])
            y = jnp.concatenate([ya, yb], axis=-1)
        else:
            p = h @ cd_w_in[i]
            qkv_c = p[..., :3 * C_WIDTH].reshape(Bn, L, 3, C_HEADS, HEAD_DIM)
            yc = neighborhood_attention(qkv_c[:, :, 0], qkv_c[:, :, 1], qkv_c[:, :, 2], c_rpb[i])
            pd = p[..., 3 * C_WIDTH:]
            qd = pd[..., :D_WIDTH].reshape(Bn, L, D_HEADS, 2, D_HALF)
            kd = pd[..., D_WIDTH:2 * D_WIDTH].reshape(Bn, L, D_HEADS, 2, D_HALF)
            vd = pd[..., 2 * D_WIDTH:].reshape(Bn, L, D_HEADS, 2 * D_HALF)
            lam_init = 0.8 - 0.6 * math.exp(-0.3 * l)
            yd = diff_attention(rotary(qd), rotary(kd), vd, d_lambda[i], d_subln[i], lam_init)
            y = jnp.concatenate([yc.reshape(Bn, L, C_WIDTH), yd.reshape(Bn, L, D_WIDTH)], axis=-1)
        x = x + y @ w_out[l]
        x = x + conv_glu_ffn(rms_norm(x, norm_ffn[l]), ffn_up[l], ffn_conv_w[l], ffn_conv_b[l], ffn_down[l])
    return rms_norm(x, final_norm)
```

```python
import functools
import math

import numpy as np
import jax
import jax.numpy as jnp
from jax import lax
from jax.experimental import pallas as pl
from jax.experimental.pallas import tpu as pltpu

F32 = jnp.float32
BF16 = jnp.bfloat16

EPS = 1e-6
GRID_W = 64
HEAD_DIM = 128
CHUNK = 128
A_GROUPS = 8
NA_KH = 8
NA_KW = 16
D_HEADS = 8
ROPE_THETA = 10000.0
HYENA_ORDER = 2
HYENA_EMB = 33
HYENA_BANDS = (HYENA_EMB - 1) // 2
HYENA_FFN = 64
HYENA_TARGET = 1e-2
HYENA_FAST = 0.3
HYENA_SLOW = 1.5
MASK_VALUE = -1e30

VMEM_LIMIT_BYTES = 56 * 1024 * 1024
FREQ_BLOCK = 256
DFT_TILE = 2 * FREQ_BLOCK


def _params(*sem):
    return pltpu.CompilerParams(dimension_semantics=sem, vmem_limit_bytes=VMEM_LIMIT_BYTES)


def _tile(n, pref):
    t = min(n, pref)
    assert n % t == 0, (n, pref)
    return t


def _rmsnorm_kernel(x_ref, g_ref, o_ref):
    x = x_ref[...]
    ms = jnp.mean(x * x, axis=-1, keepdims=True)
    o_ref[...] = (x * lax.rsqrt(ms + EPS) * g_ref[...]).astype(o_ref.dtype)


def rmsnorm(x, g, out_dtype):
    m, d = x.shape
    tm = _tile(m, 512)
    return pl.pallas_call(
        _rmsnorm_kernel,
        grid=(m // tm,),
        in_specs=[pl.BlockSpec((tm, d), lambda i: (i, 0)), pl.BlockSpec((1, d), lambda i: (0, 0))],
        out_specs=pl.BlockSpec((tm, d), lambda i: (i, 0)),
        out_shape=jax.ShapeDtypeStruct((m, d), out_dtype),
        compiler_params=_params("parallel"),
        name="rmsnorm",
    )(x, g.reshape(1, d))


def _matmul_kernel(x_ref, w_ref, o_ref):
    o_ref[...] = jnp.dot(x_ref[...], w_ref[...], preferred_element_type=F32).astype(o_ref.dtype)


def matmul(x, w, col0, ncols, out_dtype):
    m, k = x.shape
    tm = _tile(m, 1024)
    tn = _tile(ncols, 1024)
    assert col0 % tn == 0
    jb = col0 // tn
    return pl.pallas_call(
        _matmul_kernel,
        grid=(m // tm, ncols // tn),
        in_specs=[pl.BlockSpec((tm, k), lambda i, j: (i, 0)),
                  pl.BlockSpec((k, tn), lambda i, j: (0, jb + j))],
        out_specs=pl.BlockSpec((tm, tn), lambda i, j: (i, j)),
        out_shape=jax.ShapeDtypeStruct((m, ncols), out_dtype),
        compiler_params=_params("parallel", "arbitrary"),
        name="matmul",
    )(x, w)


def _matmul_res_kernel(*refs, n_pairs):
    res_ref, o_ref = refs[2 * n_pairs], refs[2 * n_pairs + 1]
    acc = res_ref[...]
    for p in range(n_pairs):
        acc = acc + jnp.dot(refs[2 * p][...], refs[2 * p + 1][...], preferred_element_type=F32)
    o_ref[...] = acc


def matmul_residual(pairs, res):
    m, n = res.shape
    tm = _tile(m, 1024)
    tn = _tile(n, 512)
    in_specs, args = [], []
    for x, w, row0 in pairs:
        k = x.shape[1]
        assert row0 % k == 0
        rb = row0 // k
        in_specs.append(pl.BlockSpec((tm, k), lambda i, j: (i, 0)))
        in_specs.append(pl.BlockSpec((k, tn), lambda i, j, rb=rb: (rb, j)))
        args += [x, w]
    in_specs.append(pl.BlockSpec((tm, tn), lambda i, j: (i, j)))
    args.append(res)
    return pl.pallas_call(
        functools.partial(_matmul_res_kernel, n_pairs=len(pairs)),
        grid=(m // tm, n // tn),
        in_specs=in_specs,
        out_specs=pl.BlockSpec((tm, tn), lambda i, j: (i, j)),
        out_shape=jax.ShapeDtypeStruct((m, n), F32),
        compiler_params=_params("parallel", "arbitrary"),
        name="matmul_residual",
    )(*args)


def _matmul_conv_kernel(*refs, glu, tm, nm, seq_tiles):
    ncomp = 2 if glu else 1
    x_ref = refs[0]
    w_refs = refs[1:1 + ncomp]
    cw_refs = refs[1 + ncomp:1 + 2 * ncomp]
    b_refs = refs[1 + 2 * ncomp:1 + 3 * ncomp]
    o_ref = refs[1 + 3 * ncomp]
    acc_ref, carry_ref = refs[2 + 3 * ncomp], refs[3 + 3 * ncomp]
    i = pl.program_id(1)
    slot = i % 2

    @pl.when(i == 0)
    def _():
        carry_ref[...] = jnp.zeros_like(carry_ref)

    @pl.when(i < nm)
    def _():
        x = x_ref[...]
        for c in range(ncomp):
            acc_ref[slot, c] = jnp.dot(x, w_refs[c][...], preferred_element_type=F32)

    @pl.when(i >= 1)
    def _():
        ps = 1 - slot
        has_prev = (i - 1) % seq_tiles != 0
        has_next = i % seq_tiles != 0
        outs = []
        for c in range(ncomp):
            a = acc_ref[ps, c]
            rows = lax.broadcasted_iota(jnp.int32, a.shape, 0)
            prev_row = jnp.where(has_prev, carry_ref[c], 0.0)
            next_row = jnp.where(has_next, acc_ref[slot, c, 0:1, :], 0.0)
            up = jnp.where(rows == 0, prev_row, pltpu.roll(a, 1, 0))
            dn = jnp.where(rows == tm - 1, next_row, pltpu.roll(a, tm - 1, 0))
            cw = cw_refs[c]
            outs.append(up * cw[0:1, :] + a * cw[1:2, :] + dn * cw[2:3, :] + b_refs[c][...])
            carry_ref[c] = a[tm - 1:tm, :]
        if glu:
            g, val = outs
            res = g * (1.0 / (1.0 + jnp.exp(-g))) * val
        else:
            res = outs[0]
        o_ref[...] = res.astype(o_ref.dtype)


def matmul_conv(x, w, cw, cb, seq_len, col0, ncols, glu, out_dtype):
    m, k = x.shape
    tm = _tile(seq_len, 512)
    tn = _tile(ncols, 512)
    nm = m // tm
    seq_tiles = seq_len // tm
    assert col0 % tn == 0
    ncomp = 2 if glu else 1
    coffs = [c * (ncols // tn) for c in range(ncomp)]
    woffs = [col0 // tn + o for o in coffs]
    cb2 = cb.reshape(1, -1)
    in_specs = [pl.BlockSpec((tm, k), lambda j, i: (jnp.minimum(i, nm - 1), 0))]
    in_specs += [pl.BlockSpec((k, tn), lambda j, i, o=o: (0, o + j)) for o in woffs]
    in_specs += [pl.BlockSpec((3, tn), lambda j, i, o=o: (0, o + j)) for o in coffs]
    in_specs += [pl.BlockSpec((1, tn), lambda j, i, o=o: (0, o + j)) for o in coffs]
    args = [x] + [w] * ncomp + [cw] * ncomp + [cb2] * ncomp
    return pl.pallas_call(
        functools.partial(_matmul_conv_kernel, glu=glu, tm=tm, nm=nm, seq_tiles=seq_tiles),
        grid=(ncols // tn, nm + 1),
        in_specs=in_specs,
        out_specs=pl.BlockSpec((tm, tn), lambda j, i: (jnp.maximum(i - 1, 0), j)),
        out_shape=jax.ShapeDtypeStruct((m, ncols), out_dtype),
        scratch_shapes=[pltpu.VMEM((2, ncomp, tm, tn), F32), pltpu.VMEM((ncomp, 1, tn), F32)],
        compiler_params=_params("parallel", "arbitrary"),
        name="matmul_conv_glu" if glu else "matmul_conv",
    )(*args)


def _gmlp_kernel(p_ref, gain_ref, ws_ref, bs_ref, o_ref, *, tm, width):
    p = p_ref[...]
    g = 0.5 * p * (1.0 + lax.erf(p * (1.0 / math.sqrt(2.0))))
    u = g[:, :width]
    v = g[:, width:]
    ms = jnp.mean(v * v, axis=-1, keepdims=True)
    vb = (v * lax.rsqrt(ms + EPS) * gain_ref[...]).astype(BF16)
    gd = width // A_GROUPS
    for c in range(tm // CHUNK):
        r0 = c * CHUNK
        for gi in range(A_GROUPS):
            c0 = gi * gd
            s = jnp.dot(ws_ref[gi], vb[r0:r0 + CHUNK, c0:c0 + gd], preferred_element_type=F32)
            s = s + bs_ref[:, c0:c0 + gd]
            o_ref[r0:r0 + CHUNK, c0:c0 + gd] = (u[r0:r0 + CHUNK, c0:c0 + gd] * s).astype(o_ref.dtype)


def spatial_gating(p, v_gain, w_s, b_s):
    m, two_w = p.shape
    width = two_w // 2
    gd = width // A_GROUPS
    tm = _tile(m, 512)
    bs_full = jnp.repeat(b_s.T, gd, axis=1)
    return pl.pallas_call(
        functools.partial(_gmlp_kernel, tm=tm, width=width),
        grid=(m // tm,),
        in_specs=[pl.BlockSpec((tm, two_w), lambda i: (i, 0)),
                  pl.BlockSpec((1, width), lambda i: (0, 0)),
                  pl.BlockSpec((A_GROUPS, CHUNK, CHUNK), lambda i: (0, 0, 0)),
                  pl.BlockSpec((CHUNK, width), lambda i: (0, 0))],
        out_specs=pl.BlockSpec((tm, width), lambda i: (i, 0)),
        out_shape=jax.ShapeDtypeStruct((m, width), BF16),
        compiler_params=_params("parallel"),
        name="spatial_gating",
    )(p, v_gain.reshape(1, width), w_s.astype(BF16), bs_full)


def _dft_matrices(L):
    n = 2 * L
    t1n = 64
    t0n = L // t1n
    f = np.arange(L, dtype=np.int64)[:, None]
    a1 = (f * (t0n * np.arange(t1n, dtype=np.int64))[None, :]) % n
    a0 = (f * np.arange(t0n, dtype=np.int64)[None, :]) % n
    ca, sa, cb, sb = lax.optimization_barrier((
        jnp.asarray(np.cos(2 * np.pi * a1 / n), F32)[:, :, None],
        jnp.asarray(np.sin(2 * np.pi * a1 / n), F32)[:, :, None],
        jnp.asarray(np.cos(2 * np.pi * a0 / n), F32)[:, None, :],
        jnp.asarray(np.sin(2 * np.pi * a0 / n), F32)[:, None, :]))
    cosm = (ca * cb - sa * sb).reshape(L, L)
    sinm = -(sa * cb + ca * sb).reshape(L, L)
    alt = jnp.asarray(1.0 - 2.0 * (np.arange(L) % 2), F32)[None, :]
    is_dc = jnp.asarray(np.arange(L) == 0)[:, None]
    sinm = jnp.where(is_dc, alt, sinm)
    nb = L // FREQ_BLOCK
    fw = jnp.stack([cosm.reshape(nb, FREQ_BLOCK, L), sinm.reshape(nb, FREQ_BLOCK, L)], axis=1)
    fw = fw.reshape(n, L)
    scale = jnp.where(is_dc, 1.0 / n, 2.0 / n)
    fi = jnp.stack([(cosm * scale).reshape(nb, FREQ_BLOCK, L),
                    (sinm * scale).reshape(nb, FREQ_BLOCK, L)], axis=1).reshape(n, L)
    return fw.astype(BF16), fi.T.astype(BF16)


def _filter_features(L):
    t = np.linspace(0.0, 1.0, L)[:, None]
    w = 2.0 * np.pi * np.arange(L)[:, None] / L
    bands = np.linspace(1e-4, HYENA_BANDS - 1, HYENA_BANDS)[None, :]
    z = np.concatenate([t, np.cos(w * bands), -np.sin(w * bands)], axis=-1)
    zr = np.concatenate([z[:1], z[:0:-1]], axis=0)
    pad = ((0, 0), (0, 128 - HYENA_EMB))
    return jnp.asarray(np.pad(z, pad), F32), jnp.asarray(np.pad(zr, pad), F32)


def _filter_spectrum_kernel(z_ref, zr_ref, w1_ref, b1_ref, w2_ref, b2_ref, fr_ref, w3f_ref, w3b_ref,
                            dl_ref, fw_ref, o_ref, klo_ref, khi_ref):
    m = pl.program_id(2)

    @pl.when(m == 0)
    def _():
        hp = lax.Precision.HIGHEST

        def ffn(zz, w3):
            h = jnp.dot(zz, w1_ref[...], precision=hp, preferred_element_type=F32) + b1_ref[...]
            h = jnp.sin(fr_ref[0:1, :] * h)
            h = jnp.dot(h, w2_ref[...], precision=hp, preferred_element_type=F32) + b2_ref[...]
            h = jnp.sin(fr_ref[1:2, :] * h)
            return jnp.dot(h, w3, precision=hp, preferred_element_type=F32)

        z = z_ref[...]
        zr = zr_ref[...]
        dl = dl_ref[...]
        hf = ffn(z, w3f_ref[...]) * jnp.exp(-z[:, 0:1] * dl)
        hb = ffn(zr, w3b_ref[...]) * jnp.exp(-zr[:, 0:1] * dl)
        rows = lax.broadcasted_iota(jnp.int32, hb.shape, 0)
        hb = jnp.where(rows == 0, 0.0, hb)
        nrm = jnp.sum(jnp.abs(hf), axis=0, keepdims=True) + jnp.sum(jnp.abs(hb), axis=0, keepdims=True)
        inv = 1.0 / nrm
        klo_ref[...] = (hf * inv).astype(BF16)
        khi_ref[...] = (hb * inv).astype(BF16)

    fw = fw_ref[...]
    lo = jnp.dot(fw, klo_ref[...], preferred_element_type=F32)
    hi = jnp.dot(fw, khi_ref[...], preferred_element_type=F32)
    rows = lax.broadcasted_iota(jnp.int32, lo.shape, 0)
    o_ref[...] = jnp.where(rows % 2 == 0, lo + hi, lo - hi)


def hyena_filter_spectrum(L, width, fw, w1, b1, w2, b2, w3, freq):
    z, zr = _filter_features(L)
    tc = _tile(width, 512)
    ncb = width // tc
    max_decay = math.log(HYENA_TARGET) / HYENA_FAST
    min_decay = math.log(HYENA_TARGET) / HYENA_SLOW
    deltas = jnp.asarray(np.abs(np.linspace(min_decay, max_decay, width)), F32).reshape(1, width)
    w1p = jnp.pad(w1, ((0, 128 - HYENA_EMB), (0, 0)))
    const = lambda o, c, m: (0, 0)
    nm = 2 * L // DFT_TILE
    return pl.pallas_call(
        _filter_spectrum_kernel,
        grid=(HYENA_ORDER, ncb, nm),
        in_specs=[pl.BlockSpec((L, 128), const), pl.BlockSpec((L, 128), const),
                  pl.BlockSpec((128, HYENA_FFN), const), pl.BlockSpec((1, HYENA_FFN), const),
                  pl.BlockSpec((HYENA_FFN, HYENA_FFN), const), pl.BlockSpec((1, HYENA_FFN), const),
                  pl.BlockSpec((2, HYENA_FFN), const),
                  pl.BlockSpec((HYENA_FFN, tc), lambda o, c, m: (0, (2 * o) * ncb + c)),
                  pl.BlockSpec((HYENA_FFN, tc), lambda o, c, m: (0, (2 * o + 1) * ncb + c)),
                  pl.BlockSpec((1, tc), lambda o, c, m: (0, c)),
                  pl.BlockSpec((DFT_TILE, L), lambda o, c, m: (m, 0))],
        out_specs=pl.BlockSpec((None, DFT_TILE, tc), lambda o, c, m: (o, m, c)),
        out_shape=jax.ShapeDtypeStruct((HYENA_ORDER, 2 * L, width), F32),
        scratch_shapes=[pltpu.VMEM((L, tc), BF16), pltpu.VMEM((L, tc), BF16)],
        compiler_params=_params("parallel", "parallel", "arbitrary"),
        name="hyena_filter_spectrum",
    )(z, zr, w1p, b1.reshape(1, -1), w2, b2.reshape(1, -1), freq, w3, w3, deltas, fw)


def _dft_forward_kernel(u_ref, fw_ref, kf_ref, o_ref, ub_ref):
    m = pl.program_id(2)

    @pl.when(m == 0)
    def _():
        ub_ref[...] = u_ref[...].astype(BF16)

    acc = jnp.dot(fw_ref[...], ub_ref[...], preferred_element_type=F32)
    ur, ui = acc[:FREQ_BLOCK], acc[FREQ_BLOCK:]
    kf = kf_ref[...]
    kr, ki = kf[:FREQ_BLOCK], kf[FREQ_BLOCK:]
    yr = ur * kr - ui * ki
    yi = ur * ki + ui * kr
    rows = lax.broadcasted_iota(jnp.int32, yr.shape, 0)
    packed = jnp.logical_and(rows == 0, m == 0)
    yr = jnp.where(packed, ur * kr, yr)
    yi = jnp.where(packed, ui * ki, yi)
    o_ref[:FREQ_BLOCK, :] = yr.astype(o_ref.dtype)
    o_ref[FREQ_BLOCK:, :] = yi.astype(o_ref.dtype)


def dft_forward_times_filter(u, col0, width, fw, kf):
    bsz, L, _ = u.shape
    tc = _tile(width, 512)
    assert col0 % tc == 0
    cb0 = col0 // tc
    nm = 2 * L // DFT_TILE
    return pl.pallas_call(
        _dft_forward_kernel,
        grid=(width // tc, bsz, nm),
        in_specs=[pl.BlockSpec((None, L, tc), lambda c, b, m: (b, 0, cb0 + c)),
                  pl.BlockSpec((DFT_TILE, L), lambda c, b, m: (m, 0)),
                  pl.BlockSpec((DFT_TILE, tc), lambda c, b, m: (m, c))],
        out_specs=pl.BlockSpec((None, DFT_TILE, tc), lambda c, b, m: (b, m, c)),
        out_shape=jax.ShapeDtypeStruct((bsz, 2 * L, width), BF16),
        scratch_shapes=[pltpu.VMEM((L, tc), BF16)],
        compiler_params=_params("parallel", "parallel", "arbitrary"),
        name="dft_forward",
    )(u, fw, kf)


def _dft_inverse_kernel(y_ref, fi_ref, u_ref, g_ref, skip_ref, o_ref):
    conv = jnp.dot(fi_ref[...], y_ref[...], preferred_element_type=F32)
    o_ref[...] = (g_ref[...] * (conv + u_ref[...] * skip_ref[...])).astype(o_ref.dtype)


def dft_inverse_gated(y, fi, u, ucol0, gate, gcol0, skip, out_dtype):
    bsz, n, width = y.shape
    L = n // 2
    tc = _tile(width, 512)
    tm = _tile(L, 512)
    ub0, gb0 = ucol0 // tc, gcol0 // tc
    return pl.pallas_call(
        _dft_inverse_kernel,
        grid=(width // tc, bsz, L // tm),
        in_specs=[pl.BlockSpec((None, n, tc), lambda c, b, m: (b, 0, c)),
                  pl.BlockSpec((tm, n), lambda c, b, m: (m, 0)),
                  pl.BlockSpec((None, tm, tc), lambda c, b, m: (b, m, ub0 + c)),
                  pl.BlockSpec((None, tm, tc), lambda c, b, m: (b, m, gb0 + c)),
                  pl.BlockSpec((1, tc), lambda c, b, m: (0, c))],
        out_specs=pl.BlockSpec((None, tm, tc), lambda c, b, m: (b, m, c)),
        out_shape=jax.ShapeDtypeStruct((bsz, L, width), out_dtype),
        compiler_params=_params("parallel", "parallel", "arbitrary"),
        name="dft_inverse",
    )(y, fi, u, gate, skip.reshape(1, width))


def _na_kernel(q_ref, k_ref, v_ref, bias_ref, o_ref, kb_ref, vb_ref, *, rows, scale):
    kb_ref[...] = k_ref[...].astype(BF16)
    vb_ref[...] = v_ref[...].astype(BF16)
    nkeys = NA_KH * GRID_W

    def body(r, carry):
        rs = jnp.clip(r - NA_KH // 2, 0, rows - NA_KH)
        q0 = pl.multiple_of(r * GRID_W, GRID_W)
        k0 = pl.multiple_of(rs * GRID_W, GRID_W)
        q = q_ref[pl.ds(q0, GRID_W), :].astype(BF16)
        kk = kb_ref[pl.ds(k0, nkeys), :]
        s = lax.dot_general(q, kk, (((1,), (1,)), ((), ())), preferred_element_type=F32)
        s = s * scale + bias_ref[rs - r + (NA_KH - 1)]
        mx = jnp.max(s, axis=-1, keepdims=True)
        e = jnp.exp(s - mx)
        p = e * (1.0 / jnp.sum(e, axis=-1, keepdims=True))
        o = jnp.dot(p.astype(BF16), vb_ref[pl.ds(k0, nkeys), :], preferred_element_type=F32)
        o_ref[pl.ds(q0, GRID_W), :] = o.astype(o_ref.dtype)
        return carry

    lax.fori_loop(0, rows, body, 0)


def _na_bias_table(rpb):
    col = np.arange(GRID_W)
    cs = np.clip(col - NA_KW // 2, 0, GRID_W - NA_KW)
    kc = np.arange(GRID_W)
    inwin = (kc[None, :] >= cs[:, None]) & (kc[None, :] < cs[:, None] + NA_KW)
    rel_col = np.clip(kc[None, :] - col[:, None] + (NA_KW - 1), 0, 2 * NA_KW - 2)
    rel_row = np.arange(NA_KH)[:, None] + np.arange(NA_KH)[None, :]
    t = rpb[:, rel_row, :]
    t = t[:, :, :, rel_col]
    t = jnp.where(jnp.asarray(inwin)[None, None, None], t, MASK_VALUE)
    t = t.transpose(0, 1, 3, 2, 4)
    return t.reshape(rpb.shape[0], NA_KH, GRID_W, NA_KH * GRID_W).astype(F32)


def neighborhood_attention(p, bsz, L, heads, rpb):
    rows = L // GRID_W
    assert rows >= NA_KH
    bias = _na_bias_table(rpb)
    nkeys = NA_KH * GRID_W
    return pl.pallas_call(
        functools.partial(_na_kernel, rows=rows, scale=HEAD_DIM ** -0.5),
        grid=(bsz, heads),
        in_specs=[pl.BlockSpec((None, L, HEAD_DIM), lambda b, h: (b, 0, h)),
                  pl.BlockSpec((None, L, HEAD_DIM), lambda b, h: (b, 0, heads + h)),
                  pl.BlockSpec((None, L, HEAD_DIM), lambda b, h: (b, 0, 2 * heads + h)),
                  pl.BlockSpec((None, NA_KH, GRID_W, nkeys), lambda b, h: (h, 0, 0, 0))],
        out_specs=pl.BlockSpec((None, L, HEAD_DIM), lambda b, h: (b, 0, h)),
        out_shape=jax.ShapeDtypeStruct((bsz, L, heads * HEAD_DIM), BF16),
        scratch_shapes=[pltpu.VMEM((L, HEAD_DIM), BF16), pltpu.VMEM((L, HEAD_DIM), BF16)],
        compiler_params=_params("parallel", "parallel"),
        name="neighborhood_attention",
    )(p, p, p, bias)


def _rope(x, cos, sin_signed, half):
    lane = lax.broadcasted_iota(jnp.int32, x.shape, 1)
    width = x.shape[1]
    first = (lane % (2 * half)) < half
    partner = jnp.where(first, pltpu.roll(x, width - half, 1), pltpu.roll(x, half, 1))
    return x * cos + partner * sin_signed


def _diff_attn_kernel(q_ref, k_ref, v_ref, cq_ref, sq_ref, ck_ref, sk_ref, lam_ref, sub_ref, o_ref,
                      kb_ref, vb_ref, *, lam_init, scale, half):
    qi = pl.program_id(2)

    @pl.when(qi == 0)
    def _():
        kb_ref[...] = _rope(k_ref[...], ck_ref[...], sk_ref[...], half).astype(BF16)
        vb_ref[...] = v_ref[...].astype(BF16)

    tq = q_ref.shape[0]
    q = _rope(q_ref[...], cq_ref[...], sq_ref[...], half) * scale
    lane = lax.broadcasted_iota(jnp.int32, q.shape, 1)
    qbd = jnp.concatenate([jnp.where(lane < 2 * half, q, 0.0), jnp.where(lane >= 2 * half, q, 0.0)],
                          axis=0).astype(BF16)
    s = lax.dot_general(qbd, kb_ref[...], (((1,), (1,)), ((), ())), preferred_element_type=F32)
    e = jnp.exp(s - jnp.max(s, axis=-1, keepdims=True))
    r = 1.0 / jnp.sum(e, axis=-1, keepdims=True)
    lp = lam_ref[...]
    lam = (jnp.exp(jnp.sum(lp[0:1] * lp[1:2], axis=-1, keepdims=True))
           - jnp.exp(jnp.sum(lp[2:3] * lp[3:4], axis=-1, keepdims=True)) + lam_init)
    a = e[:tq] * r[:tq] - e[tq:] * (lam * r[tq:])
    o = jnp.dot(a.astype(BF16), vb_ref[...], preferred_element_type=F32)
    ms = jnp.mean(o * o, axis=-1, keepdims=True)
    o_ref[...] = (o * lax.rsqrt(ms + EPS) * sub_ref[...] * (1.0 - lam_init)).astype(o_ref.dtype)


def _rope_tables(L, half):
    inv = (1.0 / (np.float32(ROPE_THETA) ** (np.arange(half, dtype=np.float32) * np.float32(2.0)
                                             / np.float32(2 * half)))).astype(np.float32)
    ang = (np.arange(L, dtype=np.float32)[:, None] * inv[None, :]).astype(np.float64)
    cos, sin = np.cos(ang), np.sin(ang)
    reps = HEAD_DIM // (2 * half)
    cos_t = np.tile(cos, (1, 2 * reps))
    sin_t = np.tile(np.concatenate([-sin, sin], axis=1), (1, reps))
    return jnp.asarray(cos_t, F32), jnp.asarray(sin_t, F32)


def diff_attention(p, col0, bsz, L, heads, lam_params, subln, lam_init):
    half = HEAD_DIM // 4
    assert col0 % HEAD_DIM == 0
    c0 = col0 // HEAD_DIM
    tq = _tile(L, 256)
    cos_t, sin_t = _rope_tables(L, half)
    return pl.pallas_call(
        functools.partial(_diff_attn_kernel, lam_init=lam_init, scale=(2 * half) ** -0.5, half=half),
        grid=(bsz, heads, L // tq),
        in_specs=[pl.BlockSpec((None, tq, HEAD_DIM), lambda b, h, i: (b, i, c0 + h)),
                  pl.BlockSpec((None, L, HEAD_DIM), lambda b, h, i: (b, 0, c0 + heads + h)),
                  pl.BlockSpec((None, L, HEAD_DIM), lambda b, h, i: (b, 0, c0 + 2 * heads + h)),
                  pl.BlockSpec((tq, HEAD_DIM), lambda b, h, i: (i, 0)),
                  pl.BlockSpec((tq, HEAD_DIM), lambda b, h, i: (i, 0)),
                  pl.BlockSpec((L, HEAD_DIM), lambda b, h, i: (0, 0)),
                  pl.BlockSpec((L, HEAD_DIM), lambda b, h, i: (0, 0)),
                  pl.BlockSpec((4, 2 * half), lambda b, h, i: (0, 0)),
                  pl.BlockSpec((1, HEAD_DIM), lambda b, h, i: (0, 0))],
        out_specs=pl.BlockSpec((None, tq, HEAD_DIM), lambda b, h, i: (b, i, h)),
        out_shape=jax.ShapeDtypeStruct((bsz, L, heads * HEAD_DIM), BF16),
        scratch_shapes=[pltpu.VMEM((L, HEAD_DIM), BF16), pltpu.VMEM((L, HEAD_DIM), BF16)],
        compiler_params=_params("parallel", "parallel", "arbitrary"),
        name="diff_attention",
    )(p, p, p, cos_t, sin_t, cos_t, sin_t, lam_params, subln.reshape(1, HEAD_DIM))


def kernel(x, norm_mix, norm_ffn, w_out, ffn_up, ffn_conv_w, ffn_conv_b, ffn_down, final_norm,
           ab_w_in, a_vnorm, a_ws, a_bs, b_conv_w, b_conv_b, b_filt_w1, b_filt_b1, b_filt_w2,
           b_filt_b2, b_filt_w3, b_filt_freq, b_skip, cd_w_in, c_rpb, d_lambda, d_subln):
    bsz, L, d = x.shape
    depth = norm_mix.shape[0]
    m = bsz * L
    half_w = d // 2
    ff = ffn_down.shape[1]
    heads = half_w // HEAD_DIM
    assert heads == D_HEADS and half_w // A_GROUPS == CHUNK and L % GRID_W == 0
    xs = x.reshape(m, d)
    fw, fi = _dft_matrices(L)
    for l in range(depth):
        i = l // 2
        h = rmsnorm(xs, norm_mix[l], BF16)
        if l % 2 == 0:
            w_in = ab_w_in[i].astype(BF16)
            pa = matmul(h, w_in, 0, 2 * half_w, F32)
            ya = spatial_gating(pa, a_vnorm[i], a_ws[i], a_bs[i])
            pb = matmul_conv(h, w_in, b_conv_w[i], b_conv_b[i], L, 2 * half_w, 3 * half_w, False, F32)
            pb = pb.reshape(bsz, L, 3 * half_w)
            kf = hyena_filter_spectrum(L, half_w, fw, b_filt_w1[i], b_filt_b1[i], b_filt_w2[i],
                                       b_filt_b2[i], b_filt_w3[i], b_filt_freq[i])
            y1 = dft_forward_times_filter(pb, 0, half_w, fw, kf[0])
            z = dft_inverse_gated(y1, fi, pb, 0, pb, half_w, b_skip[i, 0], F32)
            y2 = dft_forward_times_filter(z, 0, half_w, fw, kf[1])
            yb = dft_inverse_gated(y2, fi, z, 0, pb, 2 * half_w, b_skip[i, 1], BF16)
            yb = yb.reshape(m, half_w)
        else:
            p = matmul(h, cd_w_in[i].astype(BF16), 0, 6 * half_w, F32).reshape(bsz, L, 6 * half_w)
            lam_init = 0.8 - 0.6 * math.exp(-0.3 * l)
            ya = neighborhood_attention(p, bsz, L, heads, c_rpb[i]).reshape(m, half_w)
            yb = diff_attention(p, 3 * half_w, bsz, L, heads, d_lambda[i], d_subln[i], lam_init)
            yb = yb.reshape(m, half_w)
        wo = w_out[l].astype(BF16)
        xs = matmul_residual([(ya, wo, 0), (yb, wo, half_w)], xs)
        hf = rmsnorm(xs, norm_ffn[l], BF16)
        act = matmul_conv(hf, ffn_up[l].astype(BF16), ffn_conv_w[l], ffn_conv_b[l], L, 0, ff, True, BF16)
        xs = matmul_residual([(act, ffn_down[l].astype(BF16), 0)], xs)
    return rmsnorm(xs, final_norm, F32).reshape(bsz, L, d)
```

```python
import functools
import math

import numpy as np
import jax
import jax.numpy as jnp
from jax import lax
from jax.experimental import pallas as pl
from jax.experimental.pallas import tpu as pltpu

F32 = jnp.float32
BF16 = jnp.bfloat16

EPS = 1e-6
GRID_W = 64
HEAD_DIM = 128
CHUNK = 128
A_GROUPS = 8
NA_KH = 8
NA_KW = 16
D_HEADS = 8
ROPE_THETA = 10000.0
HYENA_ORDER = 2
HYENA_EMB = 33
HYENA_BANDS = (HYENA_EMB - 1) // 2
HYENA_FFN = 64
HYENA_TARGET = 1e-2
HYENA_FAST = 0.3
HYENA_SLOW = 1.5
MASK_VALUE = -1e30

VMEM_LIMIT_BYTES = 56 * 1024 * 1024
DFT_TILE = 512


def _params(*sem):
    return pltpu.CompilerParams(dimension_semantics=sem, vmem_limit_bytes=VMEM_LIMIT_BYTES)


def _tile(n, pref):
    t = min(n, pref)
    assert n % t == 0, (n, pref)
    return t


def _rmsnorm_kernel(x_ref, g_ref, o_ref):
    x = x_ref[...]
    ms = jnp.mean(x * x, axis=-1, keepdims=True)
    o_ref[...] = (x * lax.rsqrt(ms + EPS) * g_ref[...]).astype(o_ref.dtype)


def rmsnorm(x, g, out_dtype):
    m, d = x.shape
    tm = _tile(m, 512)
    return pl.pallas_call(
        _rmsnorm_kernel,
        grid=(m // tm,),
        in_specs=[pl.BlockSpec((tm, d), lambda i: (i, 0)), pl.BlockSpec((1, d), lambda i: (0, 0))],
        out_specs=pl.BlockSpec((tm, d), lambda i: (i, 0)),
        out_shape=jax.ShapeDtypeStruct((m, d), out_dtype),
        compiler_params=_params("parallel"),
        name="rmsnorm",
    )(x, g.reshape(1, d))


def _matmul_kernel(x_ref, w_ref, o_ref, wb_ref):
    @pl.when(pl.program_id(1) == 0)
    def _():
        wb_ref[...] = w_ref[...].astype(BF16)

    o_ref[...] = jnp.dot(x_ref[...], wb_ref[...], preferred_element_type=F32).astype(o_ref.dtype)


def matmul(x, w, layer, col0, ncols, out_dtype):
    m, k = x.shape
    tm = _tile(m, 1024)
    tn = _tile(ncols, 1024)
    assert col0 % tn == 0
    jb = col0 // tn
    return pl.pallas_call(
        _matmul_kernel,
        grid=(ncols // tn, m // tm),
        in_specs=[pl.BlockSpec((tm, k), lambda j, i: (i, 0)),
                  pl.BlockSpec((None, k, tn), lambda j, i: (layer, 0, jb + j))],
        out_specs=pl.BlockSpec((tm, tn), lambda j, i: (i, j)),
        out_shape=jax.ShapeDtypeStruct((m, ncols), out_dtype),
        scratch_shapes=[pltpu.VMEM((k, tn), BF16)],
        compiler_params=_params("parallel", "arbitrary"),
        name="matmul",
    )(x, w)


def _matmul_res_kernel(*refs, n_pairs):
    res_ref, o_ref = refs[2 * n_pairs], refs[2 * n_pairs + 1]
    wb_refs = refs[2 * n_pairs + 2:]

    @pl.when(pl.program_id(1) == 0)
    def _():
        for p in range(n_pairs):
            wb_refs[p][...] = refs[2 * p + 1][...].astype(BF16)

    acc = res_ref[...]
    for p in range(n_pairs):
        acc = acc + jnp.dot(refs[2 * p][...], wb_refs[p][...], preferred_element_type=F32)
    o_ref[...] = acc


def matmul_residual(pairs, layer, res):
    m, n = res.shape
    kmax = max(x.shape[1] for x, _, _ in pairs)
    tm = _tile(m, 1024 if kmax <= 2048 else 512)
    tn = _tile(n, 512)
    in_specs, args, scratch = [], [], []
    for x, w, row0 in pairs:
        k = x.shape[1]
        assert row0 % k == 0
        rb = row0 // k
        in_specs.append(pl.BlockSpec((tm, k), lambda j, i: (i, 0)))
        in_specs.append(pl.BlockSpec((None, k, tn), lambda j, i, rb=rb: (layer, rb, j)))
        scratch.append(pltpu.VMEM((k, tn), BF16))
        args += [x, w]
    in_specs.append(pl.BlockSpec((tm, tn), lambda j, i: (i, j)))
    args.append(res)
    return pl.pallas_call(
        functools.partial(_matmul_res_kernel, n_pairs=len(pairs)),
        grid=(n // tn, m // tm),
        in_specs=in_specs,
        out_specs=pl.BlockSpec((tm, tn), lambda j, i: (i, j)),
        out_shape=jax.ShapeDtypeStruct((m, n), F32),
        scratch_shapes=scratch,
        compiler_params=_params("parallel", "arbitrary"),
        name="matmul_residual",
    )(*args)


def _matmul_conv_kernel(*refs, glu, tm, seq_tiles):
    ncomp = 2 if glu else 1
    x_ref = refs[0]
    w_refs = refs[1:1 + ncomp]
    cw_refs = refs[1 + ncomp:1 + 2 * ncomp]
    b_refs = refs[1 + 2 * ncomp:1 + 3 * ncomp]
    o_ref = refs[1 + 3 * ncomp]
    wb_ref, acc_ref, carry_ref = refs[2 + 3 * ncomp:5 + 3 * ncomp]
    i = pl.program_id(1)
    slot = i % 2
    ps = 1 - slot

    @pl.when(i == 0)
    def _():
        for c in range(ncomp):
            wb_ref[c] = w_refs[c][...].astype(BF16)
        carry_ref[...] = jnp.zeros_like(carry_ref)
        acc_ref[1] = jnp.zeros(acc_ref.shape[1:], F32)

    x = x_ref[...]
    for c in range(ncomp):
        acc_ref[slot, c] = jnp.dot(x, wb_ref[c], preferred_element_type=F32)

    has_prev = (i + seq_tiles - 1) % seq_tiles != 0
    has_next = i % seq_tiles != 0
    outs = []
    for c in range(ncomp):
        a = acc_ref[ps, c]
        rows = lax.broadcasted_iota(jnp.int32, a.shape, 0)
        prev_row = jnp.where(has_prev, carry_ref[c], 0.0)
        next_row = jnp.where(has_next, acc_ref[slot, c, 0:1, :], 0.0)
        up = jnp.where(rows == 0, prev_row, pltpu.roll(a, 1, 0))
        dn = jnp.where(rows == tm - 1, next_row, pltpu.roll(a, tm - 1, 0))
        cw = cw_refs[c]
        outs.append(up * cw[0:1, :] + a * cw[1:2, :] + dn * cw[2:3, :] + b_refs[c][...])
        carry_ref[c] = a[tm - 1:tm, :]
    if glu:
        g, val = outs
        res = g * (1.0 / (1.0 + jnp.exp(-g))) * val
    else:
        res = outs[0]
    o_ref[...] = res.astype(o_ref.dtype)


def matmul_conv(x, w, cw, cb, layer, seq_len, col0, ncols, glu, out_dtype):
    m, k = x.shape
    tm = _tile(seq_len, 512)
    tn = _tile(ncols, 512)
    nm = m // tm
    seq_tiles = seq_len // tm
    assert col0 % tn == 0
    ncomp = 2 if glu else 1
    coffs = [c * (ncols // tn) for c in range(ncomp)]
    woffs = [col0 // tn + o for o in coffs]
    cb3 = cb.reshape(cb.shape[0], 1, cb.shape[1])
    in_specs = [pl.BlockSpec((tm, k), lambda j, i: (jnp.minimum(i, nm - 1), 0))]
    in_specs += [pl.BlockSpec((None, k, tn), lambda j, i, o=o: (layer, 0, o + j)) for o in woffs]
    in_specs += [pl.BlockSpec((None, 3, tn), lambda j, i, o=o: (layer, 0, o + j)) for o in coffs]
    in_specs += [pl.BlockSpec((None, 1, tn), lambda j, i, o=o: (layer, 0, o + j)) for o in coffs]
    args = [x] + [w] * ncomp + [cw] * ncomp + [cb3] * ncomp
    return pl.pallas_call(
        functools.partial(_matmul_conv_kernel, glu=glu, tm=tm, seq_tiles=seq_tiles),
        grid=(ncols // tn, nm + 1),
        in_specs=in_specs,
        out_specs=pl.BlockSpec((tm, tn), lambda j, i: (jnp.maximum(i - 1, 0), j)),
        out_shape=jax.ShapeDtypeStruct((m, ncols), out_dtype),
        scratch_shapes=[pltpu.VMEM((ncomp, k, tn), BF16), pltpu.VMEM((2, ncomp, tm, tn), F32),
                        pltpu.VMEM((ncomp, 1, tn), F32)],
        compiler_params=_params("parallel", "arbitrary"),
        name="matmul_conv_glu" if glu else "matmul_conv",
    )(*args)


def _gmlp_kernel(p_ref, gain_ref, ws_ref, bs_ref, o_ref, *, tm, width):
    p = p_ref[...]
    g = 0.5 * p * (1.0 + lax.erf(p * (1.0 / math.sqrt(2.0))))
    u = g[:, :width]
    v = g[:, width:]
    ms = jnp.mean(v * v, axis=-1, keepdims=True)
    vb = (v * lax.rsqrt(ms + EPS) * gain_ref[...]).astype(BF16)
    gd = width // A_GROUPS
    for c in range(tm // CHUNK):
        r0 = c * CHUNK
        for gi in range(A_GROUPS):
            c0 = gi * gd
            s = jnp.dot(ws_ref[gi], vb[r0:r0 + CHUNK, c0:c0 + gd], preferred_element_type=F32)
            s = s + bs_ref[:, c0:c0 + gd]
            o_ref[r0:r0 + CHUNK, c0:c0 + gd] = (u[r0:r0 + CHUNK, c0:c0 + gd] * s).astype(o_ref.dtype)


def spatial_gating(p, v_gain, w_s, b_s):
    m, two_w = p.shape
    width = two_w // 2
    gd = width // A_GROUPS
    tm = _tile(m, 512)
    bs_full = jnp.repeat(b_s.T, gd, axis=1)
    return pl.pallas_call(
        functools.partial(_gmlp_kernel, tm=tm, width=width),
        grid=(m // tm,),
        in_specs=[pl.BlockSpec((tm, two_w), lambda i: (i, 0)),
                  pl.BlockSpec((1, width), lambda i: (0, 0)),
                  pl.BlockSpec((A_GROUPS, CHUNK, CHUNK), lambda i: (0, 0, 0)),
                  pl.BlockSpec((CHUNK, width), lambda i: (0, 0))],
        out_specs=pl.BlockSpec((tm, width), lambda i: (i, 0)),
        out_shape=jax.ShapeDtypeStruct((m, width), BF16),
        compiler_params=_params("parallel"),
        name="spatial_gating",
    )(p, v_gain.reshape(1, width), w_s.astype(BF16), bs_full)


def _dft_matrices(L):
    n = 2 * L
    r1 = 64
    r0 = L // r1
    t = np.arange(L, dtype=np.int64)[None, :]
    a1 = ((r0 * np.arange(r1, dtype=np.int64))[:, None] * t) % n
    a0 = (np.arange(r0, dtype=np.int64)[:, None] * t) % n
    ca, sa, cb, sb = lax.optimization_barrier((
        jnp.asarray(np.cos(2 * np.pi * a1 / n), F32)[:, None, :],
        jnp.asarray(np.sin(2 * np.pi * a1 / n), F32)[:, None, :],
        jnp.asarray(np.cos(2 * np.pi * a0 / n), F32)[None, :, :],
        jnp.asarray(np.sin(2 * np.pi * a0 / n), F32)[None, :, :]))
    cosm = (ca * cb - sa * sb).astype(BF16).reshape(L, L)
    sinm = (-(sa * cb + ca * sb)).astype(BF16).reshape(L, L)
    return cosm, sinm


def _filter_features(L):
    t = np.linspace(0.0, 1.0, L)[:, None]
    w = 2.0 * np.pi * np.arange(L)[:, None] / L
    bands = np.linspace(1e-4, HYENA_BANDS - 1, HYENA_BANDS)[None, :]
    z = np.concatenate([t, np.cos(w * bands), -np.sin(w * bands)], axis=-1)
    return jnp.asarray(np.pad(z, ((0, 0), (0, 128 - HYENA_EMB))), F32)


def _alternating(shape):
    rows = lax.broadcasted_iota(jnp.int32, shape, 0)
    return jnp.where(rows % 2 == 0, 1.0, -1.0)


def _filter_spectrum_kernel(z_ref, w1_ref, b1_ref, w2_ref, b2_ref, fr_ref, w3f_ref, w3b_ref,
                            dl_ref, cos_ref, sin_ref, o_ref, ke_ref, ko_ref, nyq_ref, *, n):
    m = pl.program_id(2)

    @pl.when(m == 0)
    def _():
        hp = lax.Precision.HIGHEST
        z = z_ref[...]
        h = jnp.dot(z, w1_ref[...], precision=hp, preferred_element_type=F32) + b1_ref[...]
        h = jnp.sin(fr_ref[0:1, :] * h)
        h = jnp.dot(h, w2_ref[...], precision=hp, preferred_element_type=F32) + b2_ref[...]
        h = jnp.sin(fr_ref[1:2, :] * h)
        decay = jnp.exp(-z[:, 0:1] * dl_ref[...])
        hf = jnp.dot(h, w3f_ref[...], precision=hp, preferred_element_type=F32) * decay
        hb = jnp.dot(h, w3b_ref[...], precision=hp, preferred_element_type=F32) * decay
        rows = lax.broadcasted_iota(jnp.int32, hb.shape, 0)
        hb = jnp.where(rows == 0, 0.0, hb)
        nrm = jnp.sum(jnp.abs(hf), axis=0, keepdims=True) + jnp.sum(jnp.abs(hb), axis=0, keepdims=True)
        inv = 1.0 / nrm
        ke = (hf + hb) * inv
        ke_ref[...] = ke.astype(BF16)
        ko_ref[...] = ((hf - hb) * inv).astype(BF16)
        nyq_ref[...] = jnp.sum(ke * _alternating(ke.shape), axis=0, keepdims=True) * (1.0 / n)

    kr = jnp.dot(cos_ref[...], ke_ref[...], preferred_element_type=F32)
    ki = jnp.dot(sin_ref[...], ko_ref[...], preferred_element_type=F32)
    first = jnp.logical_and(lax.broadcasted_iota(jnp.int32, kr.shape, 0) == 0, m == 0)
    scale = jnp.where(first, 1.0 / n, 2.0 / n)
    o_ref[0] = kr * scale
    o_ref[1] = jnp.where(first, nyq_ref[...], ki * scale)


def hyena_filter_spectrum(L, width, cosm, sinm, w1, b1, w2, b2, w3, freq):
    z = _filter_features(L)
    tc = _tile(width, 256)
    ncb = width // tc
    max_decay = math.log(HYENA_TARGET) / HYENA_FAST
    min_decay = math.log(HYENA_TARGET) / HYENA_SLOW
    deltas = jnp.asarray(np.abs(np.linspace(min_decay, max_decay, width)), F32).reshape(1, width)
    w1p = jnp.pad(w1, ((0, 128 - HYENA_EMB), (0, 0)))
    const = lambda o, c, m: (0, 0)
    tm = _tile(L, DFT_TILE)
    return pl.pallas_call(
        functools.partial(_filter_spectrum_kernel, n=2 * L),
        grid=(HYENA_ORDER, ncb, L // tm),
        in_specs=[pl.BlockSpec((L, 128), const),
                  pl.BlockSpec((128, HYENA_FFN), const), pl.BlockSpec((1, HYENA_FFN), const),
                  pl.BlockSpec((HYENA_FFN, HYENA_FFN), const), pl.BlockSpec((1, HYENA_FFN), const),
                  pl.BlockSpec((2, HYENA_FFN), const),
                  pl.BlockSpec((HYENA_FFN, tc), lambda o, c, m: (0, (2 * o) * ncb + c)),
                  pl.BlockSpec((HYENA_FFN, tc), lambda o, c, m: (0, (2 * o + 1) * ncb + c)),
                  pl.BlockSpec((1, tc), lambda o, c, m: (0, c)),
                  pl.BlockSpec((tm, L), lambda o, c, m: (m, 0)),
                  pl.BlockSpec((tm, L), lambda o, c, m: (m, 0))],
        out_specs=pl.BlockSpec((None, 2, tm, tc), lambda o, c, m: (o, 0, m, c)),
        out_shape=jax.ShapeDtypeStruct((HYENA_ORDER, 2, L, width), F32),
        scratch_shapes=[pltpu.VMEM((L, tc), BF16), pltpu.VMEM((L, tc), BF16), pltpu.VMEM((1, tc), F32)],
        compiler_params=_params("parallel", "parallel", "arbitrary"),
        name="hyena_filter_spectrum",
    )(z, w1p, b1.reshape(1, -1), w2, b2.reshape(1, -1), freq, w3, w3, deltas, cosm, sinm)


def _dft_forward_kernel(u_ref, cos_ref, sin_ref, kf_ref, o_ref, ub_ref, nyq_ref):
    m = pl.program_id(2)

    @pl.when(m == 0)
    def _():
        u = u_ref[...]
        ub_ref[...] = u.astype(BF16)
        nyq_ref[...] = jnp.sum(u * _alternating(u.shape), axis=0, keepdims=True)

    ub = ub_ref[...]
    ur = jnp.dot(cos_ref[...], ub, preferred_element_type=F32)
    ui = jnp.dot(sin_ref[...], ub, preferred_element_type=F32)
    kr, ki = kf_ref[0], kf_ref[1]
    yr = ur * kr - ui * ki
    yi = ur * ki + ui * kr
    first = jnp.logical_and(lax.broadcasted_iota(jnp.int32, yi.shape, 0) == 0, m == 0)
    yi = jnp.where(first, nyq_ref[...] * ki, yi)
    o_ref[0] = yr.astype(o_ref.dtype)
    o_ref[1] = yi.astype(o_ref.dtype)


def dft_forward_times_filter(u, col0, width, cosm, sinm, kf, order):
    bsz, L, _ = u.shape
    tc = _tile(width, 512)
    tm = _tile(L, DFT_TILE)
    assert col0 % tc == 0
    cb0 = col0 // tc
    return pl.pallas_call(
        _dft_forward_kernel,
        grid=(width // tc, bsz, L // tm),
        in_specs=[pl.BlockSpec((None, L, tc), lambda c, b, m: (b, 0, cb0 + c)),
                  pl.BlockSpec((tm, L), lambda c, b, m: (m, 0)),
                  pl.BlockSpec((tm, L), lambda c, b, m: (m, 0)),
                  pl.BlockSpec((None, 2, tm, tc), lambda c, b, m: (order, 0, m, c))],
        out_specs=pl.BlockSpec((None, 2, tm, tc), lambda c, b, m: (b, 0, m, c)),
        out_shape=jax.ShapeDtypeStruct((bsz, 2, L, width), BF16),
        scratch_shapes=[pltpu.VMEM((L, tc), BF16), pltpu.VMEM((1, tc), F32)],
        compiler_params=_params("parallel", "parallel", "arbitrary"),
        name="dft_forward",
    )(u, cosm, sinm, kf)


def _dft_inverse_kernel(y_ref, cos_ref, sin_ref, u_ref, g_ref, skip_ref, o_ref):
    conv = jnp.dot(cos_ref[...], y_ref[0], preferred_element_type=F32)
    conv = conv + jnp.dot(sin_ref[...], y_ref[1], preferred_element_type=F32)
    conv = conv + _alternating(conv.shape) * y_ref[1, 0:1, :].astype(F32)
    o_ref[...] = (g_ref[...] * (conv + u_ref[...] * skip_ref[...])).astype(o_ref.dtype)


def dft_inverse_gated(y, cosm, sinm, u, ucol0, gate, gcol0, skip, out_dtype):
    bsz, _, L, width = y.shape
    tc = _tile(width, 512)
    tm = _tile(L, DFT_TILE)
    ub0, gb0 = ucol0 // tc, gcol0 // tc
    return pl.pallas_call(
        _dft_inverse_kernel,
        grid=(width // tc, bsz, L // tm),
        in_specs=[pl.BlockSpec((None, 2, L, tc), lambda c, b, m: (b, 0, 0, c)),
                  pl.BlockSpec((tm, L), lambda c, b, m: (m, 0)),
                  pl.BlockSpec((tm, L), lambda c, b, m: (m, 0)),
                  pl.BlockSpec((None, tm, tc), lambda c, b, m: (b, m, ub0 + c)),
                  pl.BlockSpec((None, tm, tc), lambda c, b, m: (b, m, gb0 + c)),
                  pl.BlockSpec((1, tc), lambda c, b, m: (0, c))],
        out_specs=pl.BlockSpec((None, tm, tc), lambda c, b, m: (b, m, c)),
        out_shape=jax.ShapeDtypeStruct((bsz, L, width), out_dtype),
        compiler_params=_params("parallel", "parallel", "arbitrary"),
        name="dft_inverse",
    )(y, cosm, sinm, u, gate, skip.reshape(1, width))


def _na_kernel(q_ref, k_ref, v_ref, bias_ref, o_ref, kb_ref, vb_ref, *, rows, scale):
    kb_ref[...] = k_ref[...].astype(BF16)
    vb_ref[...] = v_ref[...].astype(BF16)
    nkeys = NA_KH * GRID_W

    def body(r, carry):
        rs = jnp.clip(r - NA_KH // 2, 0, rows - NA_KH)
        q0 = pl.multiple_of(r * GRID_W, GRID_W)
        k0 = pl.multiple_of(rs * GRID_W, GRID_W)
        q = q_ref[pl.ds(q0, GRID_W), :].astype(BF16)
        kk = kb_ref[pl.ds(k0, nkeys), :]
        s = lax.dot_general(q, kk, (((1,), (1,)), ((), ())), preferred_element_type=F32)
        s = s * scale + bias_ref[rs - r + (NA_KH - 1)]
        mx = jnp.max(s, axis=-1, keepdims=True)
        e = jnp.exp(s - mx)
        p = e * (1.0 / jnp.sum(e, axis=-1, keepdims=True))
        o = jnp.dot(p.astype(BF16), vb_ref[pl.ds(k0, nkeys), :], preferred_element_type=F32)
        o_ref[pl.ds(q0, GRID_W), :] = o.astype(o_ref.dtype)
        return carry

    lax.fori_loop(0, rows, body, 0, unroll=4)


def _na_bias_table(rpb):
    col = np.arange(GRID_W)
    cs = np.clip(col - NA_KW // 2, 0, GRID_W - NA_KW)
    kc = np.arange(GRID_W)
    inwin = (kc[None, :] >= cs[:, None]) & (kc[None, :] < cs[:, None] + NA_KW)
    rel_col = np.clip(kc[None, :] - col[:, None] + (NA_KW - 1), 0, 2 * NA_KW - 2)
    rel_row = np.arange(NA_KH)[:, None] + np.arange(NA_KH)[None, :]
    t = rpb[:, rel_row, :]
    t = t[:, :, :, rel_col]
    t = jnp.where(jnp.asarray(inwin)[None, None, None], t, MASK_VALUE)
    t = t.transpose(0, 1, 3, 2, 4)
    return t.reshape(rpb.shape[0], NA_KH, GRID_W, NA_KH * GRID_W).astype(F32)


def neighborhood_attention(p, bsz, L, heads, rpb):
    rows = L // GRID_W
    assert rows >= NA_KH and rows % 4 == 0
    bias = _na_bias_table(rpb)
    nkeys = NA_KH * GRID_W
    return pl.pallas_call(
        functools.partial(_na_kernel, rows=rows, scale=HEAD_DIM ** -0.5),
        grid=(bsz, heads),
        in_specs=[pl.BlockSpec((None, L, HEAD_DIM), lambda b, h: (b, 0, h)),
                  pl.BlockSpec((None, L, HEAD_DIM), lambda b, h: (b, 0, heads + h)),
                  pl.BlockSpec((None, L, HEAD_DIM), lambda b, h: (b, 0, 2 * heads + h)),
                  pl.BlockSpec((None, NA_KH, GRID_W, nkeys), lambda b, h: (h, 0, 0, 0))],
        out_specs=pl.BlockSpec((None, L, HEAD_DIM), lambda b, h: (b, 0, h)),
        out_shape=jax.ShapeDtypeStruct((bsz, L, heads * HEAD_DIM), BF16),
        scratch_shapes=[pltpu.VMEM((L, HEAD_DIM), BF16), pltpu.VMEM((L, HEAD_DIM), BF16)],
        compiler_params=_params("parallel", "parallel"),
        name="neighborhood_attention",
    )(p, p, p, bias)


def _rope(x, cos, sin_signed, half):
    lane = lax.broadcasted_iota(jnp.int32, x.shape, 1)
    width = x.shape[1]
    first = (lane % (2 * half)) < half
    partner = jnp.where(first, pltpu.roll(x, width - half, 1), pltpu.roll(x, half, 1))
    return x * cos + partner * sin_signed


def _diff_attn_kernel(q_ref, k_ref, v_ref, cq_ref, sq_ref, ck_ref, sk_ref, lam_ref, sub_ref, o_ref,
                      kb_ref, vb_ref, *, lam_init, scale, half):
    qi = pl.program_id(2)

    @pl.when(qi == 0)
    def _():
        kb_ref[...] = _rope(k_ref[...], ck_ref[...], sk_ref[...], half).astype(BF16)
        vb_ref[:, :HEAD_DIM] = v_ref[...].astype(BF16)
        vb_ref[:, HEAD_DIM:] = jnp.ones((vb_ref.shape[0], HEAD_DIM), BF16)

    tq = q_ref.shape[0]
    q = _rope(q_ref[...], cq_ref[...], sq_ref[...], half) * scale
    lane = lax.broadcasted_iota(jnp.int32, q.shape, 1)
    qbd = jnp.concatenate([jnp.where(lane < 2 * half, q, 0.0), jnp.where(lane >= 2 * half, q, 0.0)],
                          axis=0).astype(BF16)
    s = lax.dot_general(qbd, kb_ref[...], (((1,), (1,)), ((), ())), preferred_element_type=F32)
    e = jnp.exp(s - jnp.max(s, axis=-1, keepdims=True)).astype(BF16)
    pv = jnp.dot(e, vb_ref[...], preferred_element_type=F32)
    on = pv[:, :HEAD_DIM] * (1.0 / pv[:, HEAD_DIM:HEAD_DIM + 1])
    lp = lam_ref[...]
    lam = (jnp.exp(jnp.sum(lp[0:1] * lp[1:2], axis=-1, keepdims=True))
           - jnp.exp(jnp.sum(lp[2:3] * lp[3:4], axis=-1, keepdims=True)) + lam_init)
    o = on[:tq] - lam * on[tq:]
    ms = jnp.mean(o * o, axis=-1, keepdims=True)
    o_ref[...] = (o * lax.rsqrt(ms + EPS) * sub_ref[...] * (1.0 - lam_init)).astype(o_ref.dtype)


def _rope_tables(L, half):
    inv = (1.0 / (np.float32(ROPE_THETA) ** (np.arange(half, dtype=np.float32) * np.float32(2.0)
                                             / np.float32(2 * half)))).astype(np.float32)
    ang = (np.arange(L, dtype=np.float32)[:, None] * inv[None, :]).astype(np.float64)
    cos, sin = np.cos(ang), np.sin(ang)
    reps = HEAD_DIM // (2 * half)
    cos_t = np.tile(cos, (1, 2 * reps))
    sin_t = np.tile(np.concatenate([-sin, sin], axis=1), (1, reps))
    return jnp.asarray(cos_t, F32), jnp.asarray(sin_t, F32)


def diff_attention(p, col0, bsz, L, heads, lam_params, subln, lam_init):
    half = HEAD_DIM // 4
    assert col0 % HEAD_DIM == 0
    c0 = col0 // HEAD_DIM
    tq = _tile(L, 256)
    cos_t, sin_t = _rope_tables(L, half)
    return pl.pallas_call(
        functools.partial(_diff_attn_kernel, lam_init=lam_init, scale=(2 * half) ** -0.5, half=half),
        grid=(bsz, heads, L // tq),
        in_specs=[pl.BlockSpec((None, tq, HEAD_DIM), lambda b, h, i: (b, i, c0 + h)),
                  pl.BlockSpec((None, L, HEAD_DIM), lambda b, h, i: (b, 0, c0 + heads + h)),
                  pl.BlockSpec((None, L, HEAD_DIM), lambda b, h, i: (b, 0, c0 + 2 * heads + h)),
                  pl.BlockSpec((tq, HEAD_DIM), lambda b, h, i: (i, 0)),
                  pl.BlockSpec((tq, HEAD_DIM), lambda b, h, i: (i, 0)),
                  pl.BlockSpec((L, HEAD_DIM), lambda b, h, i: (0, 0)),
                  pl.BlockSpec((L, HEAD_DIM), lambda b, h, i: (0, 0)),
                  pl.BlockSpec((4, 2 * half), lambda b, h, i: (0, 0)),
                  pl.BlockSpec((1, HEAD_DIM), lambda b, h, i: (0, 0))],
        out_specs=pl.BlockSpec((None, tq, HEAD_DIM), lambda b, h, i: (b, i, h)),
        out_shape=jax.ShapeDtypeStruct((bsz, L, heads * HEAD_DIM), BF16),
        scratch_shapes=[pltpu.VMEM((L, HEAD_DIM), BF16), pltpu.VMEM((L, 2 * HEAD_DIM), BF16)],
        compiler_params=_params("parallel", "parallel", "arbitrary"),
        name="diff_attention",
    )(p, p, p, cos_t, sin_t, cos_t, sin_t, lam_params, subln.reshape(1, HEAD_DIM))


def kernel(x, norm_mix, norm_ffn, w_out, ffn_up, ffn_conv_w, ffn_conv_b, ffn_down, final_norm,
           ab_w_in, a_vnorm, a_ws, a_bs, b_conv_w, b_conv_b, b_filt_w1, b_filt_b1, b_filt_w2,
           b_filt_b2, b_filt_w3, b_filt_freq, b_skip, cd_w_in, c_rpb, d_lambda, d_subln):
    bsz, L, d = x.shape
    depth = norm_mix.shape[0]
    m = bsz * L
    half_w = d // 2
    ff = ffn_down.shape[1]
    heads = half_w // HEAD_DIM
    assert heads == D_HEADS and half_w // A_GROUPS == CHUNK and L % GRID_W == 0
    xs = x.reshape(m, d)
    cosm, sinm = _dft_matrices(L)
    for l in range(depth):
        i = l // 2
        h = rmsnorm(xs, norm_mix[l], BF16)
        if l % 2 == 0:
            pa = matmul(h, ab_w_in, i, 0, 2 * half_w, F32)
            ya = spatial_gating(pa, a_vnorm[i], a_ws[i], a_bs[i])
            pb = matmul_conv(h, ab_w_in, b_conv_w, b_conv_b, i, L, 2 * half_w, 3 * half_w, False, F32)
            pb = pb.reshape(bsz, L, 3 * half_w)
            kf = hyena_filter_spectrum(L, half_w, cosm, sinm, b_filt_w1[i], b_filt_b1[i], b_filt_w2[i],
                                       b_filt_b2[i], b_filt_w3[i], b_filt_freq[i])
            y1 = dft_forward_times_filter(pb, 0, half_w, cosm, sinm, kf, 0)
            z = dft_inverse_gated(y1, cosm, sinm, pb, 0, pb, half_w, b_skip[i, 0], F32)
            y2 = dft_forward_times_filter(z, 0, half_w, cosm, sinm, kf, 1)
            yb = dft_inverse_gated(y2, cosm, sinm, z, 0, pb, 2 * half_w, b_skip[i, 1], BF16)
            yb = yb.reshape(m, half_w)
        else:
            p = matmul(h, cd_w_in, i, 0, 6 * half_w, F32).reshape(bsz, L, 6 * half_w)
            lam_init = 0.8 - 0.6 * math.exp(-0.3 * l)
            ya = neighborhood_attention(p, bsz, L, heads, c_rpb[i]).reshape(m, half_w)
            yb = diff_attention(p, 3 * half_w, bsz, L, heads, d_lambda[i], d_subln[i], lam_init)
            yb = yb.reshape(m, half_w)
        xs = matmul_residual([(ya, w_out, 0), (yb, w_out, half_w)], l, xs)
        hf = rmsnorm(xs, norm_ffn[l], BF16)
        act = matmul_conv(hf, ffn_up, ffn_conv_w, ffn_conv_b, l, L, 0, ff, True, BF16)
        xs = matmul_residual([(act, ffn_down, 0)], l, xs)
    return rmsnorm(xs, final_norm, F32).reshape(bsz, L, d)
```

```python
import functools
import math

import numpy as np
import jax
import jax.numpy as jnp
from jax import lax
from jax.experimental import pallas as pl
from jax.experimental.pallas import tpu as pltpu

F32 = jnp.float32
BF16 = jnp.bfloat16

EPS = 1e-6
GRID_W = 64
HEAD_DIM = 128
CHUNK = 128
A_GROUPS = 8
NA_KH = 8
NA_KW = 16
D_HEADS = 8
ROPE_THETA = 10000.0
HYENA_ORDER = 2
HYENA_EMB = 33
HYENA_BANDS = (HYENA_EMB - 1) // 2
HYENA_FFN = 64
HYENA_TARGET = 1e-2
HYENA_FAST = 0.3
HYENA_SLOW = 1.5
MASK_VALUE = -1e30

VMEM_LIMIT_BYTES = 56 * 1024 * 1024
DFT_TILE = 512


def _params(*sem):
    return pltpu.CompilerParams(dimension_semantics=sem, vmem_limit_bytes=VMEM_LIMIT_BYTES)


def _tile(n, pref):
    t = min(n, pref)
    assert n % t == 0, (n, pref)
    return t


def _rmsnorm_kernel(x_ref, g_ref, o_ref):
    x = x_ref[...]
    ms = jnp.mean(x * x, axis=-1, keepdims=True)
    o_ref[...] = (x * lax.rsqrt(ms + EPS) * g_ref[...]).astype(o_ref.dtype)


def rmsnorm(x, g, out_dtype):
    m, d = x.shape
    tm = _tile(m, 512)
    return pl.pallas_call(
        _rmsnorm_kernel,
        grid=(m // tm,),
        in_specs=[pl.BlockSpec((tm, d), lambda i: (i, 0)), pl.BlockSpec((1, d), lambda i: (0, 0))],
        out_specs=pl.BlockSpec((tm, d), lambda i: (i, 0)),
        out_shape=jax.ShapeDtypeStruct((m, d), out_dtype),
        compiler_params=_params("parallel"),
        name="rmsnorm",
    )(x, g.reshape(1, d))


def _matmul_kernel(x_ref, w_ref, o_ref, wb_ref):
    @pl.when(pl.program_id(1) == 0)
    def _():
        wb_ref[...] = w_ref[...].astype(BF16)

    o_ref[...] = jnp.dot(x_ref[...], wb_ref[...], preferred_element_type=F32).astype(o_ref.dtype)


def matmul(x, w, layer, col0, ncols, out_dtype):
    m, k = x.shape
    tm = _tile(m, 1024)
    tn = _tile(ncols, 1024)
    assert col0 % tn == 0
    jb = col0 // tn
    return pl.pallas_call(
        _matmul_kernel,
        grid=(ncols // tn, m // tm),
        in_specs=[pl.BlockSpec((tm, k), lambda j, i: (i, 0)),
                  pl.BlockSpec((None, k, tn), lambda j, i: (layer, 0, jb + j))],
        out_specs=pl.BlockSpec((tm, tn), lambda j, i: (i, j)),
        out_shape=jax.ShapeDtypeStruct((m, ncols), out_dtype),
        scratch_shapes=[pltpu.VMEM((k, tn), BF16)],
        compiler_params=_params("parallel", "arbitrary"),
        name="matmul",
    )(x, w)


def _matmul_res_kernel(*refs, n_pairs):
    res_ref, o_ref = refs[2 * n_pairs], refs[2 * n_pairs + 1]
    wb_refs = refs[2 * n_pairs + 2:]

    @pl.when(pl.program_id(1) == 0)
    def _():
        for p in range(n_pairs):
            wb_refs[p][...] = refs[2 * p + 1][...].astype(BF16)

    acc = res_ref[...]
    for p in range(n_pairs):
        acc = acc + jnp.dot(refs[2 * p][...], wb_refs[p][...], preferred_element_type=F32)
    o_ref[...] = acc


def matmul_residual(pairs, layer, res):
    m, n = res.shape
    kmax = max(x.shape[1] for x, _, _ in pairs)
    tm = _tile(m, 1024 if kmax <= 2048 else 512)
    tn = _tile(n, 512)
    in_specs, args, scratch = [], [], []
    for x, w, row0 in pairs:
        k = x.shape[1]
        assert row0 % k == 0
        rb = row0 // k
        in_specs.append(pl.BlockSpec((tm, k), lambda j, i: (i, 0)))
        in_specs.append(pl.BlockSpec((None, k, tn), lambda j, i, rb=rb: (layer, rb, j)))
        scratch.append(pltpu.VMEM((k, tn), BF16))
        args += [x, w]
    in_specs.append(pl.BlockSpec((tm, tn), lambda j, i: (i, j)))
    args.append(res)
    return pl.pallas_call(
        functools.partial(_matmul_res_kernel, n_pairs=len(pairs)),
        grid=(n // tn, m // tm),
        in_specs=in_specs,
        out_specs=pl.BlockSpec((tm, tn), lambda j, i: (i, j)),
        out_shape=jax.ShapeDtypeStruct((m, n), F32),
        scratch_shapes=scratch,
        compiler_params=_params("parallel", "arbitrary"),
        name="matmul_residual",
    )(*args)


def _outproj_norm_kernel(xa_ref, wa_ref, xb_ref, wb_ref, res_ref, g_ref, o_ref, h_ref):
    acc = res_ref[...] + jnp.dot(xa_ref[...], wa_ref[...], preferred_element_type=F32)
    acc = acc + jnp.dot(xb_ref[...], wb_ref[...], preferred_element_type=F32)
    o_ref[...] = acc
    ms = jnp.mean(acc * acc, axis=-1, keepdims=True)
    h_ref[...] = (acc * lax.rsqrt(ms + EPS) * g_ref[...]).astype(h_ref.dtype)


def outproj_norm(xa, xb, w, layer, res, gain):
    m, n = res.shape
    k = xa.shape[1]
    tm = _tile(m, 512)
    return pl.pallas_call(
        _outproj_norm_kernel,
        grid=(m // tm,),
        in_specs=[pl.BlockSpec((tm, k), lambda i: (i, 0)),
                  pl.BlockSpec((None, k, n), lambda i: (layer, 0, 0)),
                  pl.BlockSpec((tm, k), lambda i: (i, 0)),
                  pl.BlockSpec((None, k, n), lambda i: (layer, 1, 0)),
                  pl.BlockSpec((tm, n), lambda i: (i, 0)),
                  pl.BlockSpec((1, n), lambda i: (0, 0))],
        out_specs=[pl.BlockSpec((tm, n), lambda i: (i, 0)), pl.BlockSpec((tm, n), lambda i: (i, 0))],
        out_shape=[jax.ShapeDtypeStruct((m, n), F32), jax.ShapeDtypeStruct((m, n), BF16)],
        compiler_params=_params("parallel"),
        name="outproj_norm",
    )(xa, w, xb, w, res, gain.reshape(1, n))


def _matmul_conv_kernel(*refs, glu, tm, seq_tiles):
    ncomp = 2 if glu else 1
    x_ref = refs[0]
    w_refs = refs[1:1 + ncomp]
    cw_refs = refs[1 + ncomp:1 + 2 * ncomp]
    b_refs = refs[1 + 2 * ncomp:1 + 3 * ncomp]
    o_ref = refs[1 + 3 * ncomp]
    wb_ref, acc_ref, carry_ref = refs[2 + 3 * ncomp:5 + 3 * ncomp]
    i = pl.program_id(1)
    slot = i % 2
    ps = 1 - slot

    @pl.when(i == 0)
    def _():
        for c in range(ncomp):
            wb_ref[c] = w_refs[c][...].astype(BF16)
        carry_ref[...] = jnp.zeros_like(carry_ref)
        acc_ref[1] = jnp.zeros(acc_ref.shape[1:], F32)

    x = x_ref[...]
    for c in range(ncomp):
        acc_ref[slot, c] = jnp.dot(x, wb_ref[c], preferred_element_type=F32)

    has_prev = (i + seq_tiles - 1) % seq_tiles != 0
    has_next = i % seq_tiles != 0
    outs = []
    for c in range(ncomp):
        a = acc_ref[ps, c]
        rows = lax.broadcasted_iota(jnp.int32, a.shape, 0)
        prev_row = jnp.where(has_prev, carry_ref[c], 0.0)
        next_row = jnp.where(has_next, acc_ref[slot, c, 0:1, :], 0.0)
        up = jnp.where(rows == 0, prev_row, pltpu.roll(a, 1, 0))
        dn = jnp.where(rows == tm - 1, next_row, pltpu.roll(a, tm - 1, 0))
        cw = cw_refs[c]
        outs.append(up * cw[0:1, :] + a * cw[1:2, :] + dn * cw[2:3, :] + b_refs[c][...])
        carry_ref[c] = a[tm - 1:tm, :]
    if glu:
        g, val = outs
        res = g * (1.0 / (1.0 + jnp.exp(-g))) * val
    else:
        res = outs[0]
    o_ref[...] = res.astype(o_ref.dtype)


def matmul_conv(x, w, cw, cb, layer, seq_len, col0, ncols, glu, out_dtype):
    m, k = x.shape
    tm = _tile(seq_len, 1024)
    tn = _tile(ncols, 512)
    nm = m // tm
    seq_tiles = seq_len // tm
    assert col0 % tn == 0
    ncomp = 2 if glu else 1
    coffs = [c * (ncols // tn) for c in range(ncomp)]
    woffs = [col0 // tn + o for o in coffs]
    cb3 = cb.reshape(cb.shape[0], 1, cb.shape[1])
    in_specs = [pl.BlockSpec((tm, k), lambda j, i: (jnp.minimum(i, nm - 1), 0))]
    in_specs += [pl.BlockSpec((None, k, tn), lambda j, i, o=o: (layer, 0, o + j)) for o in woffs]
    in_specs += [pl.BlockSpec((None, 3, tn), lambda j, i, o=o: (layer, 0, o + j)) for o in coffs]
    in_specs += [pl.BlockSpec((None, 1, tn), lambda j, i, o=o: (layer, 0, o + j)) for o in coffs]
    args = [x] + [w] * ncomp + [cw] * ncomp + [cb3] * ncomp
    return pl.pallas_call(
        functools.partial(_matmul_conv_kernel, glu=glu, tm=tm, seq_tiles=seq_tiles),
        grid=(ncols // tn, nm + 1),
        in_specs=in_specs,
        out_specs=pl.BlockSpec((tm, tn), lambda j, i: (jnp.maximum(i - 1, 0), j)),
        out_shape=jax.ShapeDtypeStruct((m, ncols), out_dtype),
        scratch_shapes=[pltpu.VMEM((ncomp, k, tn), BF16), pltpu.VMEM((2, ncomp, tm, tn), F32),
                        pltpu.VMEM((ncomp, 1, tn), F32)],
        compiler_params=_params("parallel", "arbitrary"),
        name="matmul_conv_glu" if glu else "matmul_conv",
    )(*args)


def _gmlp_kernel(p_ref, gain_ref, ws_ref, bs_ref, o_ref, *, tm, width):
    p = p_ref[...]
    g = 0.5 * p * (1.0 + lax.erf(p * (1.0 / math.sqrt(2.0))))
    u = g[:, :width]
    v = g[:, width:]
    ms = jnp.mean(v * v, axis=-1, keepdims=True)
    vb = (v * lax.rsqrt(ms + EPS) * gain_ref[...]).astype(BF16)
    gd = width // A_GROUPS
    for c in range(tm // CHUNK):
        r0 = c * CHUNK
        for gi in range(A_GROUPS):
            c0 = gi * gd
            s = jnp.dot(ws_ref[gi], vb[r0:r0 + CHUNK, c0:c0 + gd], preferred_element_type=F32)
            s = s + bs_ref[:, c0:c0 + gd]
            o_ref[r0:r0 + CHUNK, c0:c0 + gd] = (u[r0:r0 + CHUNK, c0:c0 + gd] * s).astype(o_ref.dtype)


def spatial_gating(p, v_gain, w_s, b_s):
    m, two_w = p.shape
    width = two_w // 2
    gd = width // A_GROUPS
    tm = _tile(m, 512)
    bs_full = jnp.repeat(b_s.T, gd, axis=1)
    return pl.pallas_call(
        functools.partial(_gmlp_kernel, tm=tm, width=width),
        grid=(m // tm,),
        in_specs=[pl.BlockSpec((tm, two_w), lambda i: (i, 0)),
                  pl.BlockSpec((1, width), lambda i: (0, 0)),
                  pl.BlockSpec((A_GROUPS, CHUNK, CHUNK), lambda i: (0, 0, 0)),
                  pl.BlockSpec((CHUNK, width), lambda i: (0, 0))],
        out_specs=pl.BlockSpec((tm, width), lambda i: (i, 0)),
        out_shape=jax.ShapeDtypeStruct((m, width), BF16),
        compiler_params=_params("parallel"),
        name="spatial_gating",
    )(p, v_gain.reshape(1, width), w_s.astype(BF16), bs_full)


def _dft_matrices(L):
    n = 2 * L
    r1 = 64
    r0 = L // r1
    t = np.arange(L, dtype=np.int64)[None, :]
    a1 = ((r0 * np.arange(r1, dtype=np.int64))[:, None] * t) % n
    a0 = (np.arange(r0, dtype=np.int64)[:, None] * t) % n
    ca, sa, cb, sb = lax.optimization_barrier((
        jnp.asarray(np.cos(2 * np.pi * a1 / n), F32)[:, None, :],
        jnp.asarray(np.sin(2 * np.pi * a1 / n), F32)[:, None, :],
        jnp.asarray(np.cos(2 * np.pi * a0 / n), F32)[None, :, :],
        jnp.asarray(np.sin(2 * np.pi * a0 / n), F32)[None, :, :]))
    cosm = (ca * cb - sa * sb).astype(BF16).reshape(L, L)
    sinm = (-(sa * cb + ca * sb)).astype(BF16).reshape(L, L)
    return cosm, sinm


def _filter_features(L):
    t = np.linspace(0.0, 1.0, L)[:, None]
    w = 2.0 * np.pi * np.arange(L)[:, None] / L
    bands = np.linspace(1e-4, HYENA_BANDS - 1, HYENA_BANDS)[None, :]
    z = np.concatenate([t, np.cos(w * bands), -np.sin(w * bands)], axis=-1)
    return jnp.asarray(np.pad(z, ((0, 0), (0, 128 - HYENA_EMB))), F32)


def _alternating(shape):
    rows = lax.broadcasted_iota(jnp.int32, shape, 0)
    return jnp.where(rows % 2 == 0, 1.0, -1.0)


def _filter_spectrum_kernel(z_ref, w1_ref, b1_ref, w2_ref, b2_ref, fr_ref, w3f_ref, w3b_ref,
                            dl_ref, cos_ref, sin_ref, o_ref, ke_ref, ko_ref, nyq_ref, *, n):
    m = pl.program_id(2)

    @pl.when(m == 0)
    def _():
        hp = lax.Precision.HIGHEST
        z = z_ref[...]
        h = jnp.dot(z, w1_ref[...], precision=hp, preferred_element_type=F32) + b1_ref[...]
        h = jnp.sin(fr_ref[0:1, :] * h)
        h = jnp.dot(h, w2_ref[...], precision=hp, preferred_element_type=F32) + b2_ref[...]
        h = jnp.sin(fr_ref[1:2, :] * h)
        decay = jnp.exp(-z[:, 0:1] * dl_ref[...])
        hf = jnp.dot(h, w3f_ref[...], precision=hp, preferred_element_type=F32) * decay
        hb = jnp.dot(h, w3b_ref[...], precision=hp, preferred_element_type=F32) * decay
        rows = lax.broadcasted_iota(jnp.int32, hb.shape, 0)
        hb = jnp.where(rows == 0, 0.0, hb)
        nrm = jnp.sum(jnp.abs(hf), axis=0, keepdims=True) + jnp.sum(jnp.abs(hb), axis=0, keepdims=True)
        inv = 1.0 / nrm
        ke = (hf + hb) * inv
        ke_ref[...] = ke.astype(BF16)
        ko_ref[...] = ((hf - hb) * inv).astype(BF16)
        nyq_ref[...] = jnp.sum(ke * _alternating(ke.shape), axis=0, keepdims=True) * (1.0 / n)

    kr = jnp.dot(cos_ref[...], ke_ref[...], preferred_element_type=F32)
    ki = jnp.dot(sin_ref[...], ko_ref[...], preferred_element_type=F32)
    first = jnp.logical_and(lax.broadcasted_iota(jnp.int32, kr.shape, 0) == 0, m == 0)
    scale = jnp.where(first, 1.0 / n, 2.0 / n)
    o_ref[0] = kr * scale
    o_ref[1] = jnp.where(first, nyq_ref[...], ki * scale)


def hyena_filter_spectrum(L, width, cosm, sinm, w1, b1, w2, b2, w3, freq):
    z = _filter_features(L)
    tc = _tile(width, 256)
    ncb = width // tc
    max_decay = math.log(HYENA_TARGET) / HYENA_FAST
    min_decay = math.log(HYENA_TARGET) / HYENA_SLOW
    deltas = jnp.asarray(np.abs(np.linspace(min_decay, max_decay, width)), F32).reshape(1, width)
    w1p = jnp.pad(w1, ((0, 128 - HYENA_EMB), (0, 0)))
    const = lambda o, c, m: (0, 0)
    tm = _tile(L, DFT_TILE)
    return pl.pallas_call(
        functools.partial(_filter_spectrum_kernel, n=2 * L),
        grid=(HYENA_ORDER, ncb, L // tm),
        in_specs=[pl.BlockSpec((L, 128), const),
                  pl.BlockSpec((128, HYENA_FFN), const), pl.BlockSpec((1, HYENA_FFN), const),
                  pl.BlockSpec((HYENA_FFN, HYENA_FFN), const), pl.BlockSpec((1, HYENA_FFN), const),
                  pl.BlockSpec((2, HYENA_FFN), const),
                  pl.BlockSpec((HYENA_FFN, tc), lambda o, c, m: (0, (2 * o) * ncb + c)),
                  pl.BlockSpec((HYENA_FFN, tc), lambda o, c, m: (0, (2 * o + 1) * ncb + c)),
                  pl.BlockSpec((1, tc), lambda o, c, m: (0, c)),
                  pl.BlockSpec((tm, L), lambda o, c, m: (m, 0)),
                  pl.BlockSpec((tm, L), lambda o, c, m: (m, 0))],
        out_specs=pl.BlockSpec((None, 2, tm, tc), lambda o, c, m: (o, 0, m, c)),
        out_shape=jax.ShapeDtypeStruct((HYENA_ORDER, 2, L, width), F32),
        scratch_shapes=[pltpu.VMEM((L, tc), BF16), pltpu.VMEM((L, tc), BF16), pltpu.VMEM((1, tc), F32)],
        compiler_params=_params("parallel", "parallel", "arbitrary"),
        name="hyena_filter_spectrum",
    )(z, w1p, b1.reshape(1, -1), w2, b2.reshape(1, -1), freq, w3, w3, deltas, cosm, sinm)


def _dft_forward_kernel(u_ref, cos_ref, sin_ref, kf_ref, o_ref, ub_ref, nyq_ref):
    m = pl.program_id(2)

    @pl.when(m == 0)
    def _():
        u = u_ref[...]
        ub_ref[...] = u.astype(BF16)
        nyq_ref[...] = jnp.sum(u * _alternating(u.shape), axis=0, keepdims=True)

    ub = ub_ref[...]
    ur = jnp.dot(cos_ref[...], ub, preferred_element_type=F32)
    ui = jnp.dot(sin_ref[...], ub, preferred_element_type=F32)
    kr, ki = kf_ref[0], kf_ref[1]
    yr = ur * kr - ui * ki
    yi = ur * ki + ui * kr
    first = jnp.logical_and(lax.broadcasted_iota(jnp.int32, yi.shape, 0) == 0, m == 0)
    yi = jnp.where(first, nyq_ref[...] * ki, yi)
    o_ref[0] = yr.astype(o_ref.dtype)
    o_ref[1] = yi.astype(o_ref.dtype)


def dft_forward_times_filter(u, col0, width, cosm, sinm, kf, order):
    bsz, L, _ = u.shape
    tc = _tile(width, 512)
    tm = _tile(L, DFT_TILE)
    assert col0 % tc == 0
    cb0 = col0 // tc
    return pl.pallas_call(
        _dft_forward_kernel,
        grid=(width // tc, bsz, L // tm),
        in_specs=[pl.BlockSpec((None, L, tc), lambda c, b, m: (b, 0, cb0 + c)),
                  pl.BlockSpec((tm, L), lambda c, b, m: (m, 0)),
                  pl.BlockSpec((tm, L), lambda c, b, m: (m, 0)),
                  pl.BlockSpec((None, 2, tm, tc), lambda c, b, m: (order, 0, m, c))],
        out_specs=pl.BlockSpec((None, 2, tm, tc), lambda c, b, m: (b, 0, m, c)),
        out_shape=jax.ShapeDtypeStruct((bsz, 2, L, width), BF16),
        scratch_shapes=[pltpu.VMEM((L, tc), BF16), pltpu.VMEM((1, tc), F32)],
        compiler_params=_params("parallel", "parallel", "arbitrary"),
        name="dft_forward",
    )(u, cosm, sinm, kf)


def _dft_inverse_kernel(y_ref, cos_ref, sin_ref, u_ref, g_ref, skip_ref, o_ref):
    conv = jnp.dot(cos_ref[...], y_ref[0], preferred_element_type=F32)
    conv = conv + jnp.dot(sin_ref[...], y_ref[1], preferred_element_type=F32)
    conv = conv + _alternating(conv.shape) * y_ref[1, 0:1, :].astype(F32)
    o_ref[...] = (g_ref[...] * (conv + u_ref[...] * skip_ref[...])).astype(o_ref.dtype)


def dft_inverse_gated(y, cosm, sinm, u, ucol0, gate, gcol0, skip, out_dtype):
    bsz, _, L, width = y.shape
    tc = _tile(width, 512)
    tm = _tile(L, DFT_TILE)
    ub0, gb0 = ucol0 // tc, gcol0 // tc
    return pl.pallas_call(
        _dft_inverse_kernel,
        grid=(width // tc, bsz, L // tm),
        in_specs=[pl.BlockSpec((None, 2, L, tc), lambda c, b, m: (b, 0, 0, c)),
                  pl.BlockSpec((tm, L), lambda c, b, m: (m, 0)),
                  pl.BlockSpec((tm, L), lambda c, b, m: (m, 0)),
                  pl.BlockSpec((None, tm, tc), lambda c, b, m: (b, m, ub0 + c)),
                  pl.BlockSpec((None, tm, tc), lambda c, b, m: (b, m, gb0 + c)),
                  pl.BlockSpec((1, tc), lambda c, b, m: (0, c))],
        out_specs=pl.BlockSpec((None, tm, tc), lambda c, b, m: (b, m, c)),
        out_shape=jax.ShapeDtypeStruct((bsz, L, width), out_dtype),
        compiler_params=_params("parallel", "parallel", "arbitrary"),
        name="dft_inverse",
    )(y, cosm, sinm, u, gate, skip.reshape(1, width))


NA_GROUP = 8
NA_BLOCK = NA_GROUP + NA_KH
NA_PAD = NA_KH // 2


def _na_plans():
    tiles = {}
    plans = []
    for variant in range(3):
        plan = []
        for rq in range(NA_GROUP):
            lo = (max(rq, NA_PAD), rq, min(rq, NA_PAD))[variant]
            a_lo, a_hi = lo // 2, (lo + NA_KH - 1) // 2 + 1
            ids = []
            for a in range(a_lo, a_hi):
                key = tuple(kr - rq + NA_KH - 1 - NA_PAD if lo <= kr < lo + NA_KH else -1
                            for kr in (2 * a, 2 * a + 1))
                ids.append(tiles.setdefault(key, len(tiles)))
            plan.append((a_lo, tuple(ids)))
        plans.append(tuple(plan))
    return tuple(plans), list(tiles)


def _na_kernel(q_ref, k_ref, v_ref, tiles_ref, o_ref, kb_ref, vb_ref, p_ref, *, rows, scale, plans):
    L = rows * GRID_W
    pad = NA_PAD * GRID_W
    gq = NA_GROUP * GRID_W
    gk = NA_BLOCK * GRID_W
    for ref, src in ((kb_ref, k_ref), (vb_ref, v_ref)):
        ref[0:pad, :] = jnp.zeros((pad, HEAD_DIM), BF16)
        ref[pad + L:, :] = jnp.zeros((pad, HEAD_DIM), BF16)
        ref[pad:pad + L, :] = src[...].astype(BF16)

    def group(q0, plan):
        q = q_ref[pl.ds(q0, gq), :].astype(BF16)
        s = lax.dot_general(q, kb_ref[pl.ds(q0, gk), :], (((1,), (1,)), ((), ())),
                            preferred_element_type=F32)
        p_ref[...] = jnp.zeros(p_ref.shape, BF16)
        for rq, (a_lo, ids) in enumerate(plan):
            r0, c0, c1 = rq * GRID_W, 2 * GRID_W * a_lo, 2 * GRID_W * (a_lo + len(ids))
            bias = jnp.concatenate([tiles_ref[t] for t in ids], axis=1)
            sl = s[r0:r0 + GRID_W, c0:c1] * scale + bias
            e = jnp.exp(sl - jnp.max(sl, axis=-1, keepdims=True))
            pr = e * (1.0 / jnp.sum(e, axis=-1, keepdims=True))
            p_ref[r0:r0 + GRID_W, c0:c1] = pr.astype(BF16)
        o = jnp.dot(p_ref[...], vb_ref[pl.ds(q0, gk), :], preferred_element_type=F32)
        o_ref[pl.ds(q0, gq), :] = o.astype(o_ref.dtype)

    ngroups = rows // NA_GROUP
    group(0, plans[0])

    def middle(g, carry):
        group(pl.multiple_of(g * gq, gq), plans[1])
        return carry

    lax.fori_loop(1, ngroups - 1, middle, 0)
    group((ngroups - 1) * gq, plans[2])


def _na_bias_tiles(rpb, tile_keys):
    col = np.arange(GRID_W)
    cs = np.clip(col - NA_KW // 2, 0, GRID_W - NA_KW)
    kc = np.arange(GRID_W)
    inwin = (kc[None, :] >= cs[:, None]) & (kc[None, :] < cs[:, None] + NA_KW)
    rel_col = np.clip(kc[None, :] - col[:, None] + (NA_KW - 1), 0, 2 * NA_KW - 2)
    t = jnp.where(jnp.asarray(inwin)[None, None], rpb[:, :, rel_col], MASK_VALUE)
    masked = jnp.full((rpb.shape[0], 1, GRID_W, GRID_W), MASK_VALUE, F32)
    t = jnp.concatenate([t.astype(F32), masked], axis=1)
    left = np.array([k[0] for k in tile_keys])
    right = np.array([k[1] for k in tile_keys])
    return jnp.concatenate([t[:, left], t[:, right]], axis=-1)


def neighborhood_attention(p, bsz, L, heads, rpb):
    rows = L // GRID_W
    assert rows % NA_GROUP == 0 and rows >= 2 * NA_GROUP
    plans, tile_keys = _na_plans()
    tiles = _na_bias_tiles(rpb, tile_keys)
    nt = len(tile_keys)
    lp = L + 2 * NA_PAD * GRID_W
    return pl.pallas_call(
        functools.partial(_na_kernel, rows=rows, scale=HEAD_DIM ** -0.5, plans=plans),
        grid=(bsz, heads),
        in_specs=[pl.BlockSpec((None, L, HEAD_DIM), lambda b, h: (b, 0, h)),
                  pl.BlockSpec((None, L, HEAD_DIM), lambda b, h: (b, 0, heads + h)),
                  pl.BlockSpec((None, L, HEAD_DIM), lambda b, h: (b, 0, 2 * heads + h)),
                  pl.BlockSpec((None, nt, GRID_W, 2 * GRID_W), lambda b, h: (h, 0, 0, 0))],
        out_specs=pl.BlockSpec((None, L, HEAD_DIM), lambda b, h: (b, 0, h)),
        out_shape=jax.ShapeDtypeStruct((bsz, L, heads * HEAD_DIM), BF16),
        scratch_shapes=[pltpu.VMEM((lp, HEAD_DIM), BF16), pltpu.VMEM((lp, HEAD_DIM), BF16),
                        pltpu.VMEM((NA_GROUP * GRID_W, NA_BLOCK * GRID_W), BF16)],
        compiler_params=_params("parallel", "parallel"),
        name="neighborhood_attention",
    )(p, p, p, tiles)


def _rope(x, cos, sin_signed, half):
    lane = lax.broadcasted_iota(jnp.int32, x.shape, 1)
    width = x.shape[1]
    first = (lane % (2 * half)) < half
    partner = jnp.where(first, pltpu.roll(x, width - half, 1), pltpu.roll(x, half, 1))
    return x * cos + partner * sin_signed


def _diff_attn_kernel(q_ref, k_ref, v_ref, cq_ref, sq_ref, ck_ref, sk_ref, lam_ref, sub_ref, o_ref,
                      kb_ref, vb_ref, *, lam_init, scale, half):
    qi = pl.program_id(2)

    @pl.when(qi == 0)
    def _():
        kb_ref[...] = _rope(k_ref[...], ck_ref[...], sk_ref[...], half).astype(BF16)
        vb_ref[...] = v_ref[...].astype(BF16)

    q = _rope(q_ref[...], cq_ref[...], sq_ref[...], half) * scale
    lane = lax.broadcasted_iota(jnp.int32, q.shape, 1)
    kb = kb_ref[...]
    lp = lam_ref[...]
    lam = (jnp.exp(jnp.sum(lp[0:1] * lp[1:2], axis=-1, keepdims=True))
           - jnp.exp(jnp.sum(lp[2:3] * lp[3:4], axis=-1, keepdims=True)) + lam_init)

    def softmax_map(in_map, weight):
        qm = jnp.where(in_map, q, 0.0).astype(BF16)
        s = lax.dot_general(qm, kb, (((1,), (1,)), ((), ())), preferred_element_type=F32)
        e = jnp.exp(s - jnp.max(s, axis=-1, keepdims=True))
        return e * (weight / jnp.sum(e, axis=-1, keepdims=True))

    a = softmax_map(lane < 2 * half, 1.0) - softmax_map(lane >= 2 * half, lam)
    o = jnp.dot(a.astype(BF16), vb_ref[...], preferred_element_type=F32)
    ms = jnp.mean(o * o, axis=-1, keepdims=True)
    o_ref[...] = (o * lax.rsqrt(ms + EPS) * sub_ref[...] * (1.0 - lam_init)).astype(o_ref.dtype)


def _rope_tables(L, half):
    inv = (1.0 / (np.float32(ROPE_THETA) ** (np.arange(half, dtype=np.float32) * np.float32(2.0)
                                             / np.float32(2 * half)))).astype(np.float32)
    ang = (np.arange(L, dtype=np.float32)[:, None] * inv[None, :]).astype(np.float64)
    cos, sin = np.cos(ang), np.sin(ang)
    reps = HEAD_DIM // (2 * half)
    cos_t = np.tile(cos, (1, 2 * reps))
    sin_t = np.tile(np.concatenate([-sin, sin], axis=1), (1, reps))
    return jnp.asarray(cos_t, F32), jnp.asarray(sin_t, F32)


def diff_attention(p, col0, bsz, L, heads, lam_params, subln, lam_init):
    half = HEAD_DIM // 4
    assert col0 % HEAD_DIM == 0
    c0 = col0 // HEAD_DIM
    tq = _tile(L, 256)
    cos_t, sin_t = _rope_tables(L, half)
    return pl.pallas_call(
        functools.partial(_diff_attn_kernel, lam_init=lam_init, scale=(2 * half) ** -0.5, half=half),
        grid=(bsz, heads, L // tq),
        in_specs=[pl.BlockSpec((None, tq, HEAD_DIM), lambda b, h, i: (b, i, c0 + h)),
                  pl.BlockSpec((None, L, HEAD_DIM), lambda b, h, i: (b, 0, c0 + heads + h)),
                  pl.BlockSpec((None, L, HEAD_DIM), lambda b, h, i: (b, 0, c0 + 2 * heads + h)),
                  pl.BlockSpec((tq, HEAD_DIM), lambda b, h, i: (i, 0)),
                  pl.BlockSpec((tq, HEAD_DIM), lambda b, h, i: (i, 0)),
                  pl.BlockSpec((L, HEAD_DIM), lambda b, h, i: (0, 0)),
                  pl.BlockSpec((L, HEAD_DIM), lambda b, h, i: (0, 0)),
                  pl.BlockSpec((4, 2 * half), lambda b, h, i: (0, 0)),
                  pl.BlockSpec((1, HEAD_DIM), lambda b, h, i: (0, 0))],
        out_specs=pl.BlockSpec((None, tq, HEAD_DIM), lambda b, h, i: (b, i, h)),
        out_shape=jax.ShapeDtypeStruct((bsz, L, heads * HEAD_DIM), BF16),
        scratch_shapes=[pltpu.VMEM((L, HEAD_DIM), BF16), pltpu.VMEM((L, HEAD_DIM), BF16)],
        compiler_params=_params("parallel", "parallel", "arbitrary"),
        name="diff_attention",
    )(p, p, p, cos_t, sin_t, cos_t, sin_t, lam_params, subln.reshape(1, HEAD_DIM))


def kernel(x, norm_mix, norm_ffn, w_out, ffn_up, ffn_conv_w, ffn_conv_b, ffn_down, final_norm,
           ab_w_in, a_vnorm, a_ws, a_bs, b_conv_w, b_conv_b, b_filt_w1, b_filt_b1, b_filt_w2,
           b_filt_b2, b_filt_w3, b_filt_freq, b_skip, cd_w_in, c_rpb, d_lambda, d_subln):
    bsz, L, d = x.shape
    depth = norm_mix.shape[0]
    m = bsz * L
    half_w = d // 2
    ff = ffn_down.shape[1]
    heads = half_w // HEAD_DIM
    assert heads == D_HEADS and half_w // A_GROUPS == CHUNK and L % GRID_W == 0
    xs = x.reshape(m, d)
    cosm, sinm = _dft_matrices(L)
    w_out_b = w_out.astype(BF16)
    for l in range(depth):
        i = l // 2
        h = rmsnorm(xs, norm_mix[l], BF16)
        if l % 2 == 0:
            pa = matmul(h, ab_w_in, i, 0, 2 * half_w, F32)
            ya = spatial_gating(pa, a_vnorm[i], a_ws[i], a_bs[i])
            pb = matmul_conv(h, ab_w_in, b_conv_w, b_conv_b, i, L, 2 * half_w, 3 * half_w, False, F32)
            pb = pb.reshape(bsz, L, 3 * half_w)
            kf = hyena_filter_spectrum(L, half_w, cosm, sinm, b_filt_w1[i], b_filt_b1[i], b_filt_w2[i],
                                       b_filt_b2[i], b_filt_w3[i], b_filt_freq[i])
            y1 = dft_forward_times_filter(pb, 0, half_w, cosm, sinm, kf, 0)
            z = dft_inverse_gated(y1, cosm, sinm, pb, 0, pb, half_w, b_skip[i, 0], F32)
            y2 = dft_forward_times_filter(z, 0, half_w, cosm, sinm, kf, 1)
            yb = dft_inverse_gated(y2, cosm, sinm, z, 0, pb, 2 * half_w, b_skip[i, 1], BF16)
            yb = yb.reshape(m, half_w)
        else:
            p = matmul(h, cd_w_in, i, 0, 6 * half_w, F32).reshape(bsz, L, 6 * half_w)
            lam_init = 0.8 - 0.6 * math.exp(-0.3 * l)
            ya = neighborhood_attention(p, bsz, L, heads, c_rpb[i]).reshape(m, half_w)
            yb = diff_attention(p, 3 * half_w, bsz, L, heads, d_lambda[i], d_subln[i], lam_init)
            yb = yb.reshape(m, half_w)
        xs, hf = outproj_norm(ya, yb, w_out_b, l, xs, norm_ffn[l])
        act = matmul_conv(hf, ffn_up, ffn_conv_w, ffn_conv_b, l, L, 0, ff, True, BF16)
        xs = matmul_residual([(act, ffn_down, 0)], l, xs)
    return rmsnorm(xs, final_norm, F32).reshape(bsz, L, d)
```

```python
import functools
import math

import numpy as np
import jax
import jax.numpy as jnp
from jax import lax
from jax.experimental import pallas as pl
from jax.experimental.pallas import tpu as pltpu

F32 = jnp.float32
BF16 = jnp.bfloat16

EPS = 1e-6
GRID_W = 64
HEAD_DIM = 128
CHUNK = 128
A_GROUPS = 8
NA_KH = 8
NA_KW = 16
D_HEADS = 8
ROPE_THETA = 10000.0
HYENA_ORDER = 2
HYENA_EMB = 33
HYENA_BANDS = (HYENA_EMB - 1) // 2
HYENA_FFN = 64
HYENA_TARGET = 1e-2
HYENA_FAST = 0.3
HYENA_SLOW = 1.5
MASK_VALUE = -1e30

VMEM_LIMIT_BYTES = 56 * 1024 * 1024
DFT_TILE = 512


def _params(*sem):
    return pltpu.CompilerParams(dimension_semantics=sem, vmem_limit_bytes=VMEM_LIMIT_BYTES)


def _tile(n, pref):
    t = min(n, pref)
    assert n % t == 0, (n, pref)
    return t


def _rmsnorm_kernel(x_ref, g_ref, o_ref):
    x = x_ref[...]
    ms = jnp.mean(x * x, axis=-1, keepdims=True)
    o_ref[...] = (x * lax.rsqrt(ms + EPS) * g_ref[...]).astype(o_ref.dtype)


def rmsnorm(x, g, out_dtype):
    m, d = x.shape
    tm = _tile(m, 512)
    return pl.pallas_call(
        _rmsnorm_kernel,
        grid=(m // tm,),
        in_specs=[pl.BlockSpec((tm, d), lambda i: (i, 0)), pl.BlockSpec((1, d), lambda i: (0, 0))],
        out_specs=pl.BlockSpec((tm, d), lambda i: (i, 0)),
        out_shape=jax.ShapeDtypeStruct((m, d), out_dtype),
        compiler_params=_params("parallel"),
        name="rmsnorm",
    )(x, g.reshape(1, d))


def _matmul_kernel(x_ref, w_ref, o_ref, wb_ref):
    @pl.when(pl.program_id(1) == 0)
    def _():
        wb_ref[...] = w_ref[...].astype(BF16)

    o_ref[...] = jnp.dot(x_ref[...], wb_ref[...], preferred_element_type=F32).astype(o_ref.dtype)


def matmul(x, w, layer, col0, ncols, out_dtype):
    m, k = x.shape
    tm = _tile(m, 1024)
    tn = _tile(ncols, 1024)
    assert col0 % tn == 0
    jb = col0 // tn
    return pl.pallas_call(
        _matmul_kernel,
        grid=(ncols // tn, m // tm),
        in_specs=[pl.BlockSpec((tm, k), lambda j, i: (i, 0)),
                  pl.BlockSpec((None, k, tn), lambda j, i: (layer, 0, jb + j))],
        out_specs=pl.BlockSpec((tm, tn), lambda j, i: (i, j)),
        out_shape=jax.ShapeDtypeStruct((m, ncols), out_dtype),
        scratch_shapes=[pltpu.VMEM((k, tn), BF16)],
        compiler_params=_params("parallel", "arbitrary"),
        name="matmul",
    )(x, w)


def _outproj_norm_kernel(xa_ref, wa_ref, xb_ref, wb_ref, res_ref, g_ref, o_ref, h_ref):
    acc = res_ref[...] + jnp.dot(xa_ref[...], wa_ref[...], preferred_element_type=F32)
    acc = acc + jnp.dot(xb_ref[...], wb_ref[...], preferred_element_type=F32)
    o_ref[...] = acc
    ms = jnp.mean(acc * acc, axis=-1, keepdims=True)
    h_ref[...] = (acc * lax.rsqrt(ms + EPS) * g_ref[...]).astype(h_ref.dtype)


def outproj_norm(xa, xb, w, layer, res, gain):
    m, n = res.shape
    k = xa.shape[1]
    tm = _tile(m, 512)
    return pl.pallas_call(
        _outproj_norm_kernel,
        grid=(m // tm,),
        in_specs=[pl.BlockSpec((tm, k), lambda i: (i, 0)),
                  pl.BlockSpec((None, k, n), lambda i: (layer, 0, 0)),
                  pl.BlockSpec((tm, k), lambda i: (i, 0)),
                  pl.BlockSpec((None, k, n), lambda i: (layer, 1, 0)),
                  pl.BlockSpec((tm, n), lambda i: (i, 0)),
                  pl.BlockSpec((1, n), lambda i: (0, 0))],
        out_specs=[pl.BlockSpec((tm, n), lambda i: (i, 0)), pl.BlockSpec((tm, n), lambda i: (i, 0))],
        out_shape=[jax.ShapeDtypeStruct((m, n), F32), jax.ShapeDtypeStruct((m, n), BF16)],
        compiler_params=_params("parallel"),
        name="outproj_norm",
    )(xa, w, xb, w, res, gain.reshape(1, n))


def _downproj_norm_kernel(x_ref, w_ref, res_ref, g_ref, o_ref, h_ref):
    acc = res_ref[...] + jnp.dot(x_ref[...], w_ref[...], preferred_element_type=F32)
    o_ref[...] = acc
    ms = jnp.mean(acc * acc, axis=-1, keepdims=True)
    h_ref[...] = (acc * lax.rsqrt(ms + EPS) * g_ref[...]).astype(h_ref.dtype)


def downproj_norm(x, w, layer, res, gain, norm_dtype):
    m, n = res.shape
    k = x.shape[1]
    tm = _tile(m, 256)
    return pl.pallas_call(
        _downproj_norm_kernel,
        grid=(m // tm,),
        in_specs=[pl.BlockSpec((tm, k), lambda i: (i, 0)),
                  pl.BlockSpec((None, k, n), lambda i: (layer, 0, 0), pipeline_mode=pl.Buffered(1)),
                  pl.BlockSpec((tm, n), lambda i: (i, 0)),
                  pl.BlockSpec((1, n), lambda i: (0, 0))],
        out_specs=[pl.BlockSpec((tm, n), lambda i: (i, 0)), pl.BlockSpec((tm, n), lambda i: (i, 0))],
        out_shape=[jax.ShapeDtypeStruct((m, n), F32), jax.ShapeDtypeStruct((m, n), norm_dtype)],
        compiler_params=_params("parallel"),
        name="downproj_norm",
    )(x, w, res, gain.reshape(1, n))


def _matmul_conv_kernel(*refs, glu, tm, seq_tiles):
    ncomp = 2 if glu else 1
    x_ref = refs[0]
    w_refs = refs[1:1 + ncomp]
    cw_refs = refs[1 + ncomp:1 + 2 * ncomp]
    b_refs = refs[1 + 2 * ncomp:1 + 3 * ncomp]
    o_ref = refs[1 + 3 * ncomp]
    wb_ref, acc_ref, carry_ref = refs[2 + 3 * ncomp:5 + 3 * ncomp]
    i = pl.program_id(1)
    slot = i % 2
    ps = 1 - slot

    tn = o_ref.shape[1]

    @pl.when(i == 0)
    def _():
        for c in range(ncomp):
            wb_ref[:, c * tn:(c + 1) * tn] = w_refs[c][...].astype(BF16)
        carry_ref[...] = jnp.zeros_like(carry_ref)
        acc_ref[1] = jnp.zeros(acc_ref.shape[1:], F32)

    acc_ref[slot] = jnp.dot(x_ref[...], wb_ref[...], preferred_element_type=F32)

    has_prev = (i + seq_tiles - 1) % seq_tiles != 0
    has_next = i % seq_tiles != 0
    outs = []
    for c in range(ncomp):
        a = acc_ref[ps, :, c * tn:(c + 1) * tn]
        rows = lax.broadcasted_iota(jnp.int32, a.shape, 0)
        prev_row = jnp.where(has_prev, carry_ref[c], 0.0)
        next_row = jnp.where(has_next, acc_ref[slot, 0:1, c * tn:(c + 1) * tn], 0.0)
        up = jnp.where(rows == 0, prev_row, pltpu.roll(a, 1, 0))
        dn = jnp.where(rows == tm - 1, next_row, pltpu.roll(a, tm - 1, 0))
        cw = cw_refs[c]
        outs.append(up * cw[0:1, :] + a * cw[1:2, :] + dn * cw[2:3, :] + b_refs[c][...])
        carry_ref[c] = a[tm - 1:tm, :]
    if glu:
        g, val = outs
        res = g * (1.0 / (1.0 + jnp.exp(-g))) * val
    else:
        res = outs[0]
    o_ref[...] = res.astype(o_ref.dtype)


def matmul_conv(x, w, cw, cb, layer, seq_len, col0, ncols, glu, out_dtype):
    m, k = x.shape
    tm = _tile(seq_len, 1024)
    tn = _tile(ncols, 512)
    nm = m // tm
    seq_tiles = seq_len // tm
    assert col0 % tn == 0
    ncomp = 2 if glu else 1
    coffs = [c * (ncols // tn) for c in range(ncomp)]
    woffs = [col0 // tn + o for o in coffs]
    cb3 = cb.reshape(cb.shape[0], 1, cb.shape[1])
    in_specs = [pl.BlockSpec((tm, k), lambda j, i: (jnp.minimum(i, nm - 1), 0))]
    in_specs += [pl.BlockSpec((None, k, tn), lambda j, i, o=o: (layer, 0, o + j)) for o in woffs]
    in_specs += [pl.BlockSpec((None, 3, tn), lambda j, i, o=o: (layer, 0, o + j)) for o in coffs]
    in_specs += [pl.BlockSpec((None, 1, tn), lambda j, i, o=o: (layer, 0, o + j)) for o in coffs]
    args = [x] + [w] * ncomp + [cw] * ncomp + [cb3] * ncomp
    return pl.pallas_call(
        functools.partial(_matmul_conv_kernel, glu=glu, tm=tm, seq_tiles=seq_tiles),
        grid=(ncols // tn, nm + 1),
        in_specs=in_specs,
        out_specs=pl.BlockSpec((tm, tn), lambda j, i: (jnp.maximum(i - 1, 0), j)),
        out_shape=jax.ShapeDtypeStruct((m, ncols), out_dtype),
        scratch_shapes=[pltpu.VMEM((k, ncomp * tn), BF16), pltpu.VMEM((2, tm, ncomp * tn), F32),
                        pltpu.VMEM((ncomp, 1, tn), F32)],
        compiler_params=_params("parallel", "arbitrary"),
        name="matmul_conv_glu" if glu else "matmul_conv",
    )(*args)


def _gmlp_kernel(p_ref, gain_ref, ws_ref, bs_ref, o_ref, *, tm, width):
    p = p_ref[...]
    g = 0.5 * p * (1.0 + lax.erf(p * (1.0 / math.sqrt(2.0))))
    u = g[:, :width]
    v = g[:, width:]
    ms = jnp.mean(v * v, axis=-1, keepdims=True)
    vb = (v * lax.rsqrt(ms + EPS) * gain_ref[...]).astype(BF16)
    gd = width // A_GROUPS
    for c in range(tm // CHUNK):
        r0 = c * CHUNK
        for gi in range(A_GROUPS):
            c0 = gi * gd
            s = jnp.dot(ws_ref[gi], vb[r0:r0 + CHUNK, c0:c0 + gd], preferred_element_type=F32)
            s = s + bs_ref[:, c0:c0 + gd]
            o_ref[r0:r0 + CHUNK, c0:c0 + gd] = (u[r0:r0 + CHUNK, c0:c0 + gd] * s).astype(o_ref.dtype)


def spatial_gating(p, v_gain, w_s, b_s):
    m, two_w = p.shape
    width = two_w // 2
    gd = width // A_GROUPS
    tm = _tile(m, 512)
    bs_full = jnp.repeat(b_s.T, gd, axis=1)
    return pl.pallas_call(
        functools.partial(_gmlp_kernel, tm=tm, width=width),
        grid=(m // tm,),
        in_specs=[pl.BlockSpec((tm, two_w), lambda i: (i, 0)),
                  pl.BlockSpec((1, width), lambda i: (0, 0)),
                  pl.BlockSpec((A_GROUPS, CHUNK, CHUNK), lambda i: (0, 0, 0)),
                  pl.BlockSpec((CHUNK, width), lambda i: (0, 0))],
        out_specs=pl.BlockSpec((tm, width), lambda i: (i, 0)),
        out_shape=jax.ShapeDtypeStruct((m, width), BF16),
        compiler_params=_params("parallel"),
        name="spatial_gating",
    )(p, v_gain.reshape(1, width), w_s.astype(BF16), bs_full)


def _dft_matrices(L):
    n = 2 * L
    r1 = 64
    r0 = L // r1
    t = np.arange(L, dtype=np.int64)[None, :]
    a1 = ((r0 * np.arange(r1, dtype=np.int64))[:, None] * t) % n
    a0 = (np.arange(r0, dtype=np.int64)[:, None] * t) % n
    ca, sa, cb, sb = lax.optimization_barrier((
        jnp.asarray(np.cos(2 * np.pi * a1 / n), F32)[:, None, :],
        jnp.asarray(np.sin(2 * np.pi * a1 / n), F32)[:, None, :],
        jnp.asarray(np.cos(2 * np.pi * a0 / n), F32)[None, :, :],
        jnp.asarray(np.sin(2 * np.pi * a0 / n), F32)[None, :, :]))
    cosm = (ca * cb - sa * sb).astype(BF16).reshape(L, L)
    sinm = (-(sa * cb + ca * sb)).astype(BF16).reshape(L, L)
    return cosm, sinm


def _filter_features(L):
    t = np.linspace(0.0, 1.0, L)[:, None]
    w = 2.0 * np.pi * np.arange(L)[:, None] / L
    bands = np.linspace(1e-4, HYENA_BANDS - 1, HYENA_BANDS)[None, :]
    z = np.concatenate([t, np.cos(w * bands), -np.sin(w * bands)], axis=-1)
    return jnp.asarray(np.pad(z, ((0, 0), (0, 128 - HYENA_EMB))), F32)


def _alternating(shape):
    rows = lax.broadcasted_iota(jnp.int32, shape, 0)
    return jnp.where(rows % 2 == 0, 1.0, -1.0)


def _filter_spectrum_kernel(z_ref, w1_ref, b1_ref, w2_ref, b2_ref, fr_ref, w3f_ref, w3b_ref,
                            dl_ref, cos_ref, sin_ref, o_ref, ke_ref, ko_ref, nyq_ref, h_ref, *, n):
    m = pl.program_id(2)
    hp = lax.Precision.HIGHEST

    @pl.when(jnp.logical_and(jnp.logical_and(pl.program_id(0) == 0, pl.program_id(1) == 0), m == 0))
    def _():
        h = jnp.dot(z_ref[...], w1_ref[...], precision=hp, preferred_element_type=F32) + b1_ref[...]
        h = jnp.sin(fr_ref[0:1, :] * h)
        h = jnp.dot(h, w2_ref[...], precision=hp, preferred_element_type=F32) + b2_ref[...]
        h_ref[...] = jnp.sin(fr_ref[1:2, :] * h)

    @pl.when(m == 0)
    def _():
        z = z_ref[...]
        h = h_ref[...]
        decay = jnp.exp(-z[:, 0:1] * dl_ref[...])
        hf = jnp.dot(h, w3f_ref[...], precision=hp, preferred_element_type=F32) * decay
        hb = jnp.dot(h, w3b_ref[...], precision=hp, preferred_element_type=F32) * decay
        rows = lax.broadcasted_iota(jnp.int32, hb.shape, 0)
        hb = jnp.where(rows == 0, 0.0, hb)
        nrm = jnp.sum(jnp.abs(hf), axis=0, keepdims=True) + jnp.sum(jnp.abs(hb), axis=0, keepdims=True)
        inv = 1.0 / nrm
        ke = (hf + hb) * inv
        ke_ref[...] = ke.astype(BF16)
        ko_ref[...] = ((hf - hb) * inv).astype(BF16)
        nyq_ref[...] = jnp.sum(ke * _alternating(ke.shape), axis=0, keepdims=True) * (1.0 / n)

    kr = jnp.dot(cos_ref[...], ke_ref[...], preferred_element_type=F32)
    ki = jnp.dot(sin_ref[...], ko_ref[...], preferred_element_type=F32)
    first = jnp.logical_and(lax.broadcasted_iota(jnp.int32, kr.shape, 0) == 0, m == 0)
    scale = jnp.where(first, 1.0 / n, 2.0 / n)
    o_ref[0] = kr * scale
    o_ref[1] = jnp.where(first, nyq_ref[...], ki * scale)


def hyena_filter_spectrum(L, width, cosm, sinm, w1, b1, w2, b2, w3, freq):
    z = _filter_features(L)
    tc = _tile(width, 256)
    ncb = width // tc
    max_decay = math.log(HYENA_TARGET) / HYENA_FAST
    min_decay = math.log(HYENA_TARGET) / HYENA_SLOW
    deltas = jnp.asarray(np.abs(np.linspace(min_decay, max_decay, width)), F32).reshape(1, width)
    w1p = jnp.pad(w1, ((0, 128 - HYENA_EMB), (0, 0)))
    const = lambda o, c, m: (0, 0)
    tm = _tile(L, DFT_TILE)
    return pl.pallas_call(
        functools.partial(_filter_spectrum_kernel, n=2 * L),
        grid=(HYENA_ORDER, ncb, L // tm),
        in_specs=[pl.BlockSpec((L, 128), const),
                  pl.BlockSpec((128, HYENA_FFN), const), pl.BlockSpec((1, HYENA_FFN), const),
                  pl.BlockSpec((HYENA_FFN, HYENA_FFN), const), pl.BlockSpec((1, HYENA_FFN), const),
                  pl.BlockSpec((2, HYENA_FFN), const),
                  pl.BlockSpec((HYENA_FFN, tc), lambda o, c, m: (0, (2 * o) * ncb + c)),
                  pl.BlockSpec((HYENA_FFN, tc), lambda o, c, m: (0, (2 * o + 1) * ncb + c)),
                  pl.BlockSpec((1, tc), lambda o, c, m: (0, c)),
                  pl.BlockSpec((tm, L), lambda o, c, m: (m, 0)),
                  pl.BlockSpec((tm, L), lambda o, c, m: (m, 0))],
        out_specs=pl.BlockSpec((None, 2, tm, tc), lambda o, c, m: (o, 0, m, c)),
        out_shape=jax.ShapeDtypeStruct((HYENA_ORDER, 2, L, width), F32),
        scratch_shapes=[pltpu.VMEM((L, tc), BF16), pltpu.VMEM((L, tc), BF16), pltpu.VMEM((1, tc), F32),
                        pltpu.VMEM((L, HYENA_FFN), F32)],
        compiler_params=_params("arbitrary", "arbitrary", "arbitrary"),
        name="hyena_filter_spectrum",
    )(z, w1p, b1.reshape(1, -1), w2, b2.reshape(1, -1), freq, w3, w3, deltas, cosm, sinm)


def _dft_forward_kernel(u_ref, cos_ref, sin_ref, kf_ref, o_ref, ub_ref, nyq_ref):
    m = pl.program_id(2)

    @pl.when(m == 0)
    def _():
        u = u_ref[...]
        ub_ref[...] = u.astype(BF16)
        nyq_ref[...] = jnp.sum(u * _alternating(u.shape), axis=0, keepdims=True)

    ub = ub_ref[...]
    ur = jnp.dot(cos_ref[...], ub, preferred_element_type=F32)
    ui = jnp.dot(sin_ref[...], ub, preferred_element_type=F32)
    kr, ki = kf_ref[0], kf_ref[1]
    yr = ur * kr - ui * ki
    yi = ur * ki + ui * kr
    first = jnp.logical_and(lax.broadcasted_iota(jnp.int32, yi.shape, 0) == 0, m == 0)
    yi = jnp.where(first, nyq_ref[...] * ki, yi)
    o_ref[0] = yr.astype(o_ref.dtype)
    o_ref[1] = yi.astype(o_ref.dtype)


def dft_forward_times_filter(u, col0, width, cosm, sinm, kf, order):
    bsz, L, _ = u.shape
    tc = _tile(width, 512)
    tm = _tile(L, DFT_TILE)
    assert col0 % tc == 0
    cb0 = col0 // tc
    return pl.pallas_call(
        _dft_forward_kernel,
        grid=(width // tc, bsz, L // tm),
        in_specs=[pl.BlockSpec((None, L, tc), lambda c, b, m: (b, 0, cb0 + c)),
                  pl.BlockSpec((tm, L), lambda c, b, m: (m, 0)),
                  pl.BlockSpec((tm, L), lambda c, b, m: (m, 0)),
                  pl.BlockSpec((None, 2, tm, tc), lambda c, b, m: (order, 0, m, c))],
        out_specs=pl.BlockSpec((None, 2, tm, tc), lambda c, b, m: (b, 0, m, c)),
        out_shape=jax.ShapeDtypeStruct((bsz, 2, L, width), BF16),
        scratch_shapes=[pltpu.VMEM((L, tc), BF16), pltpu.VMEM((1, tc), F32)],
        compiler_params=_params("parallel", "parallel", "arbitrary"),
        name="dft_forward",
    )(u, cosm, sinm, kf)


def _dft_inverse_kernel(y_ref, cos_ref, sin_ref, u_ref, g_ref, skip_ref, o_ref):
    conv = jnp.dot(cos_ref[...], y_ref[0], preferred_element_type=F32)
    conv = conv + jnp.dot(sin_ref[...], y_ref[1], preferred_element_type=F32)
    conv = conv + _alternating(conv.shape) * y_ref[1, 0:1, :].astype(F32)
    o_ref[...] = (g_ref[...] * (conv + u_ref[...] * skip_ref[...])).astype(o_ref.dtype)


def dft_inverse_gated(y, cosm, sinm, u, ucol0, gate, gcol0, skip, out_dtype):
    bsz, _, L, width = y.shape
    tc = _tile(width, 512)
    tm = _tile(L, DFT_TILE)
    ub0, gb0 = ucol0 // tc, gcol0 // tc
    return pl.pallas_call(
        _dft_inverse_kernel,
        grid=(width // tc, bsz, L // tm),
        in_specs=[pl.BlockSpec((None, 2, L, tc), lambda c, b, m: (b, 0, 0, c)),
                  pl.BlockSpec((tm, L), lambda c, b, m: (m, 0)),
                  pl.BlockSpec((tm, L), lambda c, b, m: (m, 0)),
                  pl.BlockSpec((None, tm, tc), lambda c, b, m: (b, m, ub0 + c)),
                  pl.BlockSpec((None, tm, tc), lambda c, b, m: (b, m, gb0 + c)),
                  pl.BlockSpec((1, tc), lambda c, b, m: (0, c))],
        out_specs=pl.BlockSpec((None, tm, tc), lambda c, b, m: (b, m, c)),
        out_shape=jax.ShapeDtypeStruct((bsz, L, width), out_dtype),
        compiler_params=_params("parallel", "parallel", "arbitrary"),
        name="dft_inverse",
    )(y, cosm, sinm, u, gate, skip.reshape(1, width))


NA_GROUP = 8
NA_BLOCK = NA_GROUP + NA_KH
NA_PAD = NA_KH // 2


def _na_plans():
    tiles = {}
    plans = []
    for variant in range(3):
        plan = []
        for rq in range(NA_GROUP):
            lo = (max(rq, NA_PAD), rq, min(rq, NA_PAD))[variant]
            a_lo, a_hi = lo // 2, (lo + NA_KH - 1) // 2 + 1
            ids = []
            for a in range(a_lo, a_hi):
                key = tuple(kr - rq + NA_KH - 1 - NA_PAD if lo <= kr < lo + NA_KH else -1
                            for kr in (2 * a, 2 * a + 1))
                ids.append(tiles.setdefault(key, len(tiles)))
            plan.append((a_lo, tuple(ids)))
        plans.append(tuple(plan))
    return tuple(plans), list(tiles)


def _na_kernel(q_ref, k_ref, v_ref, tiles_ref, o_ref, kb_ref, vb_ref, p_ref, *, rows, scale, plans):
    L = rows * GRID_W
    pad = NA_PAD * GRID_W
    gq = NA_GROUP * GRID_W
    gk = NA_BLOCK * GRID_W
    for ref, src in ((kb_ref, k_ref), (vb_ref, v_ref)):
        ref[0:pad, :] = jnp.zeros((pad, HEAD_DIM), BF16)
        ref[pad + L:, :] = jnp.zeros((pad, HEAD_DIM), BF16)
        ref[pad:pad + L, :] = src[...].astype(BF16)

    def group(q0, plan):
        q = q_ref[pl.ds(q0, gq), :].astype(BF16)
        s = lax.dot_general(q, kb_ref[pl.ds(q0, gk), :], (((1,), (1,)), ((), ())),
                            preferred_element_type=F32)
        p_ref[...] = jnp.zeros(p_ref.shape, BF16)
        for rq, (a_lo, ids) in enumerate(plan):
            r0, c0, c1 = rq * GRID_W, 2 * GRID_W * a_lo, 2 * GRID_W * (a_lo + len(ids))
            bias = jnp.concatenate([tiles_ref[t] for t in ids], axis=1)
            sl = s[r0:r0 + GRID_W, c0:c1] * scale + bias
            e = jnp.exp(sl - jnp.max(sl, axis=-1, keepdims=True))
            pr = e * (1.0 / jnp.sum(e, axis=-1, keepdims=True))
            p_ref[r0:r0 + GRID_W, c0:c1] = pr.astype(BF16)
        o = jnp.dot(p_ref[...], vb_ref[pl.ds(q0, gk), :], preferred_element_type=F32)
        o_ref[pl.ds(q0, gq), :] = o.astype(o_ref.dtype)

    ngroups = rows // NA_GROUP
    group(0, plans[0])

    def middle(g, carry):
        group(pl.multiple_of(g * gq, gq), plans[1])
        return carry

    lax.fori_loop(1, ngroups - 1, middle, 0)
    group((ngroups - 1) * gq, plans[2])


def _na_bias_tiles(rpb, tile_keys):
    col = np.arange(GRID_W)
    cs = np.clip(col - NA_KW // 2, 0, GRID_W - NA_KW)
    kc = np.arange(GRID_W)
    inwin = (kc[None, :] >= cs[:, None]) & (kc[None, :] < cs[:, None] + NA_KW)
    rel_col = np.clip(kc[None, :] - col[:, None] + (NA_KW - 1), 0, 2 * NA_KW - 2)
    t = jnp.where(jnp.asarray(inwin)[None, None], rpb[:, :, rel_col], MASK_VALUE)
    masked = jnp.full((rpb.shape[0], 1, GRID_W, GRID_W), MASK_VALUE, F32)
    t = jnp.concatenate([t.astype(F32), masked], axis=1)
    left = np.array([k[0] for k in tile_keys])
    right = np.array([k[1] for k in tile_keys])
    return jnp.concatenate([t[:, left], t[:, right]], axis=-1)


def neighborhood_attention(p, bsz, L, heads, rpb):
    rows = L // GRID_W
    assert rows % NA_GROUP == 0 and rows >= 2 * NA_GROUP
    plans, tile_keys = _na_plans()
    tiles = _na_bias_tiles(rpb, tile_keys)
    nt = len(tile_keys)
    lp = L + 2 * NA_PAD * GRID_W
    return pl.pallas_call(
        functools.partial(_na_kernel, rows=rows, scale=HEAD_DIM ** -0.5, plans=plans),
        grid=(bsz, heads),
        in_specs=[pl.BlockSpec((None, L, HEAD_DIM), lambda b, h: (b, 0, h)),
                  pl.BlockSpec((None, L, HEAD_DIM), lambda b, h: (b, 0, heads + h)),
                  pl.BlockSpec((None, L, HEAD_DIM), lambda b, h: (b, 0, 2 * heads + h)),
                  pl.BlockSpec((None, nt, GRID_W, 2 * GRID_W), lambda b, h: (h, 0, 0, 0))],
        out_specs=pl.BlockSpec((None, L, HEAD_DIM), lambda b, h: (b, 0, h)),
        out_shape=jax.ShapeDtypeStruct((bsz, L, heads * HEAD_DIM), BF16),
        scratch_shapes=[pltpu.VMEM((lp, HEAD_DIM), BF16), pltpu.VMEM((lp, HEAD_DIM), BF16),
                        pltpu.VMEM((NA_GROUP * GRID_W, NA_BLOCK * GRID_W), BF16)],
        compiler_params=_params("parallel", "parallel"),
        name="neighborhood_attention",
    )(p, p, p, tiles)


DIFF_SUB_ROWS = 256


def _rope(x, cos, sin_signed, half):
    lane = lax.broadcasted_iota(jnp.int32, x.shape, 1)
    width = x.shape[1]
    first = (lane % (2 * half)) < half
    partner = jnp.where(first, pltpu.roll(x, width - half, 1), pltpu.roll(x, half, 1))
    return x * cos + partner * sin_signed


def _diff_attn_kernel(q_ref, k_ref, v_ref, cq_ref, sq_ref, ck_ref, sk_ref, lam_ref, sub_ref, o_ref,
                      kb_ref, vb_ref, *, lam_init, scale, half, n_sub):
    qi = pl.program_id(2)

    @pl.when(qi == 0)
    def _():
        kb_ref[...] = _rope(k_ref[...], ck_ref[...], sk_ref[...], half).astype(BF16)
        vb_ref[...] = v_ref[...].astype(BF16)

    kb = kb_ref[...]
    lp = lam_ref[...]
    lam = (jnp.exp(jnp.sum(lp[0:1] * lp[1:2], axis=-1, keepdims=True))
           - jnp.exp(jnp.sum(lp[2:3] * lp[3:4], axis=-1, keepdims=True)) + lam_init)
    tq = q_ref.shape[0]
    sub = tq // n_sub
    for sb in range(n_sub):
        rs = slice(sb * sub, (sb + 1) * sub)
        q = _rope(q_ref[rs, :], cq_ref[rs, :], sq_ref[rs, :], half) * (scale * math.log2(math.e))
        lane = lax.broadcasted_iota(jnp.int32, q.shape, 1)

        def softmax_map(in_map, weight):
            qm = jnp.where(in_map, q, 0.0).astype(BF16)
            s = lax.dot_general(qm, kb, (((1,), (1,)), ((), ())), preferred_element_type=F32)
            e = jnp.exp2(s - jnp.max(s, axis=-1, keepdims=True))
            return e * (weight / jnp.sum(e, axis=-1, keepdims=True))

        a = softmax_map(lane < 2 * half, 1.0) - softmax_map(lane >= 2 * half, lam)
        o = jnp.dot(a.astype(BF16), vb_ref[...], preferred_element_type=F32)
        ms = jnp.mean(o * o, axis=-1, keepdims=True)
        o_ref[rs, :] = (o * lax.rsqrt(ms + EPS) * sub_ref[...] * (1.0 - lam_init)).astype(o_ref.dtype)


def _rope_tables(L, half):
    inv = (1.0 / (np.float32(ROPE_THETA) ** (np.arange(half, dtype=np.float32) * np.float32(2.0)
                                             / np.float32(2 * half)))).astype(np.float32)
    ang = (np.arange(L, dtype=np.float32)[:, None] * inv[None, :]).astype(np.float64)
    cos, sin = np.cos(ang), np.sin(ang)
    reps = HEAD_DIM // (2 * half)
    cos_t = np.tile(cos, (1, 2 * reps))
    sin_t = np.tile(np.concatenate([-sin, sin], axis=1), (1, reps))
    return jnp.asarray(cos_t, F32), jnp.asarray(sin_t, F32)


def diff_attention(p, col0, bsz, L, heads, lam_params, subln, lam_init):
    half = HEAD_DIM // 4
    assert col0 % HEAD_DIM == 0
    c0 = col0 // HEAD_DIM
    tq = _tile(L, 2 * DIFF_SUB_ROWS)
    cos_t, sin_t = _rope_tables(L, half)
    return pl.pallas_call(
        functools.partial(_diff_attn_kernel, lam_init=lam_init, scale=(2 * half) ** -0.5, half=half,
                          n_sub=tq // DIFF_SUB_ROWS),
        grid=(bsz, heads, L // tq),
        in_specs=[pl.BlockSpec((None, tq, HEAD_DIM), lambda b, h, i: (b, i, c0 + h)),
                  pl.BlockSpec((None, L, HEAD_DIM), lambda b, h, i: (b, 0, c0 + heads + h)),
                  pl.BlockSpec((None, L, HEAD_DIM), lambda b, h, i: (b, 0, c0 + 2 * heads + h)),
                  pl.BlockSpec((tq, HEAD_DIM), lambda b, h, i: (i, 0)),
                  pl.BlockSpec((tq, HEAD_DIM), lambda b, h, i: (i, 0)),
                  pl.BlockSpec((L, HEAD_DIM), lambda b, h, i: (0, 0)),
                  pl.BlockSpec((L, HEAD_DIM), lambda b, h, i: (0, 0)),
                  pl.BlockSpec((4, 2 * half), lambda b, h, i: (0, 0)),
                  pl.BlockSpec((1, HEAD_DIM), lambda b, h, i: (0, 0))],
        out_specs=pl.BlockSpec((None, tq, HEAD_DIM), lambda b, h, i: (b, i, h)),
        out_shape=jax.ShapeDtypeStruct((bsz, L, heads * HEAD_DIM), BF16),
        scratch_shapes=[pltpu.VMEM((L, HEAD_DIM), BF16), pltpu.VMEM((L, HEAD_DIM), BF16)],
        compiler_params=_params("parallel", "parallel", "arbitrary"),
        name="diff_attention",
    )(p, p, p, cos_t, sin_t, cos_t, sin_t, lam_params, subln.reshape(1, HEAD_DIM))


def kernel(x, norm_mix, norm_ffn, w_out, ffn_up, ffn_conv_w, ffn_conv_b, ffn_down, final_norm,
           ab_w_in, a_vnorm, a_ws, a_bs, b_conv_w, b_conv_b, b_filt_w1, b_filt_b1, b_filt_w2,
           b_filt_b2, b_filt_w3, b_filt_freq, b_skip, cd_w_in, c_rpb, d_lambda, d_subln):
    bsz, L, d = x.shape
    depth = norm_mix.shape[0]
    m = bsz * L
    half_w = d // 2
    ff = ffn_down.shape[1]
    heads = half_w // HEAD_DIM
    assert heads == D_HEADS and half_w // A_GROUPS == CHUNK and L % GRID_W == 0
    xs = x.reshape(m, d)
    cosm, sinm = _dft_matrices(L)
    w_out_b = w_out.astype(BF16)
    ffn_down_b = ffn_down.astype(BF16)
    h = rmsnorm(xs, norm_mix[0], BF16)
    for l in range(depth):
        i = l // 2
        if l % 2 == 0:
            pa = matmul(h, ab_w_in, i, 0, 2 * half_w, F32)
            ya = spatial_gating(pa, a_vnorm[i], a_ws[i], a_bs[i])
            pb = matmul_conv(h, ab_w_in, b_conv_w, b_conv_b, i, L, 2 * half_w, 3 * half_w, False, F32)
            pb = pb.reshape(bsz, L, 3 * half_w)
            kf = hyena_filter_spectrum(L, half_w, cosm, sinm, b_filt_w1[i], b_filt_b1[i], b_filt_w2[i],
                                       b_filt_b2[i], b_filt_w3[i], b_filt_freq[i])
            y1 = dft_forward_times_filter(pb, 0, half_w, cosm, sinm, kf, 0)
            z = dft_inverse_gated(y1, cosm, sinm, pb, 0, pb, half_w, b_skip[i, 0], F32)
            y2 = dft_forward_times_filter(z, 0, half_w, cosm, sinm, kf, 1)
            yb = dft_inverse_gated(y2, cosm, sinm, z, 0, pb, 2 * half_w, b_skip[i, 1], BF16)
            yb = yb.reshape(m, half_w)
        else:
            p = matmul(h, cd_w_in, i, 0, 6 * half_w, F32).reshape(bsz, L, 6 * half_w)
            lam_init = 0.8 - 0.6 * math.exp(-0.3 * l)
            ya = neighborhood_attention(p, bsz, L, heads, c_rpb[i]).reshape(m, half_w)
            yb = diff_attention(p, 3 * half_w, bsz, L, heads, d_lambda[i], d_subln[i], lam_init)
            yb = yb.reshape(m, half_w)
        xs, hf = outproj_norm(ya, yb, w_out_b, l, xs, norm_ffn[l])
        act = matmul_conv(hf, ffn_up, ffn_conv_w, ffn_conv_b, l, L, 0, ff, True, BF16)
        last = l == depth - 1
        xs, h = downproj_norm(act, ffn_down_b, l, xs, final_norm if last else norm_mix[l + 1],
                              F32 if last else BF16)
    return h.reshape(bsz, L, d)
```

```python
import functools
import math

import numpy as np
import jax
import jax.numpy as jnp
from jax import lax
from jax.experimental import pallas as pl
from jax.experimental.pallas import tpu as pltpu

F32 = jnp.float32
BF16 = jnp.bfloat16

EPS = 1e-6
GRID_W = 64
HEAD_DIM = 128
CHUNK = 128
A_GROUPS = 8
NA_KH = 8
NA_KW = 16
D_HEADS = 8
ROPE_THETA = 10000.0
HYENA_ORDER = 2
HYENA_EMB = 33
HYENA_BANDS = (HYENA_EMB - 1) // 2
HYENA_FFN = 64
HYENA_TARGET = 1e-2
HYENA_FAST = 0.3
HYENA_SLOW = 1.5
MASK_VALUE = -1e30

VMEM_LIMIT_BYTES = 56 * 1024 * 1024
DFT_TILE = 512
LANES = 128


def _params(*sem):
    return pltpu.CompilerParams(dimension_semantics=sem, vmem_limit_bytes=VMEM_LIMIT_BYTES)


def _tile(n, pref):
    t = min(n, pref)
    assert n % t == 0, (n, pref)
    return t


def _rmsnorm_kernel(x_ref, g_ref, o_ref):
    x = x_ref[...]
    ms = jnp.mean(x * x, axis=-1, keepdims=True)
    o_ref[...] = (x * lax.rsqrt(ms + EPS) * g_ref[...]).astype(o_ref.dtype)


def rmsnorm(x, g, out_dtype):
    m, d = x.shape
    tm = _tile(m, 512)
    return pl.pallas_call(
        _rmsnorm_kernel,
        grid=(m // tm,),
        in_specs=[pl.BlockSpec((tm, d), lambda i: (i, 0)), pl.BlockSpec((1, d), lambda i: (0, 0))],
        out_specs=pl.BlockSpec((tm, d), lambda i: (i, 0)),
        out_shape=jax.ShapeDtypeStruct((m, d), out_dtype),
        compiler_params=_params("parallel"),
        name="rmsnorm",
    )(x, g.reshape(1, d))


def _matmul_kernel(x_ref, w_ref, o_ref, wb_ref):
    @pl.when(pl.program_id(1) == 0)
    def _():
        wb_ref[...] = w_ref[...].astype(BF16)

    o_ref[...] = jnp.dot(x_ref[...], wb_ref[...], preferred_element_type=F32).astype(o_ref.dtype)


def matmul(x, w, layer, col0, ncols, out_dtype):
    m, k = x.shape
    tm = _tile(m, 1024)
    tn = _tile(ncols, 1024)
    assert col0 % tn == 0
    jb = col0 // tn
    return pl.pallas_call(
        _matmul_kernel,
        grid=(ncols // tn, m // tm),
        in_specs=[pl.BlockSpec((tm, k), lambda j, i: (i, 0)),
                  pl.BlockSpec((None, k, tn), lambda j, i: (layer, 0, jb + j))],
        out_specs=pl.BlockSpec((tm, tn), lambda j, i: (i, j)),
        out_shape=jax.ShapeDtypeStruct((m, ncols), out_dtype),
        scratch_shapes=[pltpu.VMEM((k, tn), BF16)],
        compiler_params=_params("parallel", "arbitrary"),
        name="matmul",
    )(x, w)


def _outproj_norm_kernel(xa_ref, wa_ref, xb_ref, wb_ref, res_ref, g_ref, o_ref, h_ref):
    acc = res_ref[...] + jnp.dot(xa_ref[...], wa_ref[...], preferred_element_type=F32)
    acc = acc + jnp.dot(xb_ref[...], wb_ref[...], preferred_element_type=F32)
    o_ref[...] = acc
    ms = jnp.mean(acc * acc, axis=-1, keepdims=True)
    h_ref[...] = (acc * lax.rsqrt(ms + EPS) * g_ref[...]).astype(h_ref.dtype)


def outproj_norm(xa, xb, w, layer, res, gain):
    m, n = res.shape
    k = xa.shape[1]
    tm = _tile(m, 512)
    return pl.pallas_call(
        _outproj_norm_kernel,
        grid=(m // tm,),
        in_specs=[pl.BlockSpec((tm, k), lambda i: (i, 0)),
                  pl.BlockSpec((None, k, n), lambda i: (layer, 0, 0)),
                  pl.BlockSpec((tm, k), lambda i: (i, 0)),
                  pl.BlockSpec((None, k, n), lambda i: (layer, 1, 0)),
                  pl.BlockSpec((tm, n), lambda i: (i, 0)),
                  pl.BlockSpec((1, n), lambda i: (0, 0))],
        out_specs=[pl.BlockSpec((tm, n), lambda i: (i, 0)), pl.BlockSpec((tm, n), lambda i: (i, 0))],
        out_shape=[jax.ShapeDtypeStruct((m, n), F32), jax.ShapeDtypeStruct((m, n), BF16)],
        compiler_params=_params("parallel"),
        name="outproj_norm",
    )(xa, w, xb, w, res, gain.reshape(1, n))


def _downproj_norm_kernel(x_ref, w_ref, res_ref, g_ref, o_ref, h_ref):
    acc = res_ref[...] + jnp.dot(x_ref[...], w_ref[...], preferred_element_type=F32)
    o_ref[...] = acc
    ms = jnp.mean(acc * acc, axis=-1, keepdims=True)
    h_ref[...] = (acc * lax.rsqrt(ms + EPS) * g_ref[...]).astype(h_ref.dtype)


def downproj_norm(x, w, layer, res, gain, norm_dtype):
    m, n = res.shape
    k = x.shape[1]
    tm = _tile(m, 256)
    return pl.pallas_call(
        _downproj_norm_kernel,
        grid=(m // tm,),
        in_specs=[pl.BlockSpec((tm, k), lambda i: (i, 0)),
                  pl.BlockSpec((None, k, n), lambda i: (layer, 0, 0), pipeline_mode=pl.Buffered(1)),
                  pl.BlockSpec((tm, n), lambda i: (i, 0)),
                  pl.BlockSpec((1, n), lambda i: (0, 0))],
        out_specs=[pl.BlockSpec((tm, n), lambda i: (i, 0)), pl.BlockSpec((tm, n), lambda i: (i, 0))],
        out_shape=[jax.ShapeDtypeStruct((m, n), F32), jax.ShapeDtypeStruct((m, n), norm_dtype)],
        compiler_params=_params("parallel"),
        name="downproj_norm",
    )(x, w, res, gain.reshape(1, n))


def _matmul_conv_kernel(*refs, glu, tm, seq_tiles):
    ncomp = 2 if glu else 1
    x_ref = refs[0]
    w_refs = refs[1:1 + ncomp]
    cw_refs = refs[1 + ncomp:1 + 2 * ncomp]
    b_refs = refs[1 + 2 * ncomp:1 + 3 * ncomp]
    o_ref = refs[1 + 3 * ncomp]
    wb_ref, acc_ref, carry_ref = refs[2 + 3 * ncomp:5 + 3 * ncomp]
    i = pl.program_id(1)
    slot = i % 2
    ps = 1 - slot

    tn = o_ref.shape[1]

    @pl.when(i == 0)
    def _():
        for c in range(ncomp):
            wb_ref[:, c * tn:(c + 1) * tn] = w_refs[c][...].astype(BF16)
        carry_ref[...] = jnp.zeros_like(carry_ref)
        acc_ref[1] = jnp.zeros(acc_ref.shape[1:], F32)

    acc_ref[slot] = jnp.dot(x_ref[...], wb_ref[...], preferred_element_type=F32)

    has_prev = (i + seq_tiles - 1) % seq_tiles != 0
    has_next = i % seq_tiles != 0
    outs = []
    for c in range(ncomp):
        a = acc_ref[ps, :, c * tn:(c + 1) * tn]
        rows = lax.broadcasted_iota(jnp.int32, a.shape, 0)
        prev_row = jnp.where(has_prev, carry_ref[c], 0.0)
        next_row = jnp.where(has_next, acc_ref[slot, 0:1, c * tn:(c + 1) * tn], 0.0)
        up = jnp.where(rows == 0, prev_row, pltpu.roll(a, 1, 0))
        dn = jnp.where(rows == tm - 1, next_row, pltpu.roll(a, tm - 1, 0))
        cw = cw_refs[c]
        outs.append(up * cw[0:1, :] + a * cw[1:2, :] + dn * cw[2:3, :] + b_refs[c][...])
        carry_ref[c] = a[tm - 1:tm, :]
    if glu:
        g, val = outs
        res = g * (1.0 / (1.0 + jnp.exp(-g))) * val
    else:
        res = outs[0]
    o_ref[...] = res.astype(o_ref.dtype)


def matmul_conv(x, w, cw, cb, layer, seq_len, col0, ncols, glu, out_dtype):
    m, k = x.shape
    tm = _tile(seq_len, 1024)
    tn = _tile(ncols, 512 if glu else 1024)
    nm = m // tm
    seq_tiles = seq_len // tm
    assert col0 % tn == 0
    ncomp = 2 if glu else 1
    coffs = [c * (ncols // tn) for c in range(ncomp)]
    woffs = [col0 // tn + o for o in coffs]
    cb3 = cb.reshape(cb.shape[0], 1, cb.shape[1])
    in_specs = [pl.BlockSpec((tm, k), lambda j, i: (jnp.minimum(i, nm - 1), 0))]
    in_specs += [pl.BlockSpec((None, k, tn), lambda j, i, o=o: (layer, 0, o + j)) for o in woffs]
    in_specs += [pl.BlockSpec((None, 3, tn), lambda j, i, o=o: (layer, 0, o + j)) for o in coffs]
    in_specs += [pl.BlockSpec((None, 1, tn), lambda j, i, o=o: (layer, 0, o + j)) for o in coffs]
    args = [x] + [w] * ncomp + [cw] * ncomp + [cb3] * ncomp
    return pl.pallas_call(
        functools.partial(_matmul_conv_kernel, glu=glu, tm=tm, seq_tiles=seq_tiles),
        grid=(ncols // tn, nm + 1),
        in_specs=in_specs,
        out_specs=pl.BlockSpec((tm, tn), lambda j, i: (jnp.maximum(i - 1, 0), j)),
        out_shape=jax.ShapeDtypeStruct((m, ncols), out_dtype),
        scratch_shapes=[pltpu.VMEM((k, ncomp * tn), BF16), pltpu.VMEM((2, tm, ncomp * tn), F32),
                        pltpu.VMEM((ncomp, 1, tn), F32)],
        compiler_params=_params("parallel", "arbitrary"),
        name="matmul_conv_glu" if glu else "matmul_conv",
    )(*args)


def _gmlp_kernel(p_ref, gain_ref, ws_ref, bs_ref, o_ref, *, tm, width):
    p = p_ref[...]
    g = 0.5 * p * (1.0 + lax.erf(p * (1.0 / math.sqrt(2.0))))
    u = g[:, :width]
    v = g[:, width:]
    ms = jnp.mean(v * v, axis=-1, keepdims=True)
    vb = (v * lax.rsqrt(ms + EPS) * gain_ref[...]).astype(BF16)
    gd = width // A_GROUPS
    for c in range(tm // CHUNK):
        r0 = c * CHUNK
        for gi in range(A_GROUPS):
            c0 = gi * gd
            s = jnp.dot(ws_ref[gi], vb[r0:r0 + CHUNK, c0:c0 + gd], preferred_element_type=F32)
            s = s + bs_ref[:, c0:c0 + gd]
            o_ref[r0:r0 + CHUNK, c0:c0 + gd] = (u[r0:r0 + CHUNK, c0:c0 + gd] * s).astype(o_ref.dtype)


def spatial_gating(p, v_gain, w_s, b_s):
    m, two_w = p.shape
    width = two_w // 2
    gd = width // A_GROUPS
    tm = _tile(m, 512)
    bs_full = jnp.repeat(b_s.T, gd, axis=1)
    return pl.pallas_call(
        functools.partial(_gmlp_kernel, tm=tm, width=width),
        grid=(m // tm,),
        in_specs=[pl.BlockSpec((tm, two_w), lambda i: (i, 0)),
                  pl.BlockSpec((1, width), lambda i: (0, 0)),
                  pl.BlockSpec((A_GROUPS, CHUNK, CHUNK), lambda i: (0, 0, 0)),
                  pl.BlockSpec((CHUNK, width), lambda i: (0, 0))],
        out_specs=pl.BlockSpec((tm, width), lambda i: (i, 0)),
        out_shape=jax.ShapeDtypeStruct((m, width), BF16),
        compiler_params=_params("parallel"),
        name="spatial_gating",
    )(p, v_gain.reshape(1, width), w_s.astype(BF16), bs_full)


def _dft_matrices(L):
    n = 2 * L
    r1 = 64
    r0 = L // r1
    t = np.arange(L, dtype=np.int64)[None, :]
    a1 = ((r0 * np.arange(r1, dtype=np.int64))[:, None] * t) % n
    a0 = (np.arange(r0, dtype=np.int64)[:, None] * t) % n
    ca, sa, cb, sb = lax.optimization_barrier((
        jnp.asarray(np.cos(2 * np.pi * a1 / n), F32)[:, None, :],
        jnp.asarray(np.sin(2 * np.pi * a1 / n), F32)[:, None, :],
        jnp.asarray(np.cos(2 * np.pi * a0 / n), F32)[None, :, :],
        jnp.asarray(np.sin(2 * np.pi * a0 / n), F32)[None, :, :]))
    cosm = (ca * cb - sa * sb).astype(BF16).reshape(L, L)
    sinm = (-(sa * cb + ca * sb)).astype(BF16).reshape(L, L)
    return cosm, sinm


def _filter_features(L):
    t = np.linspace(0.0, 1.0, L)[:, None]
    w = 2.0 * np.pi * np.arange(L)[:, None] / L
    bands = np.linspace(1e-4, HYENA_BANDS - 1, HYENA_BANDS)[None, :]
    z = np.concatenate([t, np.cos(w * bands), -np.sin(w * bands)], axis=-1)
    return jnp.asarray(np.pad(z, ((0, 0), (0, 128 - HYENA_EMB))), F32)


def _alternating(shape):
    rows = lax.broadcasted_iota(jnp.int32, shape, 0)
    return jnp.where(rows % 2 == 0, 1.0, -1.0)


def _twiddles(L):
    ang = 2.0 * np.pi * np.arange(L // 2)[:, None] / (2 * L)
    ones = np.ones((1, LANES))
    return jnp.asarray(np.cos(ang) * ones, F32), jnp.asarray(np.sin(ang) * ones, F32)


def _filter_spectrum_kernel(z_ref, w1_ref, b1_ref, w2_ref, b2_ref, fr_ref, w3f_ref, w3b_ref,
                            dl_ref, cos_ref, sin_ref, o_ref, ke_ref, ko_ref, nyq_ref, h_ref, taps_ref, *, n):
    m = pl.program_id(2)
    hp = lax.Precision.HIGHEST

    @pl.when(jnp.logical_and(jnp.logical_and(pl.program_id(0) == 0, pl.program_id(1) == 0), m == 0))
    def _():
        h = jnp.dot(z_ref[...], w1_ref[...], precision=hp, preferred_element_type=F32) + b1_ref[...]
        h = jnp.sin(fr_ref[0:1, :] * h)
        h = jnp.dot(h, w2_ref[...], precision=hp, preferred_element_type=F32) + b2_ref[...]
        h_ref[...] = jnp.sin(fr_ref[1:2, :] * h)

    @pl.when(m == 0)
    def _():
        length = z_ref.shape[0]
        chunk = min(length, 2 * DFT_TILE)
        nrm = None
        for r0 in range(0, length, chunk):
            rs = slice(r0, r0 + chunk)
            h = h_ref[rs, :]
            decay = jnp.exp(-z_ref[rs, 0:1] * dl_ref[...])
            hf = jnp.dot(h, w3f_ref[...], precision=hp, preferred_element_type=F32) * decay
            hb = jnp.dot(h, w3b_ref[...], precision=hp, preferred_element_type=F32) * decay
            if r0 == 0:
                hb = jnp.where(lax.broadcasted_iota(jnp.int32, hb.shape, 0) == 0, 0.0, hb)
            part = jnp.sum(jnp.abs(hf) + jnp.abs(hb), axis=0, keepdims=True)
            nrm = part if nrm is None else nrm + part
            taps_ref[0, rs, :] = hf + hb
            taps_ref[1, rs, :] = hf - hb
        inv = 1.0 / nrm
        qc, qs = None, None
        for r0 in range(0, length, chunk):
            rs = slice(r0, r0 + chunk)
            ke = taps_ref[0, rs, :] * inv
            ko = taps_ref[1, rs, :] * inv
            alt = _alternating(ke.shape)
            ke_ref[0, rs, :] = ke.astype(BF16)
            ko_ref[0, rs, :] = ko.astype(BF16)
            ke_ref[1, rs, :] = (ke * alt).astype(BF16)
            ko_ref[1, rs, :] = (-ko * alt).astype(BF16)
            phase = lax.broadcasted_iota(jnp.int32, ke.shape, 0) % 4
            quarter_cos = jnp.where(phase == 0, 1.0, jnp.where(phase == 2, -1.0, 0.0))
            quarter_sin = jnp.where(phase == 1, 1.0, jnp.where(phase == 3, -1.0, 0.0))
            pc = jnp.sum(ke * quarter_cos, axis=0, keepdims=True)
            psn = jnp.sum(ko * quarter_sin, axis=0, keepdims=True)
            qc = pc if qc is None else qc + pc
            qs = psn if qs is None else qs + psn
        nyq_ref[0:1, :] = qc * (2.0 / n)
        nyq_ref[1:2, :] = -qs * (2.0 / n)

    cos_t, sin_t = cos_ref[...], sin_ref[...]
    first = jnp.logical_and(lax.broadcasted_iota(jnp.int32, (cos_t.shape[0], ke_ref.shape[2]), 0) == 0,
                            m == 0)
    scale = jnp.where(first, 1.0 / n, 2.0 / n)
    for half in range(2):
        kr = jnp.dot(cos_t, ke_ref[half], preferred_element_type=F32)
        ki = jnp.dot(sin_t, ko_ref[half], preferred_element_type=F32)
        o_ref[2 * half] = kr * scale
        o_ref[2 * half + 1] = jnp.where(first, nyq_ref[half:half + 1, :], ki * scale)


def hyena_filter_spectrum(L, width, cosm, sinm, w1, b1, w2, b2, w3, freq):
    z = _filter_features(L)
    tc = _tile(width, 256)
    ncb = width // tc
    max_decay = math.log(HYENA_TARGET) / HYENA_FAST
    min_decay = math.log(HYENA_TARGET) / HYENA_SLOW
    deltas = jnp.asarray(np.abs(np.linspace(min_decay, max_decay, width)), F32).reshape(1, width)
    w1p = jnp.pad(w1, ((0, 128 - HYENA_EMB), (0, 0)))
    const = lambda o, c, m: (0, 0)
    tm = _tile(L // 2, DFT_TILE)
    return pl.pallas_call(
        functools.partial(_filter_spectrum_kernel, n=2 * L),
        grid=(HYENA_ORDER, ncb, L // 2 // tm),
        in_specs=[pl.BlockSpec((L, 128), const),
                  pl.BlockSpec((128, HYENA_FFN), const), pl.BlockSpec((1, HYENA_FFN), const),
                  pl.BlockSpec((HYENA_FFN, HYENA_FFN), const), pl.BlockSpec((1, HYENA_FFN), const),
                  pl.BlockSpec((2, HYENA_FFN), const),
                  pl.BlockSpec((HYENA_FFN, tc), lambda o, c, m: (0, (2 * o) * ncb + c)),
                  pl.BlockSpec((HYENA_FFN, tc), lambda o, c, m: (0, (2 * o + 1) * ncb + c)),
                  pl.BlockSpec((1, tc), lambda o, c, m: (0, c)),
                  pl.BlockSpec((tm, L), lambda o, c, m: (m, 0)),
                  pl.BlockSpec((tm, L), lambda o, c, m: (m, 0))],
        out_specs=pl.BlockSpec((None, 4, tm, tc), lambda o, c, m: (o, 0, m, c)),
        out_shape=jax.ShapeDtypeStruct((HYENA_ORDER, 4, L // 2, width), F32),
        scratch_shapes=[pltpu.VMEM((2, L, tc), BF16), pltpu.VMEM((2, L, tc), BF16),
                        pltpu.VMEM((2, tc), F32), pltpu.VMEM((L, HYENA_FFN), F32),
                        pltpu.VMEM((2, L, tc), F32)],
        compiler_params=_params("arbitrary", "arbitrary", "arbitrary"),
        name="hyena_filter_spectrum",
    )(z, w1p, b1.reshape(1, -1), w2, b2.reshape(1, -1), freq, w3, w3, deltas, cosm, sinm)


def _dft_forward_kernel(u_ref, cos_ref, sin_ref, twc_ref, tws_ref, kf_ref, o_ref, ub_ref):
    hl = ub_ref.shape[0]
    nyq = []
    for par in range(2):
        up = u_ref[pl.ds(par, hl, stride=2), :]
        ub_ref[:, par * LANES:(par + 1) * LANES] = up.astype(BF16)
        nyq.append(jnp.sum(up * _alternating(up.shape), axis=0, keepdims=True))
    en, on = nyq
    chunk = min(hl, DFT_TILE)
    for r0 in range(0, hl, chunk):
        rs = slice(r0, r0 + chunk)
        re = jnp.dot(cos_ref[rs, :], ub_ref[...], preferred_element_type=F32)
        im = jnp.dot(sin_ref[rs, :], ub_ref[...], preferred_element_type=F32)
        er, orr = re[:, :LANES], re[:, LANES:]
        ei, oi = im[:, :LANES], im[:, LANES:]
        c, s = twc_ref[rs, :], tws_ref[rs, :]
        pr = orr * c + oi * s
        pi = oi * c - orr * s
        ar, ai, br, bi = er + pr, ei + pi, er - pr, pi - ei
        kar, kai, kbr, kbi = kf_ref[0, rs, :], kf_ref[1, rs, :], kf_ref[2, rs, :], kf_ref[3, rs, :]
        yar = ar * kar - ai * kai
        yai = ar * kai + ai * kar
        ybr = br * kbr - bi * kbi
        ybi = br * kbi + bi * kbr
        if r0 == 0:
            first = lax.broadcasted_iota(jnp.int32, yai.shape, 0) == 0
            yai = jnp.where(first, en * kai + on * kbi, yai)
            ybi = jnp.where(first, en * kbi - on * kai, ybi)
        o_ref[0, rs, :] = yar.astype(o_ref.dtype)
        o_ref[1, rs, :] = yai.astype(o_ref.dtype)
        o_ref[2, rs, :] = ybr.astype(o_ref.dtype)
        o_ref[3, rs, :] = ybi.astype(o_ref.dtype)


def _resident(shape):
    return pl.BlockSpec(shape, lambda c, b: (0,) * len(shape), pipeline_mode=pl.Buffered(1))


def dft_forward_times_filter(u, col0, width, cos2, sin2, twc, tws, kf, order):
    bsz, L, _ = u.shape
    hl = L // 2
    assert col0 % LANES == 0
    cb0 = col0 // LANES
    return pl.pallas_call(
        _dft_forward_kernel,
        grid=(width // LANES, bsz),
        in_specs=[pl.BlockSpec((None, L, LANES), lambda c, b: (b, 0, cb0 + c)),
                  _resident((hl, hl)), _resident((hl, hl)),
                  _resident((hl, LANES)), _resident((hl, LANES)),
                  pl.BlockSpec((None, 4, hl, LANES), lambda c, b: (order, 0, 0, c))],
        out_specs=pl.BlockSpec((None, 4, hl, LANES), lambda c, b: (b, 0, 0, c)),
        out_shape=jax.ShapeDtypeStruct((bsz, 4, hl, width), BF16),
        scratch_shapes=[pltpu.VMEM((hl, 2 * LANES), BF16)],
        compiler_params=_params("parallel", "parallel"),
        name="dft_forward",
    )(u, cos2, sin2, twc, tws, kf)


def _dft_inverse_kernel(y_ref, cos_ref, sin_ref, twc_ref, tws_ref, u_ref, g_ref, skip_ref, o_ref,
                        eo_ref, res_ref):
    hl = eo_ref.shape[1]
    chunk = min(hl, DFT_TILE)
    for r0 in range(0, hl, chunk):
        rs = slice(r0, r0 + chunk)
        yar, yai = y_ref[0, rs, :].astype(F32), y_ref[1, rs, :].astype(F32)
        ybr, ybi = y_ref[2, rs, :].astype(F32), y_ref[3, rs, :].astype(F32)
        c, s = twc_ref[rs, :], tws_ref[rs, :]
        pr, pi = yar - ybr, yai + ybi
        eo_ref[0, rs, :LANES] = (yar + ybr).astype(BF16)
        eo_ref[0, rs, LANES:] = (pr * c - pi * s).astype(BF16)
        eo_ref[1, rs, :LANES] = (yai - ybi).astype(BF16)
        eo_ref[1, rs, LANES:] = (pr * s + pi * c).astype(BF16)
    nyq = jnp.concatenate([y_ref[1, 0:1, :].astype(F32), -y_ref[3, 0:1, :].astype(F32)], axis=1)
    skip = skip_ref[...]
    alt = _alternating((chunk, 2 * LANES))
    for r0 in range(0, hl, chunk):
        rs = slice(r0, r0 + chunk)
        conv = jnp.dot(cos_ref[rs, :], eo_ref[0], preferred_element_type=F32)
        conv = conv + jnp.dot(sin_ref[rs, :], eo_ref[1], preferred_element_type=F32)
        conv = conv + alt * nyq
        for par in range(2):
            rows = pl.ds(2 * r0 + par, chunk, stride=2)
            res_ref[rows, :] = g_ref[rows, :] * (conv[:, par * LANES:(par + 1) * LANES]
                                                 + u_ref[rows, :] * skip)
    o_ref[...] = res_ref[...].astype(o_ref.dtype)


def dft_inverse_gated(y, cos2, sin2, twc, tws, u, ucol0, gate, gcol0, skip, out_dtype):
    bsz, _, hl, width = y.shape
    L = 2 * hl
    ub0, gb0 = ucol0 // LANES, gcol0 // LANES
    return pl.pallas_call(
        _dft_inverse_kernel,
        grid=(width // LANES, bsz),
        in_specs=[pl.BlockSpec((None, 4, hl, LANES), lambda c, b: (b, 0, 0, c)),
                  _resident((hl, hl)), _resident((hl, hl)),
                  _resident((hl, LANES)), _resident((hl, LANES)),
                  pl.BlockSpec((None, L, LANES), lambda c, b: (b, 0, ub0 + c)),
                  pl.BlockSpec((None, L, LANES), lambda c, b: (b, 0, gb0 + c)),
                  pl.BlockSpec((1, LANES), lambda c, b: (0, c))],
        out_specs=pl.BlockSpec((None, L, LANES), lambda c, b: (b, 0, c)),
        out_shape=jax.ShapeDtypeStruct((bsz, L, width), out_dtype),
        scratch_shapes=[pltpu.VMEM((2, hl, 2 * LANES), BF16), pltpu.VMEM((L, LANES), F32)],
        compiler_params=_params("parallel", "parallel"),
        name="dft_inverse",
    )(y, cos2, sin2, twc, tws, u, gate, skip.reshape(1, width))


NA_GROUP = 8
NA_BLOCK = NA_GROUP + NA_KH
NA_PAD = NA_KH // 2


def _na_plans():
    tiles = {}
    plans = []
    for variant in range(3):
        plan = []
        for rq in range(NA_GROUP):
            lo = (max(rq, NA_PAD), rq, min(rq, NA_PAD))[variant]
            a_lo, a_hi = lo // 2, (lo + NA_KH - 1) // 2 + 1
            ids = []
            for a in range(a_lo, a_hi):
                key = tuple(kr - rq + NA_KH - 1 - NA_PAD if lo <= kr < lo + NA_KH else -1
                            for kr in (2 * a, 2 * a + 1))
                ids.append(tiles.setdefault(key, len(tiles)))
            plan.append((a_lo, tuple(ids)))
        plans.append(tuple(plan))
    return tuple(plans), list(tiles)


def _na_kernel(q_ref, k_ref, v_ref, tiles_ref, o_ref, kb_ref, vb_ref, p_ref, *, rows, scale, plans):
    L = rows * GRID_W
    pad = NA_PAD * GRID_W
    gq = NA_GROUP * GRID_W
    gk = NA_BLOCK * GRID_W
    for ref, src in ((kb_ref, k_ref), (vb_ref, v_ref)):
        ref[0:pad, :] = jnp.zeros((pad, HEAD_DIM), BF16)
        ref[pad + L:, :] = jnp.zeros((pad, HEAD_DIM), BF16)
        ref[pad:pad + L, :] = src[...].astype(BF16)

    def group(q0, plan):
        q = q_ref[pl.ds(q0, gq), :].astype(BF16)
        s = lax.dot_general(q, kb_ref[pl.ds(q0, gk), :], (((1,), (1,)), ((), ())),
                            preferred_element_type=F32)
        p_ref[...] = jnp.zeros(p_ref.shape, BF16)
        for rq, (a_lo, ids) in enumerate(plan):
            r0, c0, c1 = rq * GRID_W, 2 * GRID_W * a_lo, 2 * GRID_W * (a_lo + len(ids))
            bias = jnp.concatenate([tiles_ref[t] for t in ids], axis=1)
            sl = s[r0:r0 + GRID_W, c0:c1] * scale + bias
            e = jnp.exp(sl - jnp.max(sl, axis=-1, keepdims=True))
            pr = e * (1.0 / jnp.sum(e, axis=-1, keepdims=True))
            p_ref[r0:r0 + GRID_W, c0:c1] = pr.astype(BF16)
        o = jnp.dot(p_ref[...], vb_ref[pl.ds(q0, gk), :], preferred_element_type=F32)
        o_ref[pl.ds(q0, gq), :] = o.astype(o_ref.dtype)

    ngroups = rows // NA_GROUP
    group(0, plans[0])

    def middle(g, carry):
        group(pl.multiple_of(g * gq, gq), plans[1])
        return carry

    lax.fori_loop(1, ngroups - 1, middle, 0)
    group((ngroups - 1) * gq, plans[2])


def _na_bias_tiles(rpb, tile_keys):
    col = np.arange(GRID_W)
    cs = np.clip(col - NA_KW // 2, 0, GRID_W - NA_KW)
    kc = np.arange(GRID_W)
    inwin = (kc[None, :] >= cs[:, None]) & (kc[None, :] < cs[:, None] + NA_KW)
    rel_col = np.clip(kc[None, :] - col[:, None] + (NA_KW - 1), 0, 2 * NA_KW - 2)
    t = jnp.where(jnp.asarray(inwin)[None, None], rpb[:, :, rel_col], MASK_VALUE)
    masked = jnp.full((rpb.shape[0], 1, GRID_W, GRID_W), MASK_VALUE, F32)
    t = jnp.concatenate([t.astype(F32), masked], axis=1)
    left = np.array([k[0] for k in tile_keys])
    right = np.array([k[1] for k in tile_keys])
    return jnp.concatenate([t[:, left], t[:, right]], axis=-1)


def neighborhood_attention(p, bsz, L, heads, rpb):
    rows = L // GRID_W
    assert rows % NA_GROUP == 0 and rows >= 2 * NA_GROUP
    plans, tile_keys = _na_plans()
    tiles = _na_bias_tiles(rpb, tile_keys)
    nt = len(tile_keys)
    lp = L + 2 * NA_PAD * GRID_W
    return pl.pallas_call(
        functools.partial(_na_kernel, rows=rows, scale=HEAD_DIM ** -0.5, plans=plans),
        grid=(bsz, heads),
        in_specs=[pl.BlockSpec((None, L, HEAD_DIM), lambda b, h: (b, 0, h)),
                  pl.BlockSpec((None, L, HEAD_DIM), lambda b, h: (b, 0, heads + h)),
                  pl.BlockSpec((None, L, HEAD_DIM), lambda b, h: (b, 0, 2 * heads + h)),
                  pl.BlockSpec((None, nt, GRID_W, 2 * GRID_W), lambda b, h: (h, 0, 0, 0))],
        out_specs=pl.BlockSpec((None, L, HEAD_DIM), lambda b, h: (b, 0, h)),
        out_shape=jax.ShapeDtypeStruct((bsz, L, heads * HEAD_DIM), BF16),
        scratch_shapes=[pltpu.VMEM((lp, HEAD_DIM), BF16), pltpu.VMEM((lp, HEAD_DIM), BF16),
                        pltpu.VMEM((NA_GROUP * GRID_W, NA_BLOCK * GRID_W), BF16)],
        compiler_params=_params("parallel", "parallel"),
        name="neighborhood_attention",
    )(p, p, p, tiles)


DIFF_SUB_ROWS = 256


def _rope(x, cos, sin_signed, half):
    lane = lax.broadcasted_iota(jnp.int32, x.shape, 1)
    width = x.shape[1]
    first = (lane % (2 * half)) < half
    partner = jnp.where(first, pltpu.roll(x, width - half, 1), pltpu.roll(x, half, 1))
    return x * cos + partner * sin_signed


def _diff_attn_kernel(q_ref, k_ref, v_ref, cq_ref, sq_ref, ck_ref, sk_ref, lam_ref, sub_ref, o_ref,
                      kb_ref, vb_ref, *, lam_init, scale, half, n_sub):
    qi = pl.program_id(2)

    @pl.when(qi == 0)
    def _():
        kb_ref[...] = _rope(k_ref[...], ck_ref[...], sk_ref[...], half).astype(BF16)
        vb_ref[...] = v_ref[...].astype(BF16)

    lp = lam_ref[...]
    lam = (jnp.exp(jnp.sum(lp[0:1] * lp[1:2], axis=-1, keepdims=True))
           - jnp.exp(jnp.sum(lp[2:3] * lp[3:4], axis=-1, keepdims=True)) + lam_init)
    kb = kb_ref[...]
    tq = q_ref.shape[0]
    sub = tq // n_sub
    for sb in range(n_sub):
        rs = slice(sb * sub, (sb + 1) * sub)
        q = _rope(q_ref[rs, :], cq_ref[rs, :], sq_ref[rs, :], half) * (scale * math.log2(math.e))
        lane = lax.broadcasted_iota(jnp.int32, q.shape, 1)

        def softmax_map(in_map, weight):
            qm = jnp.where(in_map, q, 0.0).astype(BF16)
            s = lax.dot_general(qm, kb, (((1,), (1,)), ((), ())), preferred_element_type=F32)
            e = jnp.exp2(s - jnp.max(s, axis=-1, keepdims=True))
            return e * (weight / jnp.sum(e, axis=-1, keepdims=True))

        a = softmax_map(lane < 2 * half, 1.0) - softmax_map(lane >= 2 * half, lam)
        o = jnp.dot(a.astype(BF16), vb_ref[...], preferred_element_type=F32)
        ms = jnp.mean(o * o, axis=-1, keepdims=True)
        o_ref[rs, :] = (o * lax.rsqrt(ms + EPS) * sub_ref[...] * (1.0 - lam_init)).astype(o_ref.dtype)


def _rope_tables(L, half):
    inv = (1.0 / (np.float32(ROPE_THETA) ** (np.arange(half, dtype=np.float32) * np.float32(2.0)
                                             / np.float32(2 * half)))).astype(np.float32)
    ang = (np.arange(L, dtype=np.float32)[:, None] * inv[None, :]).astype(np.float64)
    cos, sin = np.cos(ang), np.sin(ang)
    reps = HEAD_DIM // (2 * half)
    cos_t = np.tile(cos, (1, 2 * reps))
    sin_t = np.tile(np.concatenate([-sin, sin], axis=1), (1, reps))
    return jnp.asarray(cos_t, F32), jnp.asarray(sin_t, F32)


def diff_attention(p, col0, bsz, L, heads, lam_params, subln, lam_init):
    half = HEAD_DIM // 4
    assert col0 % HEAD_DIM == 0
    c0 = col0 // HEAD_DIM
    tq = _tile(L, 2 * DIFF_SUB_ROWS)
    n_sub = tq // DIFF_SUB_ROWS
    cos_t, sin_t = _rope_tables(L, half)
    return pl.pallas_call(
        functools.partial(_diff_attn_kernel, lam_init=lam_init, scale=(2 * half) ** -0.5, half=half,
                          n_sub=n_sub),
        grid=(bsz, heads, L // tq),
        in_specs=[pl.BlockSpec((None, tq, HEAD_DIM), lambda b, h, i: (b, i, c0 + h)),
                  pl.BlockSpec((None, L, HEAD_DIM), lambda b, h, i: (b, 0, c0 + heads + h)),
                  pl.BlockSpec((None, L, HEAD_DIM), lambda b, h, i: (b, 0, c0 + 2 * heads + h)),
                  pl.BlockSpec((tq, HEAD_DIM), lambda b, h, i: (i, 0)),
                  pl.BlockSpec((tq, HEAD_DIM), lambda b, h, i: (i, 0)),
                  pl.BlockSpec((L, HEAD_DIM), lambda b, h, i: (0, 0)),
                  pl.BlockSpec((L, HEAD_DIM), lambda b, h, i: (0, 0)),
                  pl.BlockSpec((4, 2 * half), lambda b, h, i: (0, 0)),
                  pl.BlockSpec((1, HEAD_DIM), lambda b, h, i: (0, 0))],
        out_specs=pl.BlockSpec((None, tq, HEAD_DIM), lambda b, h, i: (b, i, h)),
        out_shape=jax.ShapeDtypeStruct((bsz, L, heads * HEAD_DIM), BF16),
        scratch_shapes=[pltpu.VMEM((L, HEAD_DIM), BF16), pltpu.VMEM((L, HEAD_DIM), BF16)],
        compiler_params=_params("parallel", "parallel", "arbitrary"),
        name="diff_attention",
    )(p, p, p, cos_t, sin_t, cos_t, sin_t, lam_params, subln.reshape(1, HEAD_DIM))


def kernel(x, norm_mix, norm_ffn, w_out, ffn_up, ffn_conv_w, ffn_conv_b, ffn_down, final_norm,
           ab_w_in, a_vnorm, a_ws, a_bs, b_conv_w, b_conv_b, b_filt_w1, b_filt_b1, b_filt_w2,
           b_filt_b2, b_filt_w3, b_filt_freq, b_skip, cd_w_in, c_rpb, d_lambda, d_subln):
    bsz, L, d = x.shape
    depth = norm_mix.shape[0]
    m = bsz * L
    half_w = d // 2
    ff = ffn_down.shape[1]
    heads = half_w // HEAD_DIM
    assert heads == D_HEADS and half_w // A_GROUPS == CHUNK and L % GRID_W == 0
    xs = x.reshape(m, d)
    cosm, sinm = _dft_matrices(L)
    cos2, sin2 = _dft_matrices(L // 2)
    twc, tws = _twiddles(L)
    w_out_b = w_out.astype(BF16)
    ffn_down_b = ffn_down.astype(BF16)
    h = rmsnorm(xs, norm_mix[0], BF16)
    for l in range(depth):
        i = l // 2
        if l % 2 == 0:
            pa = matmul(h, ab_w_in, i, 0, 2 * half_w, F32)
            ya = spatial_gating(pa, a_vnorm[i], a_ws[i], a_bs[i])
            pb = matmul_conv(h, ab_w_in, b_conv_w, b_conv_b, i, L, 2 * half_w, 3 * half_w, False, F32)
            pb = pb.reshape(bsz, L, 3 * half_w)
            kf = hyena_filter_spectrum(L, half_w, cosm, sinm, b_filt_w1[i], b_filt_b1[i], b_filt_w2[i],
                                       b_filt_b2[i], b_filt_w3[i], b_filt_freq[i])
            y1 = dft_forward_times_filter(pb, 0, half_w, cos2, sin2, twc, tws, kf, 0)
            z = dft_inverse_gated(y1, cos2, sin2, twc, tws, pb, 0, pb, half_w, b_skip[i, 0], F32)
            y2 = dft_forward_times_filter(z, 0, half_w, cos2, sin2, twc, tws, kf, 1)
            yb = dft_inverse_gated(y2, cos2, sin2, twc, tws, z, 0, pb, 2 * half_w, b_skip[i, 1], BF16)
            yb = yb.reshape(m, half_w)
        else:
            p = matmul(h, cd_w_in, i, 0, 6 * half_w, F32).reshape(bsz, L, 6 * half_w)
            lam_init = 0.8 - 0.6 * math.exp(-0.3 * l)
            ya = neighborhood_attention(p, bsz, L, heads, c_rpb[i]).reshape(m, half_w)
            yb = diff_attention(p, 3 * half_w, bsz, L, heads, d_lambda[i], d_subln[i], lam_init)
            yb = yb.reshape(m, half_w)
        xs, hf = outproj_norm(ya, yb, w_out_b, l, xs, norm_ffn[l])
        act = matmul_conv(hf, ffn_up, ffn_conv_w, ffn_conv_b, l, L, 0, ff, True, BF16)
        last = l == depth - 1
        xs, h = downproj_norm(act, ffn_down_b, l, xs, final_norm if last else norm_mix[l + 1],
                              F32 if last else BF16)
    return h.reshape(bsz, L, d)
```

```python
import functools
import math

import numpy as np
import jax
import jax.numpy as jnp
from jax import lax
from jax.experimental import pallas as pl
from jax.experimental.pallas import tpu as pltpu

F32 = jnp.float32
BF16 = jnp.bfloat16

EPS = 1e-6
GRID_W = 64
HEAD_DIM = 128
CHUNK = 128
A_GROUPS = 8
NA_KH = 8
NA_KW = 16
D_HEADS = 8
ROPE_THETA = 10000.0
HYENA_ORDER = 2
HYENA_EMB = 33
HYENA_BANDS = (HYENA_EMB - 1) // 2
HYENA_FFN = 64
HYENA_TARGET = 1e-2
HYENA_FAST = 0.3
HYENA_SLOW = 1.5
MASK_VALUE = -1e30

VMEM_LIMIT_BYTES = 56 * 1024 * 1024
DFT_TILE = 512
LANES = 128


def _params(*sem):
    return pltpu.CompilerParams(dimension_semantics=sem, vmem_limit_bytes=VMEM_LIMIT_BYTES)


def _tile(n, pref):
    t = min(n, pref)
    assert n % t == 0, (n, pref)
    return t


def _rmsnorm_kernel(x_ref, g_ref, o_ref):
    x = x_ref[...]
    ms = jnp.mean(x * x, axis=-1, keepdims=True)
    o_ref[...] = (x * lax.rsqrt(ms + EPS) * g_ref[...]).astype(o_ref.dtype)


def rmsnorm(x, g, out_dtype):
    m, d = x.shape
    tm = _tile(m, 512)
    return pl.pallas_call(
        _rmsnorm_kernel,
        grid=(m // tm,),
        in_specs=[pl.BlockSpec((tm, d), lambda i: (i, 0)), pl.BlockSpec((1, d), lambda i: (0, 0))],
        out_specs=pl.BlockSpec((tm, d), lambda i: (i, 0)),
        out_shape=jax.ShapeDtypeStruct((m, d), out_dtype),
        compiler_params=_params("parallel"),
        name="rmsnorm",
    )(x, g.reshape(1, d))


def _matmul_kernel(x_ref, w_ref, o_ref, wb_ref):
    @pl.when(pl.program_id(1) == 0)
    def _():
        wb_ref[...] = w_ref[...].astype(BF16)

    o_ref[...] = jnp.dot(x_ref[...], wb_ref[...], preferred_element_type=F32).astype(o_ref.dtype)


def matmul(x, w, layer, col0, ncols, out_dtype):
    m, k = x.shape
    tm = _tile(m, 1024)
    tn = _tile(ncols, 1024)
    assert col0 % tn == 0
    jb = col0 // tn
    return pl.pallas_call(
        _matmul_kernel,
        grid=(ncols // tn, m // tm),
        in_specs=[pl.BlockSpec((tm, k), lambda j, i: (i, 0)),
                  pl.BlockSpec((None, k, tn), lambda j, i: (layer, 0, jb + j))],
        out_specs=pl.BlockSpec((tm, tn), lambda j, i: (i, j)),
        out_shape=jax.ShapeDtypeStruct((m, ncols), out_dtype),
        scratch_shapes=[pltpu.VMEM((k, tn), BF16)],
        compiler_params=_params("parallel", "arbitrary"),
        name="matmul",
    )(x, w)


def _outproj_norm_kernel(xa_ref, wa_ref, xb_ref, wb_ref, res_ref, g_ref, o_ref, h_ref):
    acc = res_ref[...] + jnp.dot(xa_ref[...], wa_ref[...], preferred_element_type=F32)
    acc = acc + jnp.dot(xb_ref[...], wb_ref[...], preferred_element_type=F32)
    o_ref[...] = acc
    ms = jnp.mean(acc * acc, axis=-1, keepdims=True)
    h_ref[...] = (acc * lax.rsqrt(ms + EPS) * g_ref[...]).astype(h_ref.dtype)


def outproj_norm(xa, xb, w, layer, res, gain):
    m, n = res.shape
    k = xa.shape[1]
    tm = _tile(m, 512)
    return pl.pallas_call(
        _outproj_norm_kernel,
        grid=(m // tm,),
        in_specs=[pl.BlockSpec((tm, k), lambda i: (i, 0)),
                  pl.BlockSpec((None, k, n), lambda i: (layer, 0, 0)),
                  pl.BlockSpec((tm, k), lambda i: (i, 0)),
                  pl.BlockSpec((None, k, n), lambda i: (layer, 1, 0)),
                  pl.BlockSpec((tm, n), lambda i: (i, 0)),
                  pl.BlockSpec((1, n), lambda i: (0, 0))],
        out_specs=[pl.BlockSpec((tm, n), lambda i: (i, 0)), pl.BlockSpec((tm, n), lambda i: (i, 0))],
        out_shape=[jax.ShapeDtypeStruct((m, n), F32), jax.ShapeDtypeStruct((m, n), BF16)],
        compiler_params=_params("parallel"),
        name="outproj_norm",
    )(xa, w, xb, w, res, gain.reshape(1, n))


def _downproj_norm_kernel(x_ref, w_ref, res_ref, g_ref, o_ref, h_ref):
    acc = res_ref[...] + jnp.dot(x_ref[...], w_ref[...], preferred_element_type=F32)
    o_ref[...] = acc
    ms = jnp.mean(acc * acc, axis=-1, keepdims=True)
    h_ref[...] = (acc * lax.rsqrt(ms + EPS) * g_ref[...]).astype(h_ref.dtype)


def downproj_norm(x, w, layer, res, gain, norm_dtype):
    m, n = res.shape
    k = x.shape[1]
    tm = _tile(m, 256)
    return pl.pallas_call(
        _downproj_norm_kernel,
        grid=(m // tm,),
        in_specs=[pl.BlockSpec((tm, k), lambda i: (i, 0)),
                  pl.BlockSpec((None, k, n), lambda i: (layer, 0, 0), pipeline_mode=pl.Buffered(1)),
                  pl.BlockSpec((tm, n), lambda i: (i, 0)),
                  pl.BlockSpec((1, n), lambda i: (0, 0))],
        out_specs=[pl.BlockSpec((tm, n), lambda i: (i, 0)), pl.BlockSpec((tm, n), lambda i: (i, 0))],
        out_shape=[jax.ShapeDtypeStruct((m, n), F32), jax.ShapeDtypeStruct((m, n), norm_dtype)],
        compiler_params=_params("parallel"),
        name="downproj_norm",
    )(x, w, res, gain.reshape(1, n))


def _matmul_conv_kernel(*refs, glu, tm, seq_tiles):
    ncomp = 2 if glu else 1
    x_ref = refs[0]
    w_refs = refs[1:1 + ncomp]
    cw_refs = refs[1 + ncomp:1 + 2 * ncomp]
    b_refs = refs[1 + 2 * ncomp:1 + 3 * ncomp]
    o_ref = refs[1 + 3 * ncomp]
    wb_ref, acc_ref, carry_ref = refs[2 + 3 * ncomp:5 + 3 * ncomp]
    i = pl.program_id(1)
    slot = i % 2
    ps = 1 - slot

    tn = o_ref.shape[1]

    @pl.when(i == 0)
    def _():
        for c in range(ncomp):
            wb_ref[:, c * tn:(c + 1) * tn] = w_refs[c][...].astype(BF16)
        carry_ref[...] = jnp.zeros_like(carry_ref)
        acc_ref[1] = jnp.zeros(acc_ref.shape[1:], F32)

    acc_ref[slot] = jnp.dot(x_ref[...], wb_ref[...], preferred_element_type=F32)

    has_prev = (i + seq_tiles - 1) % seq_tiles != 0
    has_next = i % seq_tiles != 0
    outs = []
    for c in range(ncomp):
        a = acc_ref[ps, :, c * tn:(c + 1) * tn]
        rows = lax.broadcasted_iota(jnp.int32, a.shape, 0)
        prev_row = jnp.where(has_prev, carry_ref[c], 0.0)
        next_row = jnp.where(has_next, acc_ref[slot, 0:1, c * tn:(c + 1) * tn], 0.0)
        up = jnp.where(rows == 0, prev_row, pltpu.roll(a, 1, 0))
        dn = jnp.where(rows == tm - 1, next_row, pltpu.roll(a, tm - 1, 0))
        cw = cw_refs[c]
        outs.append(up * cw[0:1, :] + a * cw[1:2, :] + dn * cw[2:3, :] + b_refs[c][...])
        carry_ref[c] = a[tm - 1:tm, :]
    if glu:
        g, val = outs
        res = g * (1.0 / (1.0 + jnp.exp(-g))) * val
    else:
        res = outs[0]
    o_ref[...] = res.astype(o_ref.dtype)


def matmul_conv(x, w, cw, cb, layer, seq_len, col0, ncols, glu, out_dtype):
    m, k = x.shape
    tm = _tile(seq_len, 1024)
    tn = _tile(ncols, 512 if glu else 1024)
    nm = m // tm
    seq_tiles = seq_len // tm
    assert col0 % tn == 0
    ncomp = 2 if glu else 1
    coffs = [c * (ncols // tn) for c in range(ncomp)]
    woffs = [col0 // tn + o for o in coffs]
    cb3 = cb.reshape(cb.shape[0], 1, cb.shape[1])
    in_specs = [pl.BlockSpec((tm, k), lambda j, i: (jnp.minimum(i, nm - 1), 0))]
    in_specs += [pl.BlockSpec((None, k, tn), lambda j, i, o=o: (layer, 0, o + j)) for o in woffs]
    in_specs += [pl.BlockSpec((None, 3, tn), lambda j, i, o=o: (layer, 0, o + j)) for o in coffs]
    in_specs += [pl.BlockSpec((None, 1, tn), lambda j, i, o=o: (layer, 0, o + j)) for o in coffs]
    args = [x] + [w] * ncomp + [cw] * ncomp + [cb3] * ncomp
    return pl.pallas_call(
        functools.partial(_matmul_conv_kernel, glu=glu, tm=tm, seq_tiles=seq_tiles),
        grid=(ncols // tn, nm + 1),
        in_specs=in_specs,
        out_specs=pl.BlockSpec((tm, tn), lambda j, i: (jnp.maximum(i - 1, 0), j)),
        out_shape=jax.ShapeDtypeStruct((m, ncols), out_dtype),
        scratch_shapes=[pltpu.VMEM((k, ncomp * tn), BF16), pltpu.VMEM((2, tm, ncomp * tn), F32),
                        pltpu.VMEM((ncomp, 1, tn), F32)],
        compiler_params=_params("parallel", "arbitrary"),
        name="matmul_conv_glu" if glu else "matmul_conv",
    )(*args)


def _gmlp_kernel(p_ref, gain_ref, ws_ref, bs_ref, o_ref, *, tm, width):
    p = p_ref[...]
    g = 0.5 * p * (1.0 + lax.erf(p * (1.0 / math.sqrt(2.0))))
    u = g[:, :width]
    v = g[:, width:]
    ms = jnp.mean(v * v, axis=-1, keepdims=True)
    vb = (v * lax.rsqrt(ms + EPS) * gain_ref[...]).astype(BF16)
    gd = width // A_GROUPS
    for c in range(tm // CHUNK):
        r0 = c * CHUNK
        for gi in range(A_GROUPS):
            c0 = gi * gd
            s = jnp.dot(ws_ref[gi], vb[r0:r0 + CHUNK, c0:c0 + gd], preferred_element_type=F32)
            s = s + bs_ref[:, c0:c0 + gd]
            o_ref[r0:r0 + CHUNK, c0:c0 + gd] = (u[r0:r0 + CHUNK, c0:c0 + gd] * s).astype(o_ref.dtype)


def spatial_gating(p, v_gain, w_s, b_s):
    m, two_w = p.shape
    width = two_w // 2
    gd = width // A_GROUPS
    tm = _tile(m, 512)
    bs_full = jnp.repeat(b_s.T, gd, axis=1)
    return pl.pallas_call(
        functools.partial(_gmlp_kernel, tm=tm, width=width),
        grid=(m // tm,),
        in_specs=[pl.BlockSpec((tm, two_w), lambda i: (i, 0)),
                  pl.BlockSpec((1, width), lambda i: (0, 0)),
                  pl.BlockSpec((A_GROUPS, CHUNK, CHUNK), lambda i: (0, 0, 0)),
                  pl.BlockSpec((CHUNK, width), lambda i: (0, 0))],
        out_specs=pl.BlockSpec((tm, width), lambda i: (i, 0)),
        out_shape=jax.ShapeDtypeStruct((m, width), BF16),
        compiler_params=_params("parallel"),
        name="spatial_gating",
    )(p, v_gain.reshape(1, width), w_s.astype(BF16), bs_full)


def _dft_matrices(L):
    n = 2 * L
    r1 = 64
    r0 = L // r1
    t = np.arange(L, dtype=np.int64)[None, :]
    a1 = ((r0 * np.arange(r1, dtype=np.int64))[:, None] * t) % n
    a0 = (np.arange(r0, dtype=np.int64)[:, None] * t) % n
    ca, sa, cb, sb = lax.optimization_barrier((
        jnp.asarray(np.cos(2 * np.pi * a1 / n), F32)[:, None, :],
        jnp.asarray(np.sin(2 * np.pi * a1 / n), F32)[:, None, :],
        jnp.asarray(np.cos(2 * np.pi * a0 / n), F32)[None, :, :],
        jnp.asarray(np.sin(2 * np.pi * a0 / n), F32)[None, :, :]))
    cosm = (ca * cb - sa * sb).astype(BF16).reshape(L, L)
    sinm = (-(sa * cb + ca * sb)).astype(BF16).reshape(L, L)
    return cosm, sinm


def _filter_features(L):
    t = np.linspace(0.0, 1.0, L)[:, None]
    w = 2.0 * np.pi * np.arange(L)[:, None] / L
    bands = np.linspace(1e-4, HYENA_BANDS - 1, HYENA_BANDS)[None, :]
    z = np.concatenate([t, np.cos(w * bands), -np.sin(w * bands)], axis=-1)
    return jnp.asarray(np.pad(z, ((0, 0), (0, 128 - HYENA_EMB))), F32)


def _alternating(shape):
    rows = lax.broadcasted_iota(jnp.int32, shape, 0)
    return jnp.where(rows % 2 == 0, 1.0, -1.0)


def _twiddles(L):
    g = np.arange(L // 4)[:, None] * np.ones((1, LANES))
    a1, a2 = 2.0 * np.pi * g / L, 2.0 * np.pi * g / (2 * L)
    return jnp.asarray(np.stack([np.cos(a1), np.sin(a1), np.cos(a2), np.sin(a2)]), F32)


def _cadd(a, b):
    return a[0] + b[0], a[1] + b[1]


def _csub(a, b):
    return a[0] - b[0], a[1] - b[1]


def _conj(a):
    return a[0], -a[1]


def _cmul(a, b):
    return a[0] * b[0] - a[1] * b[1], a[0] * b[1] + a[1] * b[0]


def _rot(a, c, s):
    return a[0] * c + a[1] * s, a[1] * c - a[0] * s


def _unrot(a, c, s):
    return a[0] * c - a[1] * s, a[1] * c + a[0] * s


def _times_i(a):
    return -a[1], a[0]


def _times_minus_i(a):
    return a[1], -a[0]


def _lane_block(x, j):
    return x[:, j * LANES:(j + 1) * LANES]


def _periodic(phase, values):
    out = jnp.full(phase.shape, values[-1], F32)
    for k, v in enumerate(values[:-1]):
        out = jnp.where(phase == k, v, out)
    return out


def _filter_spectrum_kernel(z_ref, w1_ref, b1_ref, w2_ref, b2_ref, fr_ref, w3f_ref, w3b_ref,
                            dl_ref, cos_ref, sin_ref, o_ref, rc_ref, rs_ref, nyq_ref, h_ref, taps_ref, *, n):
    m = pl.program_id(2)
    hp = lax.Precision.HIGHEST

    @pl.when(jnp.logical_and(jnp.logical_and(pl.program_id(0) == 0, pl.program_id(1) == 0), m == 0))
    def _():
        h = jnp.dot(z_ref[...], w1_ref[...], precision=hp, preferred_element_type=F32) + b1_ref[...]
        h = jnp.sin(fr_ref[0:1, :] * h)
        h = jnp.dot(h, w2_ref[...], precision=hp, preferred_element_type=F32) + b2_ref[...]
        h_ref[...] = jnp.sin(fr_ref[1:2, :] * h)

    @pl.when(m == 0)
    def _():
        length = z_ref.shape[0]
        chunk = min(length, 2 * DFT_TILE)
        nrm = None
        for r0 in range(0, length, chunk):
            rs = slice(r0, r0 + chunk)
            h = h_ref[rs, :]
            decay = jnp.exp(-z_ref[rs, 0:1] * dl_ref[...])
            hf = jnp.dot(h, w3f_ref[...], precision=hp, preferred_element_type=F32) * decay
            hb = jnp.dot(h, w3b_ref[...], precision=hp, preferred_element_type=F32) * decay
            if r0 == 0:
                hb = jnp.where(lax.broadcasted_iota(jnp.int32, hb.shape, 0) == 0, 0.0, hb)
            part = jnp.sum(jnp.abs(hf) + jnp.abs(hb), axis=0, keepdims=True)
            nrm = part if nrm is None else nrm + part
            taps_ref[0, rs, :] = hf + hb
            taps_ref[1, rs, :] = hf - hb
        inv = 1.0 / nrm
        tc = taps_ref.shape[2]
        r = math.sqrt(0.5)
        packed = [None] * 4
        for r0 in range(0, length, chunk):
            rs = slice(r0, r0 + chunk)
            ke = taps_ref[0, rs, :] * inv
            ko = taps_ref[1, rs, :] * inv
            t8 = lax.broadcasted_iota(jnp.int32, ke.shape, 0) % 8
            alt = _periodic(t8, (1.0, -1.0) * 4)
            hc = _periodic(t8, (1.0, 0.0, -1.0, 0.0) * 2)
            hs = _periodic(t8, (0.0, 1.0, 0.0, -1.0) * 2)
            qc = _periodic(t8, (1.0, r, 0.0, -r, -1.0, -r, 0.0, r))
            qs = _periodic(t8, (0.0, r, 1.0, r, 0.0, -r, -1.0, -r))
            q3c = _periodic(t8, (1.0, -r, 0.0, r, -1.0, r, 0.0, -r))
            q3s = _periodic(t8, (0.0, r, -1.0, r, 0.0, -r, 1.0, -r))
            for blk, (vc, vs) in enumerate(((ke, ko), (ke * alt, -ko * alt), (ke * hc, ke * hs),
                                            (ko * hs, ko * hc))):
                rc_ref[rs, blk * tc:(blk + 1) * tc] = vc.astype(BF16)
                rs_ref[rs, blk * tc:(blk + 1) * tc] = vs.astype(BF16)
            parts = (jnp.sum(ke * qc, axis=0, keepdims=True), -jnp.sum(ko * qs, axis=0, keepdims=True),
                     jnp.sum(ke * q3c, axis=0, keepdims=True), -jnp.sum(ko * q3s, axis=0, keepdims=True))
            packed = [pt if acc is None else acc + pt for acc, pt in zip(packed, parts)]
        for i in range(4):
            nyq_ref[i:i + 1, :] = packed[i] * (2.0 / n)

    tc = o_ref.shape[2]
    rc = jnp.dot(cos_ref[...], rc_ref[...], preferred_element_type=F32)
    rs = jnp.dot(sin_ref[...], rs_ref[...], preferred_element_type=F32)
    c = [rc[:, i * tc:(i + 1) * tc] for i in range(4)]
    s = [rs[:, i * tc:(i + 1) * tc] for i in range(4)]
    planes = [c[0], s[0], c[1], s[1], c[2] - s[2], -c[3] - s[3], c[2] + s[2], s[3] - c[3]]
    first = jnp.logical_and(lax.broadcasted_iota(jnp.int32, c[0].shape, 0) == 0, m == 0)
    scale = jnp.where(first, 1.0 / n, 2.0 / n)
    slots = {1: 2, 3: 3, 6: 0, 7: 1}
    for i in range(8):
        val = planes[i] * scale
        if i in slots:
            val = jnp.where(first, nyq_ref[slots[i]:slots[i] + 1, :], val)
        o_ref[i] = val


def hyena_filter_spectrum(L, width, cosm, sinm, w1, b1, w2, b2, w3, freq):
    z = _filter_features(L)
    tc = LANES
    ncb = width // tc
    max_decay = math.log(HYENA_TARGET) / HYENA_FAST
    min_decay = math.log(HYENA_TARGET) / HYENA_SLOW
    deltas = jnp.asarray(np.abs(np.linspace(min_decay, max_decay, width)), F32).reshape(1, width)
    w1p = jnp.pad(w1, ((0, 128 - HYENA_EMB), (0, 0)))
    const = lambda o, c, m: (0, 0)
    tm = _tile(L // 4, DFT_TILE)
    return pl.pallas_call(
        functools.partial(_filter_spectrum_kernel, n=2 * L),
        grid=(HYENA_ORDER, ncb, L // 4 // tm),
        in_specs=[pl.BlockSpec((L, 128), const),
                  pl.BlockSpec((128, HYENA_FFN), const), pl.BlockSpec((1, HYENA_FFN), const),
                  pl.BlockSpec((HYENA_FFN, HYENA_FFN), const), pl.BlockSpec((1, HYENA_FFN), const),
                  pl.BlockSpec((2, HYENA_FFN), const),
                  pl.BlockSpec((HYENA_FFN, tc), lambda o, c, m: (0, (2 * o) * ncb + c)),
                  pl.BlockSpec((HYENA_FFN, tc), lambda o, c, m: (0, (2 * o + 1) * ncb + c)),
                  pl.BlockSpec((1, tc), lambda o, c, m: (0, c)),
                  pl.BlockSpec((tm, L), lambda o, c, m: (m, 0)),
                  pl.BlockSpec((tm, L), lambda o, c, m: (m, 0))],
        out_specs=pl.BlockSpec((None, 8, tm, tc), lambda o, c, m: (o, 0, m, c)),
        out_shape=jax.ShapeDtypeStruct((HYENA_ORDER, 8, L // 4, width), F32),
        scratch_shapes=[pltpu.VMEM((L, 4 * tc), BF16), pltpu.VMEM((L, 4 * tc), BF16),
                        pltpu.VMEM((4, tc), F32), pltpu.VMEM((L, HYENA_FFN), F32),
                        pltpu.VMEM((2, L, tc), F32)],
        compiler_params=_params("arbitrary", "arbitrary", "arbitrary"),
        name="hyena_filter_spectrum",
    )(z, w1p, b1.reshape(1, -1), w2, b2.reshape(1, -1), freq, w3, w3, deltas, cosm, sinm)


def _dft_forward_kernel(u_ref, cos_ref, sin_ref, tw_ref, kf_ref, o_ref, ub_ref):
    q = ub_ref.shape[0]
    nyq = []
    for j in range(4):
        xj = u_ref[pl.ds(j, q, stride=4), :]
        ub_ref[:, j * LANES:(j + 1) * LANES] = xj.astype(BF16)
        nyq.append(jnp.sum(xj * _alternating(xj.shape), axis=0, keepdims=True))
    eq, oq = (nyq[0], -nyq[2]), (nyq[1], -nyq[3])
    woq = _rot(oq, math.sqrt(0.5), math.sqrt(0.5))
    u_q, u_3q = _cadd(eq, woq), _conj(_csub(eq, woq))
    chunk = min(q, DFT_TILE)
    for r0 in range(0, q, chunk):
        rs = slice(r0, r0 + chunk)
        re = jnp.dot(cos_ref[rs, :], ub_ref[...], preferred_element_type=F32)
        im = jnp.dot(sin_ref[rs, :], ub_ref[...], preferred_element_type=F32)
        x = [(_lane_block(re, j), _lane_block(im, j)) for j in range(4)]
        c1, s1, c2, s2 = tw_ref[0, rs, :], tw_ref[1, rs, :], tw_ref[2, rs, :], tw_ref[3, rs, :]
        t2, t3 = _rot(x[2], c1, s1), _rot(x[3], c1, s1)
        ea, eb = _cadd(x[0], t2), _csub(x[0], t2)
        oa, ob = _cadd(x[1], t3), _csub(x[1], t3)
        ta, tb = _rot(oa, c2, s2), _times_i(_rot(ob, c2, s2))
        p = [_cadd(ea, ta), _conj(_csub(ea, ta)), _conj(_cadd(eb, tb)), _csub(eb, tb)]
        k = [(kf_ref[2 * i, rs, :], kf_ref[2 * i + 1, rs, :]) for i in range(4)]
        if r0 == 0:
            first = lax.broadcasted_iota(jnp.int32, re.shape[:1] + (LANES,), 0) == 0
            p[3] = (jnp.where(first, u_q[0], p[3][0]), jnp.where(first, u_q[1], p[3][1]))
        y = [_cmul(p[i], k[i]) for i in range(4)]
        if r0 == 0:
            y3q = _cmul(u_3q, (k[0][1], k[1][1]))
            y[0] = (y[0][0], jnp.where(first, y3q[0], y[0][1]))
            y[1] = (y[1][0], jnp.where(first, y3q[1], y[1][1]))
        for i in range(4):
            o_ref[2 * i, rs, :] = y[i][0].astype(o_ref.dtype)
            o_ref[2 * i + 1, rs, :] = y[i][1].astype(o_ref.dtype)


def _resident(shape):
    return pl.BlockSpec(shape, lambda c, b: (0,) * len(shape), pipeline_mode=pl.Buffered(1))


def dft_forward_times_filter(u, col0, width, cos4, sin4, tw, kf, order):
    bsz, L, _ = u.shape
    q = L // 4
    assert col0 % LANES == 0
    cb0 = col0 // LANES
    return pl.pallas_call(
        _dft_forward_kernel,
        grid=(width // LANES, bsz),
        in_specs=[pl.BlockSpec((None, L, LANES), lambda c, b: (b, 0, cb0 + c)),
                  _resident((q, q)), _resident((q, q)), _resident((4, q, LANES)),
                  pl.BlockSpec((None, 8, q, LANES), lambda c, b: (order, 0, 0, c))],
        out_specs=pl.BlockSpec((None, 8, q, LANES), lambda c, b: (b, 0, 0, c)),
        out_shape=jax.ShapeDtypeStruct((bsz, 8, q, width), BF16),
        scratch_shapes=[pltpu.VMEM((q, 4 * LANES), BF16)],
        compiler_params=_params("parallel", "parallel"),
        name="dft_forward",
    )(u, cos4, sin4, tw, kf)


def _dft_inverse_kernel(y_ref, cos_ref, sin_ref, tw_ref, u_ref, g_ref, skip_ref, o_ref, x_ref, res_ref):
    q = x_ref.shape[1]

    def row0(i):
        return y_ref[i, 0:1, :].astype(F32)

    u_q, u_3q = (row0(6), row0(7)), (row0(1), row0(3))
    eq = _cadd(u_q, _conj(u_3q))
    oq = _unrot(_csub(u_q, _conj(u_3q)), math.sqrt(0.5), math.sqrt(0.5))
    nyq = jnp.concatenate([eq[0], oq[0], -eq[1], -oq[1]], axis=1)
    chunk = min(q, DFT_TILE)
    for r0 in range(0, q, chunk):
        rs = slice(r0, r0 + chunk)
        p = [(y_ref[2 * i, rs, :].astype(F32), y_ref[2 * i + 1, rs, :].astype(F32)) for i in range(4)]
        if r0 == 0:
            first = lax.broadcasted_iota(jnp.int32, p[0][0].shape, 0) == 0
            p[0] = (p[0][0], jnp.where(first, 0.0, p[0][1]))
            p[1] = (p[1][0], jnp.where(first, 0.0, p[1][1]))
            p[3] = (jnp.where(first, p[2][0], p[3][0]), jnp.where(first, p[2][1], p[3][1]))
        c1, s1, c2, s2 = tw_ref[0, rs, :], tw_ref[1, rs, :], tw_ref[2, rs, :], tw_ref[3, rs, :]
        ea, ta = _cadd(p[0], _conj(p[1])), _csub(p[0], _conj(p[1]))
        eb, tb = _cadd(_conj(p[2]), p[3]), _csub(_conj(p[2]), p[3])
        oa = _unrot(ta, c2, s2)
        ob = _times_minus_i(_unrot(tb, c2, s2))
        x = [_cadd(ea, eb), _cadd(oa, ob), _unrot(_csub(ea, eb), c1, s1), _unrot(_csub(oa, ob), c1, s1)]
        for j in range(4):
            x_ref[0, rs, j * LANES:(j + 1) * LANES] = x[j][0].astype(BF16)
            x_ref[1, rs, j * LANES:(j + 1) * LANES] = x[j][1].astype(BF16)
    skip = skip_ref[...]
    alt = _alternating((chunk, 4 * LANES))
    for r0 in range(0, q, chunk):
        rs = slice(r0, r0 + chunk)
        conv = jnp.dot(cos_ref[rs, :], x_ref[0], preferred_element_type=F32)
        conv = conv + jnp.dot(sin_ref[rs, :], x_ref[1], preferred_element_type=F32)
        conv = conv + alt * nyq
        for j in range(4):
            rows = pl.ds(4 * r0 + j, chunk, stride=4)
            res_ref[rows, :] = g_ref[rows, :] * (_lane_block(conv, j) + u_ref[rows, :] * skip)
    o_ref[...] = res_ref[...].astype(o_ref.dtype)


def dft_inverse_gated(y, cos4, sin4, tw, u, ucol0, gate, gcol0, skip, out_dtype):
    bsz, _, q, width = y.shape
    L = 4 * q
    ub0, gb0 = ucol0 // LANES, gcol0 // LANES
    return pl.pallas_call(
        _dft_inverse_kernel,
        grid=(width // LANES, bsz),
        in_specs=[pl.BlockSpec((None, 8, q, LANES), lambda c, b: (b, 0, 0, c)),
                  _resident((q, q)), _resident((q, q)), _resident((4, q, LANES)),
                  pl.BlockSpec((None, L, LANES), lambda c, b: (b, 0, ub0 + c)),
                  pl.BlockSpec((None, L, LANES), lambda c, b: (b, 0, gb0 + c)),
                  pl.BlockSpec((1, LANES), lambda c, b: (0, c))],
        out_specs=pl.BlockSpec((None, L, LANES), lambda c, b: (b, 0, c)),
        out_shape=jax.ShapeDtypeStruct((bsz, L, width), out_dtype),
        scratch_shapes=[pltpu.VMEM((2, q, 4 * LANES), BF16), pltpu.VMEM((L, LANES), F32)],
        compiler_params=_params("parallel", "parallel"),
        name="dft_inverse",
    )(y, cos4, sin4, tw, u, gate, skip.reshape(1, width))


NA_GROUP = 8
NA_BLOCK = NA_GROUP + NA_KH
NA_PAD = NA_KH // 2


def _na_plans():
    tiles = {}
    plans = []
    for variant in range(3):
        plan = []
        for rq in range(NA_GROUP):
            lo = (max(rq, NA_PAD), rq, min(rq, NA_PAD))[variant]
            a_lo, a_hi = lo // 2, (lo + NA_KH - 1) // 2 + 1
            ids = []
            for a in range(a_lo, a_hi):
                key = tuple(kr - rq + NA_KH - 1 - NA_PAD if lo <= kr < lo + NA_KH else -1
                            for kr in (2 * a, 2 * a + 1))
                ids.append(tiles.setdefault(key, len(tiles)))
            plan.append((a_lo, tuple(ids)))
        plans.append(tuple(plan))
    return tuple(plans), list(tiles)


def _na_kernel(q_ref, k_ref, v_ref, tiles_ref, o_ref, kb_ref, vb_ref, p_ref, *, rows, scale, plans):
    L = rows * GRID_W
    pad = NA_PAD * GRID_W
    gq = NA_GROUP * GRID_W
    gk = NA_BLOCK * GRID_W
    for ref, src in ((kb_ref, k_ref), (vb_ref, v_ref)):
        ref[0:pad, :] = jnp.zeros((pad, HEAD_DIM), BF16)
        ref[pad + L:, :] = jnp.zeros((pad, HEAD_DIM), BF16)
        ref[pad:pad + L, :] = src[...].astype(BF16)

    def group(q0, plan):
        q = q_ref[pl.ds(q0, gq), :].astype(BF16)
        s = lax.dot_general(q, kb_ref[pl.ds(q0, gk), :], (((1,), (1,)), ((), ())),
                            preferred_element_type=F32)
        p_ref[...] = jnp.zeros(p_ref.shape, BF16)
        for rq, (a_lo, ids) in enumerate(plan):
            r0, c0, c1 = rq * GRID_W, 2 * GRID_W * a_lo, 2 * GRID_W * (a_lo + len(ids))
            bias = jnp.concatenate([tiles_ref[t] for t in ids], axis=1)
            sl = s[r0:r0 + GRID_W, c0:c1] * scale + bias
            e = jnp.exp(sl - jnp.max(sl, axis=-1, keepdims=True))
            pr = e * (1.0 / jnp.sum(e, axis=-1, keepdims=True))
            p_ref[r0:r0 + GRID_W, c0:c1] = pr.astype(BF16)
        o = jnp.dot(p_ref[...], vb_ref[pl.ds(q0, gk), :], preferred_element_type=F32)
        o_ref[pl.ds(q0, gq), :] = o.astype(o_ref.dtype)

    ngroups = rows // NA_GROUP
    group(0, plans[0])

    def middle(g, carry):
        group(pl.multiple_of(g * gq, gq), plans[1])
        return carry

    lax.fori_loop(1, ngroups - 1, middle, 0)
    group((ngroups - 1) * gq, plans[2])


def _na_bias_tiles(rpb, tile_keys):
    col = np.arange(GRID_W)
    cs = np.clip(col - NA_KW // 2, 0, GRID_W - NA_KW)
    kc = np.arange(GRID_W)
    inwin = (kc[None, :] >= cs[:, None]) & (kc[None, :] < cs[:, None] + NA_KW)
    rel_col = np.clip(kc[None, :] - col[:, None] + (NA_KW - 1), 0, 2 * NA_KW - 2)
    t = jnp.where(jnp.asarray(inwin)[None, None], rpb[:, :, rel_col], MASK_VALUE)
    masked = jnp.full((rpb.shape[0], 1, GRID_W, GRID_W), MASK_VALUE, F32)
    t = jnp.concatenate([t.astype(F32), masked], axis=1)
    left = np.array([k[0] for k in tile_keys])
    right = np.array([k[1] for k in tile_keys])
    return jnp.concatenate([t[:, left], t[:, right]], axis=-1)


def neighborhood_attention(p, bsz, L, heads, rpb):
    rows = L // GRID_W
    assert rows % NA_GROUP == 0 and rows >= 2 * NA_GROUP
    plans, tile_keys = _na_plans()
    tiles = _na_bias_tiles(rpb, tile_keys)
    nt = len(tile_keys)
    lp = L + 2 * NA_PAD * GRID_W
    return pl.pallas_call(
        functools.partial(_na_kernel, rows=rows, scale=HEAD_DIM ** -0.5, plans=plans),
        grid=(bsz, heads),
        in_specs=[pl.BlockSpec((None, L, HEAD_DIM), lambda b, h: (b, 0, h)),
                  pl.BlockSpec((None, L, HEAD_DIM), lambda b, h: (b, 0, heads + h)),
                  pl.BlockSpec((None, L, HEAD_DIM), lambda b, h: (b, 0, 2 * heads + h)),
                  pl.BlockSpec((None, nt, GRID_W, 2 * GRID_W), lambda b, h: (h, 0, 0, 0))],
        out_specs=pl.BlockSpec((None, L, HEAD_DIM), lambda b, h: (b, 0, h)),
        out_shape=jax.ShapeDtypeStruct((bsz, L, heads * HEAD_DIM), BF16),
        scratch_shapes=[pltpu.VMEM((lp, HEAD_DIM), BF16), pltpu.VMEM((lp, HEAD_DIM), BF16),
                        pltpu.VMEM((NA_GROUP * GRID_W, NA_BLOCK * GRID_W), BF16)],
        compiler_params=_params("parallel", "parallel"),
        name="neighborhood_attention",
    )(p, p, p, tiles)


DIFF_SUB_ROWS = 256


def _rope(x, cos, sin_signed, half):
    lane = lax.broadcasted_iota(jnp.int32, x.shape, 1)
    width = x.shape[1]
    first = (lane % (2 * half)) < half
    partner = jnp.where(first, pltpu.roll(x, width - half, 1), pltpu.roll(x, half, 1))
    return x * cos + partner * sin_signed


def _diff_attn_kernel(q_ref, k_ref, v_ref, cq_ref, sq_ref, ck_ref, sk_ref, lam_ref, sub_ref, o_ref,
                      kb_ref, vb_ref, *, lam_init, scale, half, n_sub):
    qi = pl.program_id(2)

    @pl.when(qi == 0)
    def _():
        kb_ref[...] = _rope(k_ref[...], ck_ref[...], sk_ref[...], half).astype(BF16)
        vb_ref[...] = v_ref[...].astype(BF16)

    lp = lam_ref[...]
    lam = (jnp.exp(jnp.sum(lp[0:1] * lp[1:2], axis=-1, keepdims=True))
           - jnp.exp(jnp.sum(lp[2:3] * lp[3:4], axis=-1, keepdims=True)) + lam_init)
    kb = kb_ref[...]
    tq = q_ref.shape[0]
    sub = tq // n_sub
    for sb in range(n_sub):
        rs = slice(sb * sub, (sb + 1) * sub)
        q = _rope(q_ref[rs, :], cq_ref[rs, :], sq_ref[rs, :], half) * (scale * math.log2(math.e))
        lane = lax.broadcasted_iota(jnp.int32, q.shape, 1)

        def softmax_map(in_map, weight):
            qm = jnp.where(in_map, q, 0.0).astype(BF16)
            s = lax.dot_general(qm, kb, (((1,), (1,)), ((), ())), preferred_element_type=F32)
            e = jnp.exp2(s - jnp.max(s, axis=-1, keepdims=True))
            return e * (weight / jnp.sum(e, axis=-1, keepdims=True))

        a = softmax_map(lane < 2 * half, 1.0) - softmax_map(lane >= 2 * half, lam)
        o = jnp.dot(a.astype(BF16), vb_ref[...], preferred_element_type=F32)
        ms = jnp.mean(o * o, axis=-1, keepdims=True)
        o_ref[rs, :] = (o * lax.rsqrt(ms + EPS) * sub_ref[...] * (1.0 - lam_init)).astype(o_ref.dtype)


def _rope_tables(L, half):
    inv = (1.0 / (np.float32(ROPE_THETA) ** (np.arange(half, dtype=np.float32) * np.float32(2.0)
                                             / np.float32(2 * half)))).astype(np.float32)
    ang = (np.arange(L, dtype=np.float32)[:, None] * inv[None, :]).astype(np.float64)
    cos, sin = np.cos(ang), np.sin(ang)
    reps = HEAD_DIM // (2 * half)
    cos_t = np.tile(cos, (1, 2 * reps))
    sin_t = np.tile(np.concatenate([-sin, sin], axis=1), (1, reps))
    return jnp.asarray(cos_t, F32), jnp.asarray(sin_t, F32)


def diff_attention(p, col0, bsz, L, heads, lam_params, subln, lam_init):
    half = HEAD_DIM // 4
    assert col0 % HEAD_DIM == 0
    c0 = col0 // HEAD_DIM
    tq = _tile(L, 2 * DIFF_SUB_ROWS)
    n_sub = tq // DIFF_SUB_ROWS
    cos_t, sin_t = _rope_tables(L, half)
    return pl.pallas_call(
        functools.partial(_diff_attn_kernel, lam_init=lam_init, scale=(2 * half) ** -0.5, half=half,
                          n_sub=n_sub),
        grid=(bsz, heads, L // tq),
        in_specs=[pl.BlockSpec((None, tq, HEAD_DIM), lambda b, h, i: (b, i, c0 + h)),
                  pl.BlockSpec((None, L, HEAD_DIM), lambda b, h, i: (b, 0, c0 + heads + h)),
                  pl.BlockSpec((None, L, HEAD_DIM), lambda b, h, i: (b, 0, c0 + 2 * heads + h)),
                  pl.BlockSpec((tq, HEAD_DIM), lambda b, h, i: (i, 0)),
                  pl.BlockSpec((tq, HEAD_DIM), lambda b, h, i: (i, 0)),
                  pl.BlockSpec((L, HEAD_DIM), lambda b, h, i: (0, 0)),
                  pl.BlockSpec((L, HEAD_DIM), lambda b, h, i: (0, 0)),
                  pl.BlockSpec((4, 2 * half), lambda b, h, i: (0, 0)),
                  pl.BlockSpec((1, HEAD_DIM), lambda b, h, i: (0, 0))],
        out_specs=pl.BlockSpec((None, tq, HEAD_DIM), lambda b, h, i: (b, i, h)),
        out_shape=jax.ShapeDtypeStruct((bsz, L, heads * HEAD_DIM), BF16),
        scratch_shapes=[pltpu.VMEM((L, HEAD_DIM), BF16), pltpu.VMEM((L, HEAD_DIM), BF16)],
        compiler_params=_params("parallel", "parallel", "arbitrary"),
        name="diff_attention",
    )(p, p, p, cos_t, sin_t, cos_t, sin_t, lam_params, subln.reshape(1, HEAD_DIM))


def kernel(x, norm_mix, norm_ffn, w_out, ffn_up, ffn_conv_w, ffn_conv_b, ffn_down, final_norm,
           ab_w_in, a_vnorm, a_ws, a_bs, b_conv_w, b_conv_b, b_filt_w1, b_filt_b1, b_filt_w2,
           b_filt_b2, b_filt_w3, b_filt_freq, b_skip, cd_w_in, c_rpb, d_lambda, d_subln):
    bsz, L, d = x.shape
    depth = norm_mix.shape[0]
    m = bsz * L
    half_w = d // 2
    ff = ffn_down.shape[1]
    heads = half_w // HEAD_DIM
    assert heads == D_HEADS and half_w // A_GROUPS == CHUNK and L % GRID_W == 0
    xs = x.reshape(m, d)
    cosm, sinm = _dft_matrices(L)
    cos4, sin4 = _dft_matrices(L // 4)
    tw = _twiddles(L)
    w_out_b = w_out.astype(BF16)
    ffn_down_b = ffn_down.astype(BF16)
    h = rmsnorm(xs, norm_mix[0], BF16)
    for l in range(depth):
        i = l // 2
        if l % 2 == 0:
            pa = matmul(h, ab_w_in, i, 0, 2 * half_w, F32)
            ya = spatial_gating(pa, a_vnorm[i], a_ws[i], a_bs[i])
            pb = matmul_conv(h, ab_w_in, b_conv_w, b_conv_b, i, L, 2 * half_w, 3 * half_w, False, F32)
            pb = pb.reshape(bsz, L, 3 * half_w)
            kf = hyena_filter_spectrum(L, half_w, cosm, sinm, b_filt_w1[i], b_filt_b1[i], b_filt_w2[i],
                                       b_filt_b2[i], b_filt_w3[i], b_filt_freq[i])
            y1 = dft_forward_times_filter(pb, 0, half_w, cos4, sin4, tw, kf, 0)
            z = dft_inverse_gated(y1, cos4, sin4, tw, pb, 0, pb, half_w, b_skip[i, 0], F32)
            y2 = dft_forward_times_filter(z, 0, half_w, cos4, sin4, tw, kf, 1)
            yb = dft_inverse_gated(y2, cos4, sin4, tw, z, 0, pb, 2 * half_w, b_skip[i, 1], BF16)
            yb = yb.reshape(m, half_w)
        else:
            p = matmul(h, cd_w_in, i, 0, 6 * half_w, F32).reshape(bsz, L, 6 * half_w)
            lam_init = 0.8 - 0.6 * math.exp(-0.3 * l)
            ya = neighborhood_attention(p, bsz, L, heads, c_rpb[i]).reshape(m, half_w)
            yb = diff_attention(p, 3 * half_w, bsz, L, heads, d_lambda[i], d_subln[i], lam_init)
            yb = yb.reshape(m, half_w)
        xs, hf = outproj_norm(ya, yb, w_out_b, l, xs, norm_ffn[l])
        act = matmul_conv(hf, ffn_up, ffn_conv_w, ffn_conv_b, l, L, 0, ff, True, BF16)
        last = l == depth - 1
        xs, h = downproj_norm(act, ffn_down_b, l, xs, final_norm if last else norm_mix[l + 1],
                              F32 if last else BF16)
    return h.reshape(bsz, L, d)
```

```python
import functools
import math

import numpy as np
import jax
import jax.numpy as jnp
from jax import lax
from jax.experimental import pallas as pl
from jax.experimental.pallas import tpu as pltpu

F32 = jnp.float32
BF16 = jnp.bfloat16

EPS = 1e-6
GRID_W = 64
HEAD_DIM = 128
CHUNK = 128
A_GROUPS = 8
NA_KH = 8
NA_KW = 16
D_HEADS = 8
ROPE_THETA = 10000.0
HYENA_ORDER = 2
HYENA_EMB = 33
HYENA_BANDS = (HYENA_EMB - 1) // 2
HYENA_FFN = 64
HYENA_TARGET = 1e-2
HYENA_FAST = 0.3
HYENA_SLOW = 1.5
MASK_VALUE = -1e30

VMEM_LIMIT_BYTES = 56 * 1024 * 1024
DFT_TILE = 512
LANES = 128


def _params(*sem):
    return pltpu.CompilerParams(dimension_semantics=sem, vmem_limit_bytes=VMEM_LIMIT_BYTES)


def _tile(n, pref):
    t = min(n, pref)
    assert n % t == 0, (n, pref)
    return t


def _rmsnorm_kernel(x_ref, g_ref, o_ref):
    x = x_ref[...]
    ms = jnp.mean(x * x, axis=-1, keepdims=True)
    o_ref[...] = (x * lax.rsqrt(ms + EPS) * g_ref[...]).astype(o_ref.dtype)


def rmsnorm(x, g, out_dtype):
    m, d = x.shape
    tm = _tile(m, 512)
    return pl.pallas_call(
        _rmsnorm_kernel,
        grid=(m // tm,),
        in_specs=[pl.BlockSpec((tm, d), lambda i: (i, 0)), pl.BlockSpec((1, d), lambda i: (0, 0))],
        out_specs=pl.BlockSpec((tm, d), lambda i: (i, 0)),
        out_shape=jax.ShapeDtypeStruct((m, d), out_dtype),
        compiler_params=_params("parallel"),
        name="rmsnorm",
    )(x, g.reshape(1, d))


def _matmul_kernel(x_ref, w_ref, o_ref, wb_ref):
    @pl.when(pl.program_id(1) == 0)
    def _():
        wb_ref[...] = w_ref[...].astype(BF16)

    o_ref[...] = jnp.dot(x_ref[...], wb_ref[...], preferred_element_type=F32).astype(o_ref.dtype)


def matmul(x, w, layer, col0, ncols, out_dtype):
    m, k = x.shape
    tm = _tile(m, 1024)
    tn = _tile(ncols, 1024)
    assert col0 % tn == 0
    jb = col0 // tn
    return pl.pallas_call(
        _matmul_kernel,
        grid=(ncols // tn, m // tm),
        in_specs=[pl.BlockSpec((tm, k), lambda j, i: (i, 0)),
                  pl.BlockSpec((None, k, tn), lambda j, i: (layer, 0, jb + j))],
        out_specs=pl.BlockSpec((tm, tn), lambda j, i: (i, j)),
        out_shape=jax.ShapeDtypeStruct((m, ncols), out_dtype),
        scratch_shapes=[pltpu.VMEM((k, tn), BF16)],
        compiler_params=_params("parallel", "arbitrary"),
        name="matmul",
    )(x, w)


def _outproj_norm_kernel(xa_ref, wa_ref, xb_ref, wb_ref, res_ref, g_ref, o_ref, h_ref):
    acc = res_ref[...] + jnp.dot(xa_ref[...], wa_ref[...], preferred_element_type=F32)
    acc = acc + jnp.dot(xb_ref[...], wb_ref[...], preferred_element_type=F32)
    o_ref[...] = acc
    ms = jnp.mean(acc * acc, axis=-1, keepdims=True)
    h_ref[...] = (acc * lax.rsqrt(ms + EPS) * g_ref[...]).astype(h_ref.dtype)


def outproj_norm(xa, xb, w, layer, res, gain):
    m, n = res.shape
    k = xa.shape[1]
    tm = _tile(m, 512)
    return pl.pallas_call(
        _outproj_norm_kernel,
        grid=(m // tm,),
        in_specs=[pl.BlockSpec((tm, k), lambda i: (i, 0)),
                  pl.BlockSpec((None, k, n), lambda i: (layer, 0, 0)),
                  pl.BlockSpec((tm, k), lambda i: (i, 0)),
                  pl.BlockSpec((None, k, n), lambda i: (layer, 1, 0)),
                  pl.BlockSpec((tm, n), lambda i: (i, 0)),
                  pl.BlockSpec((1, n), lambda i: (0, 0))],
        out_specs=[pl.BlockSpec((tm, n), lambda i: (i, 0)), pl.BlockSpec((tm, n), lambda i: (i, 0))],
        out_shape=[jax.ShapeDtypeStruct((m, n), F32), jax.ShapeDtypeStruct((m, n), BF16)],
        compiler_params=_params("parallel"),
        name="outproj_norm",
    )(xa, w, xb, w, res, gain.reshape(1, n))


def _downproj_norm_kernel(x_ref, w_ref, res_ref, g_ref, o_ref, h_ref):
    acc = res_ref[...] + jnp.dot(x_ref[...], w_ref[...], preferred_element_type=F32)
    o_ref[...] = acc
    ms = jnp.mean(acc * acc, axis=-1, keepdims=True)
    h_ref[...] = (acc * lax.rsqrt(ms + EPS) * g_ref[...]).astype(h_ref.dtype)


def downproj_norm(x, w, layer, res, gain, norm_dtype):
    m, n = res.shape
    k = x.shape[1]
    tm = _tile(m, 256)
    return pl.pallas_call(
        _downproj_norm_kernel,
        grid=(m // tm,),
        in_specs=[pl.BlockSpec((tm, k), lambda i: (i, 0)),
                  pl.BlockSpec((None, k, n), lambda i: (layer, 0, 0), pipeline_mode=pl.Buffered(1)),
                  pl.BlockSpec((tm, n), lambda i: (i, 0)),
                  pl.BlockSpec((1, n), lambda i: (0, 0))],
        out_specs=[pl.BlockSpec((tm, n), lambda i: (i, 0)), pl.BlockSpec((tm, n), lambda i: (i, 0))],
        out_shape=[jax.ShapeDtypeStruct((m, n), F32), jax.ShapeDtypeStruct((m, n), norm_dtype)],
        compiler_params=_params("parallel"),
        name="downproj_norm",
    )(x, w, res, gain.reshape(1, n))


def _matmul_conv_kernel(*refs, glu, tm, seq_tiles):
    ncomp = 2 if glu else 1
    x_ref = refs[0]
    w_refs = refs[1:1 + ncomp]
    cw_refs = refs[1 + ncomp:1 + 2 * ncomp]
    b_refs = refs[1 + 2 * ncomp:1 + 3 * ncomp]
    o_ref = refs[1 + 3 * ncomp]
    wb_ref, acc_ref, carry_ref = refs[2 + 3 * ncomp:5 + 3 * ncomp]
    i = pl.program_id(1)
    slot = i % 2
    ps = 1 - slot

    tn = o_ref.shape[1]

    @pl.when(i == 0)
    def _():
        for c in range(ncomp):
            wb_ref[:, c * tn:(c + 1) * tn] = w_refs[c][...].astype(BF16)
        carry_ref[...] = jnp.zeros_like(carry_ref)
        acc_ref[1] = jnp.zeros(acc_ref.shape[1:], F32)

    acc_ref[slot] = jnp.dot(x_ref[...], wb_ref[...], preferred_element_type=F32)

    has_prev = (i + seq_tiles - 1) % seq_tiles != 0
    has_next = i % seq_tiles != 0
    outs = []
    for c in range(ncomp):
        a = acc_ref[ps, :, c * tn:(c + 1) * tn]
        rows = lax.broadcasted_iota(jnp.int32, a.shape, 0)
        prev_row = jnp.where(has_prev, carry_ref[c], 0.0)
        next_row = jnp.where(has_next, acc_ref[slot, 0:1, c * tn:(c + 1) * tn], 0.0)
        up = jnp.where(rows == 0, prev_row, pltpu.roll(a, 1, 0))
        dn = jnp.where(rows == tm - 1, next_row, pltpu.roll(a, tm - 1, 0))
        cw = cw_refs[c]
        outs.append(up * cw[0:1, :] + a * cw[1:2, :] + dn * cw[2:3, :] + b_refs[c][...])
        carry_ref[c] = a[tm - 1:tm, :]
    if glu:
        g, val = outs
        res = g * (1.0 / (1.0 + jnp.exp(-g))) * val
    else:
        res = outs[0]
    o_ref[...] = res.astype(o_ref.dtype)


def matmul_conv(x, w, cw, cb, layer, seq_len, col0, ncols, glu, out_dtype):
    m, k = x.shape
    tm = _tile(seq_len, 1024)
    tn = _tile(ncols, 512 if glu else 1024)
    nm = m // tm
    seq_tiles = seq_len // tm
    assert col0 % tn == 0
    ncomp = 2 if glu else 1
    coffs = [c * (ncols // tn) for c in range(ncomp)]
    woffs = [col0 // tn + o for o in coffs]
    cb3 = cb.reshape(cb.shape[0], 1, cb.shape[1])
    in_specs = [pl.BlockSpec((tm, k), lambda j, i: (jnp.minimum(i, nm - 1), 0))]
    in_specs += [pl.BlockSpec((None, k, tn), lambda j, i, o=o: (layer, 0, o + j)) for o in woffs]
    in_specs += [pl.BlockSpec((None, 3, tn), lambda j, i, o=o: (layer, 0, o + j)) for o in coffs]
    in_specs += [pl.BlockSpec((None, 1, tn), lambda j, i, o=o: (layer, 0, o + j)) for o in coffs]
    args = [x] + [w] * ncomp + [cw] * ncomp + [cb3] * ncomp
    return pl.pallas_call(
        functools.partial(_matmul_conv_kernel, glu=glu, tm=tm, seq_tiles=seq_tiles),
        grid=(ncols // tn, nm + 1),
        in_specs=in_specs,
        out_specs=pl.BlockSpec((tm, tn), lambda j, i: (jnp.maximum(i - 1, 0), j)),
        out_shape=jax.ShapeDtypeStruct((m, ncols), out_dtype),
        scratch_shapes=[pltpu.VMEM((k, ncomp * tn), BF16), pltpu.VMEM((2, tm, ncomp * tn), F32),
                        pltpu.VMEM((ncomp, 1, tn), F32)],
        compiler_params=_params("parallel", "arbitrary"),
        name="matmul_conv_glu" if glu else "matmul_conv",
    )(*args)


def _gmlp_kernel(p_ref, gain_ref, ws_ref, bs_ref, o_ref, *, tm, width):
    p = p_ref[...]
    g = 0.5 * p * (1.0 + lax.erf(p * (1.0 / math.sqrt(2.0))))
    u = g[:, :width]
    v = g[:, width:]
    ms = jnp.mean(v * v, axis=-1, keepdims=True)
    vb = (v * lax.rsqrt(ms + EPS) * gain_ref[...]).astype(BF16)
    gd = width // A_GROUPS
    for c in range(tm // CHUNK):
        r0 = c * CHUNK
        for gi in range(A_GROUPS):
            c0 = gi * gd
            s = jnp.dot(ws_ref[gi], vb[r0:r0 + CHUNK, c0:c0 + gd], preferred_element_type=F32)
            s = s + bs_ref[:, c0:c0 + gd]
            o_ref[r0:r0 + CHUNK, c0:c0 + gd] = (u[r0:r0 + CHUNK, c0:c0 + gd] * s).astype(o_ref.dtype)


def spatial_gating(p, v_gain, w_s, b_s):
    m, two_w = p.shape
    width = two_w // 2
    gd = width // A_GROUPS
    tm = _tile(m, 512)
    bs_full = jnp.repeat(b_s.T, gd, axis=1)
    return pl.pallas_call(
        functools.partial(_gmlp_kernel, tm=tm, width=width),
        grid=(m // tm,),
        in_specs=[pl.BlockSpec((tm, two_w), lambda i: (i, 0)),
                  pl.BlockSpec((1, width), lambda i: (0, 0)),
                  pl.BlockSpec((A_GROUPS, CHUNK, CHUNK), lambda i: (0, 0, 0)),
                  pl.BlockSpec((CHUNK, width), lambda i: (0, 0))],
        out_specs=pl.BlockSpec((tm, width), lambda i: (i, 0)),
        out_shape=jax.ShapeDtypeStruct((m, width), BF16),
        compiler_params=_params("parallel"),
        name="spatial_gating",
    )(p, v_gain.reshape(1, width), w_s.astype(BF16), bs_full)


def _dft_matrices(L):
    n = 2 * L
    r1 = 64
    r0 = L // r1
    t = np.arange(L, dtype=np.int64)[None, :]
    a1 = ((r0 * np.arange(r1, dtype=np.int64))[:, None] * t) % n
    a0 = (np.arange(r0, dtype=np.int64)[:, None] * t) % n
    ca, sa, cb, sb = lax.optimization_barrier((
        jnp.asarray(np.cos(2 * np.pi * a1 / n), F32)[:, None, :],
        jnp.asarray(np.sin(2 * np.pi * a1 / n), F32)[:, None, :],
        jnp.asarray(np.cos(2 * np.pi * a0 / n), F32)[None, :, :],
        jnp.asarray(np.sin(2 * np.pi * a0 / n), F32)[None, :, :]))
    cosm = (ca * cb - sa * sb).astype(BF16).reshape(L, L)
    sinm = (-(sa * cb + ca * sb)).astype(BF16).reshape(L, L)
    return cosm, sinm


def _filter_features(L):
    t = np.linspace(0.0, 1.0, L)[:, None]
    w = 2.0 * np.pi * np.arange(L)[:, None] / L
    bands = np.linspace(1e-4, HYENA_BANDS - 1, HYENA_BANDS)[None, :]
    z = np.concatenate([t, np.cos(w * bands), -np.sin(w * bands)], axis=-1)
    return jnp.asarray(np.pad(z, ((0, 0), (0, 128 - HYENA_EMB))), F32)


def _alternating(shape):
    rows = lax.broadcasted_iota(jnp.int32, shape, 0)
    return jnp.where(rows % 2 == 0, 1.0, -1.0)


def _twiddles(L):
    g = np.arange(L // 4)[:, None] * np.ones((1, LANES))
    a1, a2 = 2.0 * np.pi * g / L, 2.0 * np.pi * g / (2 * L)
    return jnp.asarray(np.stack([np.cos(a1), np.sin(a1), np.cos(a2), np.sin(a2)]), F32)


def _cadd(a, b):
    return a[0] + b[0], a[1] + b[1]


def _csub(a, b):
    return a[0] - b[0], a[1] - b[1]


def _conj(a):
    return a[0], -a[1]


def _cmul(a, b):
    return a[0] * b[0] - a[1] * b[1], a[0] * b[1] + a[1] * b[0]


def _rot(a, c, s):
    return a[0] * c + a[1] * s, a[1] * c - a[0] * s


def _unrot(a, c, s):
    return a[0] * c - a[1] * s, a[1] * c + a[0] * s


def _times_i(a):
    return -a[1], a[0]


def _times_minus_i(a):
    return a[1], -a[0]


def _lane_block(x, j):
    return x[:, j * LANES:(j + 1) * LANES]


def _period8_patterns():
    t = np.arange(8)[:, None] * np.ones((1, LANES))
    rows = [np.cos(np.pi * t)]
    for k in (2, 1, 3):
        rows += [np.cos(k * np.pi * t / 4), np.sin(k * np.pi * t / 4)]
    return jnp.asarray(np.round(np.stack(rows), 15), F32)


def _filter_spectrum_kernel(z_ref, w1_ref, b1_ref, w2_ref, b2_ref, fr_ref, w3f_ref, w3b_ref, dl_ref,
                            pat_ref, cos_ref, sin_ref, o_ref, rc_ref, rs_ref, nyq_ref, h_ref, taps_ref,
                            *, n):
    m = pl.program_id(2)
    hp = lax.Precision.HIGHEST

    @pl.when(jnp.logical_and(jnp.logical_and(pl.program_id(0) == 0, pl.program_id(1) == 0), m == 0))
    def _():
        h = jnp.dot(z_ref[...], w1_ref[...], precision=hp, preferred_element_type=F32) + b1_ref[...]
        h = jnp.sin(fr_ref[0:1, :] * h)
        h = jnp.dot(h, w2_ref[...], precision=hp, preferred_element_type=F32) + b2_ref[...]
        h_ref[...] = jnp.sin(fr_ref[1:2, :] * h)

    @pl.when(m == 0)
    def _():
        length = z_ref.shape[0]
        chunk = min(length, 2 * DFT_TILE)
        nrm = None
        tc = taps_ref.shape[2]
        w3 = jnp.concatenate([w3f_ref[...], w3b_ref[...]], axis=1)
        for r0 in range(0, length, chunk):
            rs = slice(r0, r0 + chunk)
            decay = jnp.exp(-z_ref[rs, 0:1] * dl_ref[...])
            hfb = jnp.dot(h_ref[rs, :], w3, precision=hp, preferred_element_type=F32)
            hf, hb = hfb[:, :tc] * decay, hfb[:, tc:] * decay
            if r0 == 0:
                hb = jnp.where(lax.broadcasted_iota(jnp.int32, hb.shape, 0) == 0, 0.0, hb)
            part = jnp.sum(jnp.abs(hf) + jnp.abs(hb), axis=0, keepdims=True)
            nrm = part if nrm is None else nrm + part
            taps_ref[0, rs, :] = hf + hb
            taps_ref[1, rs, :] = hf - hb
        inv = 1.0 / nrm
        packed = [None] * 4
        alt, hc, hs, qc, qs, q3c, q3s = (pltpu.repeat(pat_ref[i], chunk // 8, axis=0) for i in range(7))
        for r0 in range(0, length, chunk):
            rs = slice(r0, r0 + chunk)
            ke = taps_ref[0, rs, :] * inv
            ko = taps_ref[1, rs, :] * inv
            for blk, (vc, vs) in enumerate(((ke, ko), (ke * alt, -ko * alt), (ke * hc, ke * hs),
                                            (ko * hs, ko * hc))):
                rc_ref[rs, blk * tc:(blk + 1) * tc] = vc.astype(BF16)
                rs_ref[rs, blk * tc:(blk + 1) * tc] = vs.astype(BF16)
            parts = (jnp.sum(ke * qc, axis=0, keepdims=True), -jnp.sum(ko * qs, axis=0, keepdims=True),
                     jnp.sum(ke * q3c, axis=0, keepdims=True), -jnp.sum(ko * q3s, axis=0, keepdims=True))
            packed = [pt if acc is None else acc + pt for acc, pt in zip(packed, parts)]
        for i in range(4):
            nyq_ref[i:i + 1, :] = packed[i] * (2.0 / n)

    tc = o_ref.shape[2]
    rc = jnp.dot(cos_ref[...], rc_ref[...], preferred_element_type=F32)
    rs = jnp.dot(sin_ref[...], rs_ref[...], preferred_element_type=F32)
    c = [rc[:, i * tc:(i + 1) * tc] for i in range(4)]
    s = [rs[:, i * tc:(i + 1) * tc] for i in range(4)]
    planes = [c[0], s[0], c[1], s[1], c[2] - s[2], -c[3] - s[3], c[2] + s[2], s[3] - c[3]]
    first = jnp.logical_and(lax.broadcasted_iota(jnp.int32, c[0].shape, 0) == 0, m == 0)
    scale = jnp.where(first, 1.0 / n, 2.0 / n)
    slots = {1: 2, 3: 3, 6: 0, 7: 1}
    for i in range(8):
        val = planes[i] * scale
        if i in slots:
            val = jnp.where(first, nyq_ref[slots[i]:slots[i] + 1, :], val)
        o_ref[i] = val


def hyena_filter_spectrum(L, width, cosm, sinm, w1, b1, w2, b2, w3, freq):
    z = _filter_features(L)
    tc = LANES
    ncb = width // tc
    max_decay = math.log(HYENA_TARGET) / HYENA_FAST
    min_decay = math.log(HYENA_TARGET) / HYENA_SLOW
    deltas = jnp.asarray(np.abs(np.linspace(min_decay, max_decay, width)), F32).reshape(1, width)
    w1p = jnp.pad(w1, ((0, 128 - HYENA_EMB), (0, 0)))
    const = lambda o, c, m: (0, 0)
    tm = _tile(L // 4, DFT_TILE)
    return pl.pallas_call(
        functools.partial(_filter_spectrum_kernel, n=2 * L),
        grid=(HYENA_ORDER, ncb, L // 4 // tm),
        in_specs=[pl.BlockSpec((L, 128), const),
                  pl.BlockSpec((128, HYENA_FFN), const), pl.BlockSpec((1, HYENA_FFN), const),
                  pl.BlockSpec((HYENA_FFN, HYENA_FFN), const), pl.BlockSpec((1, HYENA_FFN), const),
                  pl.BlockSpec((2, HYENA_FFN), const),
                  pl.BlockSpec((HYENA_FFN, tc), lambda o, c, m: (0, (2 * o) * ncb + c)),
                  pl.BlockSpec((HYENA_FFN, tc), lambda o, c, m: (0, (2 * o + 1) * ncb + c)),
                  pl.BlockSpec((1, tc), lambda o, c, m: (0, c)),
                  pl.BlockSpec((7, 8, LANES), lambda o, c, m: (0, 0, 0)),
                  pl.BlockSpec((tm, L), lambda o, c, m: (m, 0)),
                  pl.BlockSpec((tm, L), lambda o, c, m: (m, 0))],
        out_specs=pl.BlockSpec((None, 8, tm, tc), lambda o, c, m: (o, 0, m, c)),
        out_shape=jax.ShapeDtypeStruct((HYENA_ORDER, 8, L // 4, width), F32),
        scratch_shapes=[pltpu.VMEM((L, 4 * tc), BF16), pltpu.VMEM((L, 4 * tc), BF16),
                        pltpu.VMEM((4, tc), F32), pltpu.VMEM((L, HYENA_FFN), F32),
                        pltpu.VMEM((2, L, tc), F32)],
        compiler_params=_params("arbitrary", "arbitrary", "arbitrary"),
        name="hyena_filter_spectrum",
    )(z, w1p, b1.reshape(1, -1), w2, b2.reshape(1, -1), freq, w3, w3, deltas, _period8_patterns(),
      cosm, sinm)


def _dft_forward_kernel(u_ref, cos_ref, sin_ref, tw_ref, kf_ref, o_ref, ub_ref):
    q = ub_ref.shape[0]
    nyq = []
    for j in range(4):
        xj = u_ref[pl.ds(j, q, stride=4), :]
        ub_ref[:, j * LANES:(j + 1) * LANES] = xj.astype(BF16)
        nyq.append(jnp.sum(xj * _alternating(xj.shape), axis=0, keepdims=True))
    eq, oq = (nyq[0], -nyq[2]), (nyq[1], -nyq[3])
    woq = _rot(oq, math.sqrt(0.5), math.sqrt(0.5))
    u_q, u_3q = _cadd(eq, woq), _conj(_csub(eq, woq))
    chunk = min(q, DFT_TILE)
    for r0 in range(0, q, chunk):
        rs = slice(r0, r0 + chunk)
        re = jnp.dot(cos_ref[rs, :], ub_ref[...], preferred_element_type=F32)
        im = jnp.dot(sin_ref[rs, :], ub_ref[...], preferred_element_type=F32)
        x = [(_lane_block(re, j), _lane_block(im, j)) for j in range(4)]
        c1, s1, c2, s2 = tw_ref[0, rs, :], tw_ref[1, rs, :], tw_ref[2, rs, :], tw_ref[3, rs, :]
        t2, t3 = _rot(x[2], c1, s1), _rot(x[3], c1, s1)
        ea, eb = _cadd(x[0], t2), _csub(x[0], t2)
        oa, ob = _cadd(x[1], t3), _csub(x[1], t3)
        ta, tb = _rot(oa, c2, s2), _times_i(_rot(ob, c2, s2))
        p = [_cadd(ea, ta), _conj(_csub(ea, ta)), _conj(_cadd(eb, tb)), _csub(eb, tb)]
        k = [(kf_ref[2 * i, rs, :], kf_ref[2 * i + 1, rs, :]) for i in range(4)]
        if r0 == 0:
            first = lax.broadcasted_iota(jnp.int32, re.shape[:1] + (LANES,), 0) == 0
            p[3] = (jnp.where(first, u_q[0], p[3][0]), jnp.where(first, u_q[1], p[3][1]))
        y = [_cmul(p[i], k[i]) for i in range(4)]
        if r0 == 0:
            y3q = _cmul(u_3q, (k[0][1], k[1][1]))
            y[0] = (y[0][0], jnp.where(first, y3q[0], y[0][1]))
            y[1] = (y[1][0], jnp.where(first, y3q[1], y[1][1]))
        for i in range(4):
            o_ref[2 * i, rs, :] = y[i][0].astype(o_ref.dtype)
            o_ref[2 * i + 1, rs, :] = y[i][1].astype(o_ref.dtype)


def _resident(shape):
    return pl.BlockSpec(shape, lambda c, b: (0,) * len(shape), pipeline_mode=pl.Buffered(1))


def dft_forward_times_filter(u, col0, width, cos4, sin4, tw, kf, order):
    bsz, L, _ = u.shape
    q = L // 4
    assert col0 % LANES == 0
    cb0 = col0 // LANES
    return pl.pallas_call(
        _dft_forward_kernel,
        grid=(width // LANES, bsz),
        in_specs=[pl.BlockSpec((None, L, LANES), lambda c, b: (b, 0, cb0 + c)),
                  _resident((q, q)), _resident((q, q)), _resident((4, q, LANES)),
                  pl.BlockSpec((None, 8, q, LANES), lambda c, b: (order, 0, 0, c))],
        out_specs=pl.BlockSpec((None, 8, q, LANES), lambda c, b: (b, 0, 0, c)),
        out_shape=jax.ShapeDtypeStruct((bsz, 8, q, width), BF16),
        scratch_shapes=[pltpu.VMEM((q, 4 * LANES), BF16)],
        compiler_params=_params("parallel", "parallel"),
        name="dft_forward",
    )(u, cos4, sin4, tw, kf)


def _dft_inverse_kernel(y_ref, cos_ref, sin_ref, tw_ref, u_ref, g_ref, skip_ref, o_ref, x_ref, res_ref):
    q = x_ref.shape[1]

    def row0(i):
        return y_ref[i, 0:1, :].astype(F32)

    u_q, u_3q = (row0(6), row0(7)), (row0(1), row0(3))
    eq = _cadd(u_q, _conj(u_3q))
    oq = _unrot(_csub(u_q, _conj(u_3q)), math.sqrt(0.5), math.sqrt(0.5))
    nyq = jnp.concatenate([eq[0], oq[0], -eq[1], -oq[1]], axis=1)
    chunk = min(q, DFT_TILE)
    for r0 in range(0, q, chunk):
        rs = slice(r0, r0 + chunk)
        p = [(y_ref[2 * i, rs, :].astype(F32), y_ref[2 * i + 1, rs, :].astype(F32)) for i in range(4)]
        if r0 == 0:
            first = lax.broadcasted_iota(jnp.int32, p[0][0].shape, 0) == 0
            p[0] = (p[0][0], jnp.where(first, 0.0, p[0][1]))
            p[1] = (p[1][0], jnp.where(first, 0.0, p[1][1]))
            p[3] = (jnp.where(first, p[2][0], p[3][0]), jnp.where(first, p[2][1], p[3][1]))
        c1, s1, c2, s2 = tw_ref[0, rs, :], tw_ref[1, rs, :], tw_ref[2, rs, :], tw_ref[3, rs, :]
        ea, ta = _cadd(p[0], _conj(p[1])), _csub(p[0], _conj(p[1]))
        eb, tb = _cadd(_conj(p[2]), p[3]), _csub(_conj(p[2]), p[3])
        oa = _unrot(ta, c2, s2)
        ob = _times_minus_i(_unrot(tb, c2, s2))
        x = [_cadd(ea, eb), _cadd(oa, ob), _unrot(_csub(ea, eb), c1, s1), _unrot(_csub(oa, ob), c1, s1)]
        for j in range(4):
            x_ref[0, rs, j * LANES:(j + 1) * LANES] = x[j][0].astype(BF16)
            x_ref[1, rs, j * LANES:(j + 1) * LANES] = x[j][1].astype(BF16)
    skip = skip_ref[...]
    alt = _alternating((chunk, 4 * LANES))
    for r0 in range(0, q, chunk):
        rs = slice(r0, r0 + chunk)
        conv = jnp.dot(cos_ref[rs, :], x_ref[0], preferred_element_type=F32)
        conv = conv + jnp.dot(sin_ref[rs, :], x_ref[1], preferred_element_type=F32)
        conv = conv + alt * nyq
        for j in range(4):
            rows = pl.ds(4 * r0 + j, chunk, stride=4)
            res_ref[rows, :] = g_ref[rows, :] * (_lane_block(conv, j) + u_ref[rows, :] * skip)
    o_ref[...] = res_ref[...].astype(o_ref.dtype)


def dft_inverse_gated(y, cos4, sin4, tw, u, ucol0, gate, gcol0, skip, out_dtype):
    bsz, _, q, width = y.shape
    L = 4 * q
    ub0, gb0 = ucol0 // LANES, gcol0 // LANES
    return pl.pallas_call(
        _dft_inverse_kernel,
        grid=(width // LANES, bsz),
        in_specs=[pl.BlockSpec((None, 8, q, LANES), lambda c, b: (b, 0, 0, c)),
                  _resident((q, q)), _resident((q, q)), _resident((4, q, LANES)),
                  pl.BlockSpec((None, L, LANES), lambda c, b: (b, 0, ub0 + c)),
                  pl.BlockSpec((None, L, LANES), lambda c, b: (b, 0, gb0 + c)),
                  pl.BlockSpec((1, LANES), lambda c, b: (0, c))],
        out_specs=pl.BlockSpec((None, L, LANES), lambda c, b: (b, 0, c)),
        out_shape=jax.ShapeDtypeStruct((bsz, L, width), out_dtype),
        scratch_shapes=[pltpu.VMEM((2, q, 4 * LANES), BF16), pltpu.VMEM((L, LANES), F32)],
        compiler_params=_params("parallel", "parallel"),
        name="dft_inverse",
    )(y, cos4, sin4, tw, u, gate, skip.reshape(1, width))


NA_GROUP = 8
NA_BLOCK = NA_GROUP + NA_KH
NA_PAD = NA_KH // 2


def _na_plans():
    tiles = {}
    plans = []
    for variant in range(3):
        plan = []
        for rq in range(NA_GROUP):
            lo = (max(rq, NA_PAD), rq, min(rq, NA_PAD))[variant]
            a_lo, a_hi = lo // 2, (lo + NA_KH - 1) // 2 + 1
            ids = []
            for a in range(a_lo, a_hi):
                key = tuple(kr - rq + NA_KH - 1 - NA_PAD if lo <= kr < lo + NA_KH else -1
                            for kr in (2 * a, 2 * a + 1))
                ids.append(tiles.setdefault(key, len(tiles)))
            plan.append((a_lo, tuple(ids)))
        plans.append(tuple(plan))
    return tuple(plans), list(tiles)


def _na_kernel(q_ref, k_ref, v_ref, tiles_ref, o_ref, kb_ref, vb_ref, p_ref, *, rows, scale, plans):
    L = rows * GRID_W
    pad = NA_PAD * GRID_W
    gq = NA_GROUP * GRID_W
    gk = NA_BLOCK * GRID_W
    for ref, src in ((kb_ref, k_ref), (vb_ref, v_ref)):
        ref[0:pad, :] = jnp.zeros((pad, HEAD_DIM), BF16)
        ref[pad + L:, :] = jnp.zeros((pad, HEAD_DIM), BF16)
        ref[pad:pad + L, :] = src[...].astype(BF16)

    def group(q0, plan):
        q = q_ref[pl.ds(q0, gq), :].astype(BF16)
        s = lax.dot_general(q, kb_ref[pl.ds(q0, gk), :], (((1,), (1,)), ((), ())),
                            preferred_element_type=F32)
        p_ref[...] = jnp.zeros(p_ref.shape, BF16)
        for rq, (a_lo, ids) in enumerate(plan):
            r0, c0, c1 = rq * GRID_W, 2 * GRID_W * a_lo, 2 * GRID_W * (a_lo + len(ids))
            bias = jnp.concatenate([tiles_ref[t] for t in ids], axis=1)
            sl = s[r0:r0 + GRID_W, c0:c1] * scale + bias
            e = jnp.exp(sl - jnp.max(sl, axis=-1, keepdims=True))
            pr = e * (1.0 / jnp.sum(e, axis=-1, keepdims=True))
            p_ref[r0:r0 + GRID_W, c0:c1] = pr.astype(BF16)
        o = jnp.dot(p_ref[...], vb_ref[pl.ds(q0, gk), :], preferred_element_type=F32)
        o_ref[pl.ds(q0, gq), :] = o.astype(o_ref.dtype)

    ngroups = rows // NA_GROUP
    group(0, plans[0])

    def middle(g, carry):
        group(pl.multiple_of(g * gq, gq), plans[1])
        return carry

    lax.fori_loop(1, ngroups - 1, middle, 0)
    group((ngroups - 1) * gq, plans[2])


def _na_bias_tiles(rpb, tile_keys):
    col = np.arange(GRID_W)
    cs = np.clip(col - NA_KW // 2, 0, GRID_W - NA_KW)
    kc = np.arange(GRID_W)
    inwin = (kc[None, :] >= cs[:, None]) & (kc[None, :] < cs[:, None] + NA_KW)
    rel_col = np.clip(kc[None, :] - col[:, None] + (NA_KW - 1), 0, 2 * NA_KW - 2)
    t = jnp.where(jnp.asarray(inwin)[None, None], rpb[:, :, rel_col], MASK_VALUE)
    masked = jnp.full((rpb.shape[0], 1, GRID_W, GRID_W), MASK_VALUE, F32)
    t = jnp.concatenate([t.astype(F32), masked], axis=1)
    left = np.array([k[0] for k in tile_keys])
    right = np.array([k[1] for k in tile_keys])
    return jnp.concatenate([t[:, left], t[:, right]], axis=-1)


def neighborhood_attention(p, bsz, L, heads, rpb):
    rows = L // GRID_W
    assert rows % NA_GROUP == 0 and rows >= 2 * NA_GROUP
    plans, tile_keys = _na_plans()
    tiles = _na_bias_tiles(rpb, tile_keys)
    nt = len(tile_keys)
    lp = L + 2 * NA_PAD * GRID_W
    return pl.pallas_call(
        functools.partial(_na_kernel, rows=rows, scale=HEAD_DIM ** -0.5, plans=plans),
        grid=(bsz, heads),
        in_specs=[pl.BlockSpec((None, L, HEAD_DIM), lambda b, h: (b, 0, h)),
                  pl.BlockSpec((None, L, HEAD_DIM), lambda b, h: (b, 0, heads + h)),
                  pl.BlockSpec((None, L, HEAD_DIM), lambda b, h: (b, 0, 2 * heads + h)),
                  pl.BlockSpec((None, nt, GRID_W, 2 * GRID_W), lambda b, h: (h, 0, 0, 0))],
        out_specs=pl.BlockSpec((None, L, HEAD_DIM), lambda b, h: (b, 0, h)),
        out_shape=jax.ShapeDtypeStruct((bsz, L, heads * HEAD_DIM), BF16),
        scratch_shapes=[pltpu.VMEM((lp, HEAD_DIM), BF16), pltpu.VMEM((lp, HEAD_DIM), BF16),
                        pltpu.VMEM((NA_GROUP * GRID_W, NA_BLOCK * GRID_W), BF16)],
        compiler_params=_params("parallel", "parallel"),
        name="neighborhood_attention",
    )(p, p, p, tiles)


DIFF_SUB_ROWS = 256


def _rope(x, cos, sin_signed, half):
    lane = lax.broadcasted_iota(jnp.int32, x.shape, 1)
    width = x.shape[1]
    first = (lane % (2 * half)) < half
    partner = jnp.where(first, pltpu.roll(x, width - half, 1), pltpu.roll(x, half, 1))
    return x * cos + partner * sin_signed


def _diff_attn_kernel(q_ref, k_ref, v_ref, cq_ref, sq_ref, ck_ref, sk_ref, lam_ref, sub_ref, o_ref,
                      kb_ref, vb_ref, *, lam_init, scale, half, n_sub):
    qi = pl.program_id(2)

    @pl.when(qi == 0)
    def _():
        kb_ref[...] = _rope(k_ref[...], ck_ref[...], sk_ref[...], half).astype(BF16)
        vb_ref[...] = v_ref[...].astype(BF16)

    lp = lam_ref[...]
    lam = (jnp.exp(jnp.sum(lp[0:1] * lp[1:2], axis=-1, keepdims=True))
           - jnp.exp(jnp.sum(lp[2:3] * lp[3:4], axis=-1, keepdims=True)) + lam_init)
    kb = kb_ref[...]
    tq = q_ref.shape[0]
    sub = tq // n_sub
    for sb in range(n_sub):
        rs = slice(sb * sub, (sb + 1) * sub)
        q = _rope(q_ref[rs, :], cq_ref[rs, :], sq_ref[rs, :], half) * (scale * math.log2(math.e))
        lane = lax.broadcasted_iota(jnp.int32, q.shape, 1)

        def softmax_map(in_map, weight):
            qm = jnp.where(in_map, q, 0.0).astype(BF16)
            s = lax.dot_general(qm, kb, (((1,), (1,)), ((), ())), preferred_element_type=F32)
            e = jnp.exp2(s - jnp.max(s, axis=-1, keepdims=True))
            return e * (weight / jnp.sum(e, axis=-1, keepdims=True))

        a = softmax_map(lane < 2 * half, 1.0) - softmax_map(lane >= 2 * half, lam)
        o = jnp.dot(a.astype(BF16), vb_ref[...], preferred_element_type=F32)
        ms = jnp.mean(o * o, axis=-1, keepdims=True)
        o_ref[rs, :] = (o * lax.rsqrt(ms + EPS) * sub_ref[...] * (1.0 - lam_init)).astype(o_ref.dtype)


def _rope_tables(L, half):
    inv = (1.0 / (np.float32(ROPE_THETA) ** (np.arange(half, dtype=np.float32) * np.float32(2.0)
                                             / np.float32(2 * half)))).astype(np.float32)
    ang = (np.arange(L, dtype=np.float32)[:, None] * inv[None, :]).astype(np.float64)
    cos, sin = np.cos(ang), np.sin(ang)
    reps = HEAD_DIM // (2 * half)
    cos_t = np.tile(cos, (1, 2 * reps))
    sin_t = np.tile(np.concatenate([-sin, sin], axis=1), (1, reps))
    return jnp.asarray(cos_t, F32), jnp.asarray(sin_t, F32)


def diff_attention(p, col0, bsz, L, heads, lam_params, subln, lam_init):
    half = HEAD_DIM // 4
    assert col0 % HEAD_DIM == 0
    c0 = col0 // HEAD_DIM
    tq = _tile(L, 2 * DIFF_SUB_ROWS)
    n_sub = tq // DIFF_SUB_ROWS
    cos_t, sin_t = _rope_tables(L, half)
    return pl.pallas_call(
        functools.partial(_diff_attn_kernel, lam_init=lam_init, scale=(2 * half) ** -0.5, half=half,
                          n_sub=n_sub),
        grid=(bsz, heads, L // tq),
        in_specs=[pl.BlockSpec((None, tq, HEAD_DIM), lambda b, h, i: (b, i, c0 + h)),
                  pl.BlockSpec((None, L, HEAD_DIM), lambda b, h, i: (b, 0, c0 + heads + h)),
                  pl.BlockSpec((None, L, HEAD_DIM), lambda b, h, i: (b, 0, c0 + 2 * heads + h)),
                  pl.BlockSpec((tq, HEAD_DIM), lambda b, h, i: (i, 0)),
                  pl.BlockSpec((tq, HEAD_DIM), lambda b, h, i: (i, 0)),
                  pl.BlockSpec((L, HEAD_DIM), lambda b, h, i: (0, 0)),
                  pl.BlockSpec((L, HEAD_DIM), lambda b, h, i: (0, 0)),
                  pl.BlockSpec((4, 2 * half), lambda b, h, i: (0, 0)),
                  pl.BlockSpec((1, HEAD_DIM), lambda b, h, i: (0, 0))],
        out_specs=pl.BlockSpec((None, tq, HEAD_DIM), lambda b, h, i: (b, i, h)),
        out_shape=jax.ShapeDtypeStruct((bsz, L, heads * HEAD_DIM), BF16),
        scratch_shapes=[pltpu.VMEM((L, HEAD_DIM), BF16), pltpu.VMEM((L, HEAD_DIM), BF16)],
        compiler_params=_params("parallel", "parallel", "arbitrary"),
        name="diff_attention",
    )(p, p, p, cos_t, sin_t, cos_t, sin_t, lam_params, subln.reshape(1, HEAD_DIM))


def kernel(x, norm_mix, norm_ffn, w_out, ffn_up, ffn_conv_w, ffn_conv_b, ffn_down, final_norm,
           ab_w_in, a_vnorm, a_ws, a_bs, b_conv_w, b_conv_b, b_filt_w1, b_filt_b1, b_filt_w2,
           b_filt_b2, b_filt_w3, b_filt_freq, b_skip, cd_w_in, c_rpb, d_lambda, d_subln):
    bsz, L, d = x.shape
    depth = norm_mix.shape[0]
    m = bsz * L
    half_w = d // 2
    ff = ffn_down.shape[1]
    heads = half_w // HEAD_DIM
    assert heads == D_HEADS and half_w // A_GROUPS == CHUNK and L % GRID_W == 0
    xs = x.reshape(m, d)
    cosm, sinm = _dft_matrices(L)
    cos4, sin4 = _dft_matrices(L // 4)
    tw = _twiddles(L)
    w_out_b = w_out.astype(BF16)
    ffn_down_b = ffn_down.astype(BF16)
    h = rmsnorm(xs, norm_mix[0], BF16)
    for l in range(depth):
        i = l // 2
        if l % 2 == 0:
            pa = matmul(h, ab_w_in, i, 0, 2 * half_w, F32)
            ya = spatial_gating(pa, a_vnorm[i], a_ws[i], a_bs[i])
            pb = matmul_conv(h, ab_w_in, b_conv_w, b_conv_b, i, L, 2 * half_w, 3 * half_w, False, F32)
            pb = pb.reshape(bsz, L, 3 * half_w)
            kf = hyena_filter_spectrum(L, half_w, cosm, sinm, b_filt_w1[i], b_filt_b1[i], b_filt_w2[i],
                                       b_filt_b2[i], b_filt_w3[i], b_filt_freq[i])
            y1 = dft_forward_times_filter(pb, 0, half_w, cos4, sin4, tw, kf, 0)
            z = dft_inverse_gated(y1, cos4, sin4, tw, pb, 0, pb, half_w, b_skip[i, 0], F32)
            y2 = dft_forward_times_filter(z, 0, half_w, cos4, sin4, tw, kf, 1)
            yb = dft_inverse_gated(y2, cos4, sin4, tw, z, 0, pb, 2 * half_w, b_skip[i, 1], BF16)
            yb = yb.reshape(m, half_w)
        else:
            p = matmul(h, cd_w_in, i, 0, 6 * half_w, F32).reshape(bsz, L, 6 * half_w)
            lam_init = 0.8 - 0.6 * math.exp(-0.3 * l)
            ya = neighborhood_attention(p, bsz, L, heads, c_rpb[i]).reshape(m, half_w)
            yb = diff_attention(p, 3 * half_w, bsz, L, heads, d_lambda[i], d_subln[i], lam_init)
            yb = yb.reshape(m, half_w)
        xs, hf = outproj_norm(ya, yb, w_out_b, l, xs, norm_ffn[l])
        act = matmul_conv(hf, ffn_up, ffn_conv_w, ffn_conv_b, l, L, 0, ff, True, BF16)
        last = l == depth - 1
        xs, h = downproj_norm(act, ffn_down_b, l, xs, final_norm if last else norm_mix[l + 1],
                              F32 if last else BF16)
    return h.reshape(bsz, L, d)
```

```python
import functools
import math

import numpy as np
import jax
import jax.numpy as jnp
from jax import lax
from jax.experimental import pallas as pl
from jax.experimental.pallas import tpu as pltpu

F32 = jnp.float32
BF16 = jnp.bfloat16

EPS = 1e-6
GRID_W = 64
HEAD_DIM = 128
CHUNK = 128
A_GROUPS = 8
NA_KH = 8
NA_KW = 16
D_HEADS = 8
ROPE_THETA = 10000.0
HYENA_ORDER = 2
HYENA_EMB = 33
HYENA_BANDS = (HYENA_EMB - 1) // 2
HYENA_FFN = 64
HYENA_TARGET = 1e-2
HYENA_FAST = 0.3
HYENA_SLOW = 1.5
MASK_VALUE = -1e30

VMEM_LIMIT_BYTES = 56 * 1024 * 1024
DFT_TILE = 512
LANES = 128


def _params(*sem):
    return pltpu.CompilerParams(dimension_semantics=sem, vmem_limit_bytes=VMEM_LIMIT_BYTES)


def _tile(n, pref):
    t = min(n, pref)
    assert n % t == 0, (n, pref)
    return t


def _rmsnorm_kernel(x_ref, g_ref, o_ref):
    x = x_ref[...]
    ms = jnp.mean(x * x, axis=-1, keepdims=True)
    o_ref[...] = (x * lax.rsqrt(ms + EPS) * g_ref[...]).astype(o_ref.dtype)


def rmsnorm(x, g, out_dtype):
    m, d = x.shape
    tm = _tile(m, 512)
    return pl.pallas_call(
        _rmsnorm_kernel,
        grid=(m // tm,),
        in_specs=[pl.BlockSpec((tm, d), lambda i: (i, 0)), pl.BlockSpec((1, d), lambda i: (0, 0))],
        out_specs=pl.BlockSpec((tm, d), lambda i: (i, 0)),
        out_shape=jax.ShapeDtypeStruct((m, d), out_dtype),
        compiler_params=_params("parallel"),
        name="rmsnorm",
    )(x, g.reshape(1, d))


def _matmul_kernel(x_ref, w_ref, o_ref, wb_ref):
    @pl.when(pl.program_id(1) == 0)
    def _():
        wb_ref[...] = w_ref[...].astype(BF16)

    o_ref[...] = jnp.dot(x_ref[...], wb_ref[...], preferred_element_type=F32).astype(o_ref.dtype)


def matmul(x, w, layer, col0, ncols, out_dtype):
    m, k = x.shape
    tm = _tile(m, 1024)
    tn = _tile(ncols, 1024)
    assert col0 % tn == 0
    jb = col0 // tn
    return pl.pallas_call(
        _matmul_kernel,
        grid=(ncols // tn, m // tm),
        in_specs=[pl.BlockSpec((tm, k), lambda j, i: (i, 0)),
                  pl.BlockSpec((None, k, tn), lambda j, i: (layer, 0, jb + j))],
        out_specs=pl.BlockSpec((tm, tn), lambda j, i: (i, j)),
        out_shape=jax.ShapeDtypeStruct((m, ncols), out_dtype),
        scratch_shapes=[pltpu.VMEM((k, tn), BF16)],
        compiler_params=_params("parallel", "arbitrary"),
        name="matmul",
    )(x, w)


def _outproj_norm_kernel(xa_ref, wa_ref, xb_ref, wb_ref, res_ref, g_ref, o_ref, h_ref):
    acc = res_ref[...] + jnp.dot(xa_ref[...], wa_ref[...], preferred_element_type=F32)
    acc = acc + jnp.dot(xb_ref[...], wb_ref[...], preferred_element_type=F32)
    o_ref[...] = acc
    ms = jnp.mean(acc * acc, axis=-1, keepdims=True)
    h_ref[...] = (acc * lax.rsqrt(ms + EPS) * g_ref[...]).astype(h_ref.dtype)


def outproj_norm(xa, xb, w, layer, res, gain):
    m, n = res.shape
    k = xa.shape[1]
    tm = _tile(m, 512)
    return pl.pallas_call(
        _outproj_norm_kernel,
        grid=(m // tm,),
        in_specs=[pl.BlockSpec((tm, k), lambda i: (i, 0)),
                  pl.BlockSpec((None, k, n), lambda i: (layer, 0, 0)),
                  pl.BlockSpec((tm, k), lambda i: (i, 0)),
                  pl.BlockSpec((None, k, n), lambda i: (layer, 1, 0)),
                  pl.BlockSpec((tm, n), lambda i: (i, 0)),
                  pl.BlockSpec((1, n), lambda i: (0, 0))],
        out_specs=[pl.BlockSpec((tm, n), lambda i: (i, 0)), pl.BlockSpec((tm, n), lambda i: (i, 0))],
        out_shape=[jax.ShapeDtypeStruct((m, n), F32), jax.ShapeDtypeStruct((m, n), BF16)],
        compiler_params=_params("parallel"),
        name="outproj_norm",
    )(xa, w, xb, w, res, gain.reshape(1, n))


def _downproj_norm_kernel(x_ref, w_ref, res_ref, g_ref, o_ref, h_ref):
    acc = res_ref[...] + jnp.dot(x_ref[...], w_ref[...], preferred_element_type=F32)
    o_ref[...] = acc
    ms = jnp.mean(acc * acc, axis=-1, keepdims=True)
    h_ref[...] = (acc * lax.rsqrt(ms + EPS) * g_ref[...]).astype(h_ref.dtype)


def downproj_norm(x, w, layer, res, gain, norm_dtype):
    m, n = res.shape
    k = x.shape[1]
    tm = _tile(m, 256)
    return pl.pallas_call(
        _downproj_norm_kernel,
        grid=(m // tm,),
        in_specs=[pl.BlockSpec((tm, k), lambda i: (i, 0)),
                  pl.BlockSpec((None, k, n), lambda i: (layer, 0, 0), pipeline_mode=pl.Buffered(1)),
                  pl.BlockSpec((tm, n), lambda i: (i, 0)),
                  pl.BlockSpec((1, n), lambda i: (0, 0))],
        out_specs=[pl.BlockSpec((tm, n), lambda i: (i, 0)), pl.BlockSpec((tm, n), lambda i: (i, 0))],
        out_shape=[jax.ShapeDtypeStruct((m, n), F32), jax.ShapeDtypeStruct((m, n), norm_dtype)],
        compiler_params=_params("parallel"),
        name="downproj_norm",
    )(x, w, res, gain.reshape(1, n))


def _matmul_conv_kernel(*refs, glu, tm, seq_tiles):
    ncomp = 2 if glu else 1
    x_ref = refs[0]
    w_refs = refs[1:1 + ncomp]
    cw_refs = refs[1 + ncomp:1 + 2 * ncomp]
    b_refs = refs[1 + 2 * ncomp:1 + 3 * ncomp]
    o_ref = refs[1 + 3 * ncomp]
    wb_ref, acc_ref, carry_ref = refs[2 + 3 * ncomp:5 + 3 * ncomp]
    i = pl.program_id(1)
    slot = i % 2
    ps = 1 - slot

    tn = o_ref.shape[1]

    @pl.when(i == 0)
    def _():
        for c in range(ncomp):
            wb_ref[:, c * tn:(c + 1) * tn] = w_refs[c][...].astype(BF16)
        carry_ref[...] = jnp.zeros_like(carry_ref)
        acc_ref[1] = jnp.zeros(acc_ref.shape[1:], F32)

    acc_ref[slot] = jnp.dot(x_ref[...], wb_ref[...], preferred_element_type=F32)

    has_prev = (i + seq_tiles - 1) % seq_tiles != 0
    has_next = i % seq_tiles != 0
    outs = []
    for c in range(ncomp):
        a = acc_ref[ps, :, c * tn:(c + 1) * tn]
        rows = lax.broadcasted_iota(jnp.int32, a.shape, 0)
        prev_row = jnp.where(has_prev, carry_ref[c], 0.0)
        next_row = jnp.where(has_next, acc_ref[slot, 0:1, c * tn:(c + 1) * tn], 0.0)
        up = jnp.where(rows == 0, prev_row, pltpu.roll(a, 1, 0))
        dn = jnp.where(rows == tm - 1, next_row, pltpu.roll(a, tm - 1, 0))
        cw = cw_refs[c]
        outs.append(up * cw[0:1, :] + a * cw[1:2, :] + dn * cw[2:3, :] + b_refs[c][...])
        carry_ref[c] = a[tm - 1:tm, :]
    if glu:
        g, val = outs
        res = g * (1.0 / (1.0 + jnp.exp(-g))) * val
    else:
        res = outs[0]
    o_ref[...] = res.astype(o_ref.dtype)


def matmul_conv(x, w, cw, cb, layer, seq_len, col0, ncols, glu, out_dtype):
    m, k = x.shape
    tm = _tile(seq_len, 1024)
    tn = _tile(ncols, 512 if glu else 1024)
    nm = m // tm
    seq_tiles = seq_len // tm
    assert col0 % tn == 0
    ncomp = 2 if glu else 1
    coffs = [c * (ncols // tn) for c in range(ncomp)]
    woffs = [col0 // tn + o for o in coffs]
    cb3 = cb.reshape(cb.shape[0], 1, cb.shape[1])
    in_specs = [pl.BlockSpec((tm, k), lambda j, i: (jnp.minimum(i, nm - 1), 0))]
    in_specs += [pl.BlockSpec((None, k, tn), lambda j, i, o=o: (layer, 0, o + j)) for o in woffs]
    in_specs += [pl.BlockSpec((None, 3, tn), lambda j, i, o=o: (layer, 0, o + j)) for o in coffs]
    in_specs += [pl.BlockSpec((None, 1, tn), lambda j, i, o=o: (layer, 0, o + j)) for o in coffs]
    args = [x] + [w] * ncomp + [cw] * ncomp + [cb3] * ncomp
    return pl.pallas_call(
        functools.partial(_matmul_conv_kernel, glu=glu, tm=tm, seq_tiles=seq_tiles),
        grid=(ncols // tn, nm + 1),
        in_specs=in_specs,
        out_specs=pl.BlockSpec((tm, tn), lambda j, i: (jnp.maximum(i - 1, 0), j)),
        out_shape=jax.ShapeDtypeStruct((m, ncols), out_dtype),
        scratch_shapes=[pltpu.VMEM((k, ncomp * tn), BF16), pltpu.VMEM((2, tm, ncomp * tn), F32),
                        pltpu.VMEM((ncomp, 1, tn), F32)],
        compiler_params=_params("parallel", "arbitrary"),
        name="matmul_conv_glu" if glu else "matmul_conv",
    )(*args)


def _gmlp_kernel(p_ref, gain_ref, ws_ref, bs_ref, o_ref, *, tm, width):
    p = p_ref[...]
    g = 0.5 * p * (1.0 + lax.erf(p * (1.0 / math.sqrt(2.0))))
    u = g[:, :width]
    v = g[:, width:]
    ms = jnp.mean(v * v, axis=-1, keepdims=True)
    vb = (v * lax.rsqrt(ms + EPS) * gain_ref[...]).astype(BF16)
    gd = width // A_GROUPS
    for c in range(tm // CHUNK):
        r0 = c * CHUNK
        for gi in range(A_GROUPS):
            c0 = gi * gd
            s = jnp.dot(ws_ref[gi], vb[r0:r0 + CHUNK, c0:c0 + gd], preferred_element_type=F32)
            s = s + bs_ref[:, c0:c0 + gd]
            o_ref[r0:r0 + CHUNK, c0:c0 + gd] = (u[r0:r0 + CHUNK, c0:c0 + gd] * s).astype(o_ref.dtype)


def spatial_gating(p, v_gain, w_s, b_s):
    m, two_w = p.shape
    width = two_w // 2
    gd = width // A_GROUPS
    tm = _tile(m, 512)
    bs_full = jnp.repeat(b_s.T, gd, axis=1)
    return pl.pallas_call(
        functools.partial(_gmlp_kernel, tm=tm, width=width),
        grid=(m // tm,),
        in_specs=[pl.BlockSpec((tm, two_w), lambda i: (i, 0)),
                  pl.BlockSpec((1, width), lambda i: (0, 0)),
                  pl.BlockSpec((A_GROUPS, CHUNK, CHUNK), lambda i: (0, 0, 0)),
                  pl.BlockSpec((CHUNK, width), lambda i: (0, 0))],
        out_specs=pl.BlockSpec((tm, width), lambda i: (i, 0)),
        out_shape=jax.ShapeDtypeStruct((m, width), BF16),
        compiler_params=_params("parallel"),
        name="spatial_gating",
    )(p, v_gain.reshape(1, width), w_s.astype(BF16), bs_full)


def _dft_matrices(L):
    n = 2 * L
    r1 = 64
    r0 = L // r1
    t = np.arange(L, dtype=np.int64)[None, :]
    a1 = ((r0 * np.arange(r1, dtype=np.int64))[:, None] * t) % n
    a0 = (np.arange(r0, dtype=np.int64)[:, None] * t) % n
    ca, sa, cb, sb = lax.optimization_barrier((
        jnp.asarray(np.cos(2 * np.pi * a1 / n), F32)[:, None, :],
        jnp.asarray(np.sin(2 * np.pi * a1 / n), F32)[:, None, :],
        jnp.asarray(np.cos(2 * np.pi * a0 / n), F32)[None, :, :],
        jnp.asarray(np.sin(2 * np.pi * a0 / n), F32)[None, :, :]))
    cosm = (ca * cb - sa * sb).astype(BF16).reshape(L, L)
    sinm = (-(sa * cb + ca * sb)).astype(BF16).reshape(L, L)
    return cosm, sinm


def _filter_features(L):
    t = np.linspace(0.0, 1.0, L)[:, None]
    w = 2.0 * np.pi * np.arange(L)[:, None] / L
    bands = np.linspace(1e-4, HYENA_BANDS - 1, HYENA_BANDS)[None, :]
    z = np.concatenate([t, np.cos(w * bands), -np.sin(w * bands)], axis=-1)
    return jnp.asarray(np.pad(z, ((0, 0), (0, 128 - HYENA_EMB))), F32)


def _alternating(shape):
    rows = lax.broadcasted_iota(jnp.int32, shape, 0)
    return jnp.where(rows % 2 == 0, 1.0, -1.0)


def _twiddles(L):
    g = np.arange(L // 4)[:, None] * np.ones((1, LANES))
    a1, a2 = 2.0 * np.pi * g / L, 2.0 * np.pi * g / (2 * L)
    return jnp.asarray(np.stack([np.cos(a1), np.sin(a1), np.cos(a2), np.sin(a2)]), F32)


def _cadd(a, b):
    return a[0] + b[0], a[1] + b[1]


def _csub(a, b):
    return a[0] - b[0], a[1] - b[1]


def _conj(a):
    return a[0], -a[1]


def _cmul(a, b):
    return a[0] * b[0] - a[1] * b[1], a[0] * b[1] + a[1] * b[0]


def _rot(a, c, s):
    return a[0] * c + a[1] * s, a[1] * c - a[0] * s


def _unrot(a, c, s):
    return a[0] * c - a[1] * s, a[1] * c + a[0] * s


def _times_i(a):
    return -a[1], a[0]


def _times_minus_i(a):
    return a[1], -a[0]


def _lane_block(x, j):
    return x[:, j * LANES:(j + 1) * LANES]


def _period8_patterns():
    t = np.arange(8)[:, None] * np.ones((1, LANES))
    rows = [np.cos(np.pi * t)]
    for k in (2, 1, 3):
        rows += [np.cos(k * np.pi * t / 4), np.sin(k * np.pi * t / 4)]
    return jnp.asarray(np.round(np.stack(rows), 15), F32)


def _filter_spectrum_kernel(z_ref, w1_ref, b1_ref, w2_ref, b2_ref, fr_ref, w3f_ref, w3b_ref, dl_ref,
                            pat_ref, cos_ref, sin_ref, o_ref, rc_ref, rs_ref, nyq_ref, h_ref, taps_ref,
                            *, n):
    m = pl.program_id(2)
    hp = lax.Precision.HIGHEST

    @pl.when(jnp.logical_and(jnp.logical_and(pl.program_id(0) == 0, pl.program_id(1) == 0), m == 0))
    def _():
        h = jnp.dot(z_ref[...], w1_ref[...], precision=hp, preferred_element_type=F32) + b1_ref[...]
        h = jnp.sin(fr_ref[0:1, :] * h)
        h = jnp.dot(h, w2_ref[...], precision=hp, preferred_element_type=F32) + b2_ref[...]
        h_ref[...] = jnp.sin(fr_ref[1:2, :] * h)

    @pl.when(m == 0)
    def _():
        length = z_ref.shape[0]
        chunk = min(length, 2 * DFT_TILE)
        nrm = None
        tc = taps_ref.shape[2]
        w3 = jnp.concatenate([w3f_ref[...], w3b_ref[...]], axis=1)
        for r0 in range(0, length, chunk):
            rs = slice(r0, r0 + chunk)
            decay = jnp.exp(-z_ref[rs, 0:1] * dl_ref[...])
            hfb = jnp.dot(h_ref[rs, :], w3, precision=hp, preferred_element_type=F32)
            hf, hb = hfb[:, :tc] * decay, hfb[:, tc:] * decay
            if r0 == 0:
                hb = jnp.where(lax.broadcasted_iota(jnp.int32, hb.shape, 0) == 0, 0.0, hb)
            part = jnp.sum(jnp.abs(hf) + jnp.abs(hb), axis=0, keepdims=True)
            nrm = part if nrm is None else nrm + part
            taps_ref[0, rs, :] = hf + hb
            taps_ref[1, rs, :] = hf - hb
        inv = 1.0 / nrm
        packed = [None] * 4
        alt, hc, hs, qc, qs, q3c, q3s = (jnp.concatenate([pat_ref[i]] * (chunk // 8), axis=0)
                                         for i in range(7))
        for r0 in range(0, length, chunk):
            rs = slice(r0, r0 + chunk)
            ke = taps_ref[0, rs, :] * inv
            ko = taps_ref[1, rs, :] * inv
            for blk, (vc, vs) in enumerate(((ke, ko), (ke * alt, -ko * alt), (ke * hc, ke * hs),
                                            (ko * hs, ko * hc))):
                rc_ref[rs, blk * tc:(blk + 1) * tc] = vc.astype(BF16)
                rs_ref[rs, blk * tc:(blk + 1) * tc] = vs.astype(BF16)
            parts = (jnp.sum(ke * qc, axis=0, keepdims=True), -jnp.sum(ko * qs, axis=0, keepdims=True),
                     jnp.sum(ke * q3c, axis=0, keepdims=True), -jnp.sum(ko * q3s, axis=0, keepdims=True))
            packed = [pt if acc is None else acc + pt for acc, pt in zip(packed, parts)]
        for i in range(4):
            nyq_ref[i:i + 1, :] = packed[i] * (2.0 / n)

    tc = o_ref.shape[2]
    rc = jnp.dot(cos_ref[...], rc_ref[...], preferred_element_type=F32)
    rs = jnp.dot(sin_ref[...], rs_ref[...], preferred_element_type=F32)
    c = [rc[:, i * tc:(i + 1) * tc] for i in range(4)]
    s = [rs[:, i * tc:(i + 1) * tc] for i in range(4)]
    planes = [c[0], s[0], c[1], s[1], c[2] - s[2], -c[3] - s[3], c[2] + s[2], s[3] - c[3]]
    first = jnp.logical_and(lax.broadcasted_iota(jnp.int32, c[0].shape, 0) == 0, m == 0)
    scale = jnp.where(first, 1.0 / n, 2.0 / n)
    slots = {1: 2, 3: 3, 6: 0, 7: 1}
    for i in range(8):
        val = planes[i] * scale
        if i in slots:
            val = jnp.where(first, nyq_ref[slots[i]:slots[i] + 1, :], val)
        o_ref[i] = val


def hyena_filter_spectrum(L, width, cosm, sinm, w1, b1, w2, b2, w3, freq):
    z = _filter_features(L)
    tc = LANES
    ncb = width // tc
    max_decay = math.log(HYENA_TARGET) / HYENA_FAST
    min_decay = math.log(HYENA_TARGET) / HYENA_SLOW
    deltas = jnp.asarray(np.abs(np.linspace(min_decay, max_decay, width)), F32).reshape(1, width)
    w1p = jnp.pad(w1, ((0, 128 - HYENA_EMB), (0, 0)))
    const = lambda o, c, m: (0, 0)
    tm = _tile(L // 4, DFT_TILE)
    return pl.pallas_call(
        functools.partial(_filter_spectrum_kernel, n=2 * L),
        grid=(HYENA_ORDER, ncb, L // 4 // tm),
        in_specs=[pl.BlockSpec((L, 128), const),
                  pl.BlockSpec((128, HYENA_FFN), const), pl.BlockSpec((1, HYENA_FFN), const),
                  pl.BlockSpec((HYENA_FFN, HYENA_FFN), const), pl.BlockSpec((1, HYENA_FFN), const),
                  pl.BlockSpec((2, HYENA_FFN), const),
                  pl.BlockSpec((HYENA_FFN, tc), lambda o, c, m: (0, (2 * o) * ncb + c)),
                  pl.BlockSpec((HYENA_FFN, tc), lambda o, c, m: (0, (2 * o + 1) * ncb + c)),
                  pl.BlockSpec((1, tc), lambda o, c, m: (0, c)),
                  pl.BlockSpec((7, 8, LANES), lambda o, c, m: (0, 0, 0)),
                  pl.BlockSpec((tm, L), lambda o, c, m: (m, 0)),
                  pl.BlockSpec((tm, L), lambda o, c, m: (m, 0))],
        out_specs=pl.BlockSpec((None, 8, tm, tc), lambda o, c, m: (o, 0, m, c)),
        out_shape=jax.ShapeDtypeStruct((HYENA_ORDER, 8, L // 4, width), F32),
        scratch_shapes=[pltpu.VMEM((L, 4 * tc), BF16), pltpu.VMEM((L, 4 * tc), BF16),
                        pltpu.VMEM((4, tc), F32), pltpu.VMEM((L, HYENA_FFN), F32),
                        pltpu.VMEM((2, L, tc), F32)],
        compiler_params=_params("arbitrary", "arbitrary", "arbitrary"),
        name="hyena_filter_spectrum",
    )(z, w1p, b1.reshape(1, -1), w2, b2.reshape(1, -1), freq, w3, w3, deltas, _period8_patterns(),
      cosm, sinm)


def _dft_forward_kernel(u_ref, cos_ref, sin_ref, tw_ref, kf_ref, o_ref, ub_ref):
    q = ub_ref.shape[0]
    nyq = []
    for j in range(4):
        xj = u_ref[pl.ds(j, q, stride=4), :]
        ub_ref[:, j * LANES:(j + 1) * LANES] = xj.astype(BF16)
        nyq.append(jnp.sum(xj * _alternating(xj.shape), axis=0, keepdims=True))
    eq, oq = (nyq[0], -nyq[2]), (nyq[1], -nyq[3])
    woq = _rot(oq, math.sqrt(0.5), math.sqrt(0.5))
    u_q, u_3q = _cadd(eq, woq), _conj(_csub(eq, woq))
    chunk = min(q, DFT_TILE)
    for r0 in range(0, q, chunk):
        rs = slice(r0, r0 + chunk)
        re = jnp.dot(cos_ref[rs, :], ub_ref[...], preferred_element_type=F32)
        im = jnp.dot(sin_ref[rs, :], ub_ref[...], preferred_element_type=F32)
        x = [(_lane_block(re, j), _lane_block(im, j)) for j in range(4)]
        c1, s1, c2, s2 = tw_ref[0, rs, :], tw_ref[1, rs, :], tw_ref[2, rs, :], tw_ref[3, rs, :]
        t2, t3 = _rot(x[2], c1, s1), _rot(x[3], c1, s1)
        ea, eb = _cadd(x[0], t2), _csub(x[0], t2)
        oa, ob = _cadd(x[1], t3), _csub(x[1], t3)
        ta, tb = _rot(oa, c2, s2), _times_i(_rot(ob, c2, s2))
        p = [_cadd(ea, ta), _conj(_csub(ea, ta)), _conj(_cadd(eb, tb)), _csub(eb, tb)]
        k = [(kf_ref[2 * i, rs, :], kf_ref[2 * i + 1, rs, :]) for i in range(4)]
        if r0 == 0:
            first = lax.broadcasted_iota(jnp.int32, re.shape[:1] + (LANES,), 0) == 0
            p[3] = (jnp.where(first, u_q[0], p[3][0]), jnp.where(first, u_q[1], p[3][1]))
        y = [_cmul(p[i], k[i]) for i in range(4)]
        if r0 == 0:
            y3q = _cmul(u_3q, (k[0][1], k[1][1]))
            y[0] = (y[0][0], jnp.where(first, y3q[0], y[0][1]))
            y[1] = (y[1][0], jnp.where(first, y3q[1], y[1][1]))
        for i in range(4):
            o_ref[2 * i, rs, :] = y[i][0].astype(o_ref.dtype)
            o_ref[2 * i + 1, rs, :] = y[i][1].astype(o_ref.dtype)


def _resident(shape):
    return pl.BlockSpec(shape, lambda c, b: (0,) * len(shape), pipeline_mode=pl.Buffered(1))


def dft_forward_times_filter(u, col0, width, cos4, sin4, tw, kf, order):
    bsz, L, _ = u.shape
    q = L // 4
    assert col0 % LANES == 0
    cb0 = col0 // LANES
    return pl.pallas_call(
        _dft_forward_kernel,
        grid=(width // LANES, bsz),
        in_specs=[pl.BlockSpec((None, L, LANES), lambda c, b: (b, 0, cb0 + c)),
                  _resident((q, q)), _resident((q, q)), _resident((4, q, LANES)),
                  pl.BlockSpec((None, 8, q, LANES), lambda c, b: (order, 0, 0, c))],
        out_specs=pl.BlockSpec((None, 8, q, LANES), lambda c, b: (b, 0, 0, c)),
        out_shape=jax.ShapeDtypeStruct((bsz, 8, q, width), BF16),
        scratch_shapes=[pltpu.VMEM((q, 4 * LANES), BF16)],
        compiler_params=_params("parallel", "parallel"),
        name="dft_forward",
    )(u, cos4, sin4, tw, kf)


def _dft_inverse_kernel(y_ref, cos_ref, sin_ref, tw_ref, u_ref, g_ref, skip_ref, o_ref, x_ref, res_ref):
    q = x_ref.shape[1]

    def row0(i):
        return y_ref[i, 0:1, :].astype(F32)

    u_q, u_3q = (row0(6), row0(7)), (row0(1), row0(3))
    eq = _cadd(u_q, _conj(u_3q))
    oq = _unrot(_csub(u_q, _conj(u_3q)), math.sqrt(0.5), math.sqrt(0.5))
    nyq = jnp.concatenate([eq[0], oq[0], -eq[1], -oq[1]], axis=1)
    chunk = min(q, DFT_TILE)
    for r0 in range(0, q, chunk):
        rs = slice(r0, r0 + chunk)
        p = [(y_ref[2 * i, rs, :].astype(F32), y_ref[2 * i + 1, rs, :].astype(F32)) for i in range(4)]
        if r0 == 0:
            first = lax.broadcasted_iota(jnp.int32, p[0][0].shape, 0) == 0
            p[0] = (p[0][0], jnp.where(first, 0.0, p[0][1]))
            p[1] = (p[1][0], jnp.where(first, 0.0, p[1][1]))
            p[3] = (jnp.where(first, p[2][0], p[3][0]), jnp.where(first, p[2][1], p[3][1]))
        c1, s1, c2, s2 = tw_ref[0, rs, :], tw_ref[1, rs, :], tw_ref[2, rs, :], tw_ref[3, rs, :]
        ea, ta = _cadd(p[0], _conj(p[1])), _csub(p[0], _conj(p[1]))
        eb, tb = _cadd(_conj(p[2]), p[3]), _csub(_conj(p[2]), p[3])
        oa = _unrot(ta, c2, s2)
        ob = _times_minus_i(_unrot(tb, c2, s2))
        x = [_cadd(ea, eb), _cadd(oa, ob), _unrot(_csub(ea, eb), c1, s1), _unrot(_csub(oa, ob), c1, s1)]
        for j in range(4):
            x_ref[0, rs, j * LANES:(j + 1) * LANES] = x[j][0].astype(BF16)
            x_ref[1, rs, j * LANES:(j + 1) * LANES] = x[j][1].astype(BF16)
    skip = skip_ref[...]
    alt = _alternating((chunk, 4 * LANES))
    for r0 in range(0, q, chunk):
        rs = slice(r0, r0 + chunk)
        conv = jnp.dot(cos_ref[rs, :], x_ref[0], preferred_element_type=F32)
        conv = conv + jnp.dot(sin_ref[rs, :], x_ref[1], preferred_element_type=F32)
        conv = conv + alt * nyq
        for j in range(4):
            rows = pl.ds(4 * r0 + j, chunk, stride=4)
            res_ref[rows, :] = g_ref[rows, :] * (_lane_block(conv, j) + u_ref[rows, :] * skip)
    o_ref[...] = res_ref[...].astype(o_ref.dtype)


def dft_inverse_gated(y, cos4, sin4, tw, u, ucol0, gate, gcol0, skip, out_dtype):
    bsz, _, q, width = y.shape
    L = 4 * q
    ub0, gb0 = ucol0 // LANES, gcol0 // LANES
    return pl.pallas_call(
        _dft_inverse_kernel,
        grid=(width // LANES, bsz),
        in_specs=[pl.BlockSpec((None, 8, q, LANES), lambda c, b: (b, 0, 0, c)),
                  _resident((q, q)), _resident((q, q)), _resident((4, q, LANES)),
                  pl.BlockSpec((None, L, LANES), lambda c, b: (b, 0, ub0 + c)),
                  pl.BlockSpec((None, L, LANES), lambda c, b: (b, 0, gb0 + c)),
                  pl.BlockSpec((1, LANES), lambda c, b: (0, c))],
        out_specs=pl.BlockSpec((None, L, LANES), lambda c, b: (b, 0, c)),
        out_shape=jax.ShapeDtypeStruct((bsz, L, width), out_dtype),
        scratch_shapes=[pltpu.VMEM((2, q, 4 * LANES), BF16), pltpu.VMEM((L, LANES), F32)],
        compiler_params=_params("parallel", "parallel"),
        name="dft_inverse",
    )(y, cos4, sin4, tw, u, gate, skip.reshape(1, width))


NA_GROUP = 8
NA_BLOCK = NA_GROUP + NA_KH
NA_PAD = NA_KH // 2


def _na_plans():
    tiles = {}
    plans = []
    for variant in range(3):
        plan = []
        for rq in range(NA_GROUP):
            lo = (max(rq, NA_PAD), rq, min(rq, NA_PAD))[variant]
            a_lo, a_hi = lo // 2, (lo + NA_KH - 1) // 2 + 1
            ids = []
            for a in range(a_lo, a_hi):
                key = tuple(kr - rq + NA_KH - 1 - NA_PAD if lo <= kr < lo + NA_KH else -1
                            for kr in (2 * a, 2 * a + 1))
                ids.append(tiles.setdefault(key, len(tiles)))
            plan.append((a_lo, tuple(ids)))
        plans.append(tuple(plan))
    return tuple(plans), list(tiles)


def _na_kernel(q_ref, k_ref, v_ref, tiles_ref, o_ref, kb_ref, vb_ref, p_ref, *, rows, scale, plans):
    L = rows * GRID_W
    pad = NA_PAD * GRID_W
    gq = NA_GROUP * GRID_W
    gk = NA_BLOCK * GRID_W
    for ref, src in ((kb_ref, k_ref), (vb_ref, v_ref)):
        ref[0:pad, :] = jnp.zeros((pad, HEAD_DIM), BF16)
        ref[pad + L:, :] = jnp.zeros((pad, HEAD_DIM), BF16)
        ref[pad:pad + L, :] = src[...].astype(BF16)

    def group(q0, plan):
        q = q_ref[pl.ds(q0, gq), :].astype(BF16)
        s = lax.dot_general(q, kb_ref[pl.ds(q0, gk), :], (((1,), (1,)), ((), ())),
                            preferred_element_type=F32)
        p_ref[...] = jnp.zeros(p_ref.shape, BF16)
        for rq, (a_lo, ids) in enumerate(plan):
            r0, c0, c1 = rq * GRID_W, 2 * GRID_W * a_lo, 2 * GRID_W * (a_lo + len(ids))
            bias = jnp.concatenate([tiles_ref[t] for t in ids], axis=1)
            sl = s[r0:r0 + GRID_W, c0:c1] * scale + bias
            e = jnp.exp(sl - jnp.max(sl, axis=-1, keepdims=True))
            pr = e * (1.0 / jnp.sum(e, axis=-1, keepdims=True))
            p_ref[r0:r0 + GRID_W, c0:c1] = pr.astype(BF16)
        o = jnp.dot(p_ref[...], vb_ref[pl.ds(q0, gk), :], preferred_element_type=F32)
        o_ref[pl.ds(q0, gq), :] = o.astype(o_ref.dtype)

    ngroups = rows // NA_GROUP
    group(0, plans[0])

    def middle(g, carry):
        group(pl.multiple_of(g * gq, gq), plans[1])
        return carry

    lax.fori_loop(1, ngroups - 1, middle, 0)
    group((ngroups - 1) * gq, plans[2])


def _na_bias_tiles(rpb, tile_keys):
    col = np.arange(GRID_W)
    cs = np.clip(col - NA_KW // 2, 0, GRID_W - NA_KW)
    kc = np.arange(GRID_W)
    inwin = (kc[None, :] >= cs[:, None]) & (kc[None, :] < cs[:, None] + NA_KW)
    rel_col = np.clip(kc[None, :] - col[:, None] + (NA_KW - 1), 0, 2 * NA_KW - 2)
    t = jnp.where(jnp.asarray(inwin)[None, None], rpb[:, :, rel_col], MASK_VALUE)
    masked = jnp.full((rpb.shape[0], 1, GRID_W, GRID_W), MASK_VALUE, F32)
    t = jnp.concatenate([t.astype(F32), masked], axis=1)
    left = np.array([k[0] for k in tile_keys])
    right = np.array([k[1] for k in tile_keys])
    return jnp.concatenate([t[:, left], t[:, right]], axis=-1)


def neighborhood_attention(p, bsz, L, heads, rpb):
    rows = L // GRID_W
    assert rows % NA_GROUP == 0 and rows >= 2 * NA_GROUP
    plans, tile_keys = _na_plans()
    tiles = _na_bias_tiles(rpb, tile_keys)
    nt = len(tile_keys)
    lp = L + 2 * NA_PAD * GRID_W
    return pl.pallas_call(
        functools.partial(_na_kernel, rows=rows, scale=HEAD_DIM ** -0.5, plans=plans),
        grid=(bsz, heads),
        in_specs=[pl.BlockSpec((None, L, HEAD_DIM), lambda b, h: (b, 0, h)),
                  pl.BlockSpec((None, L, HEAD_DIM), lambda b, h: (b, 0, heads + h)),
                  pl.BlockSpec((None, L, HEAD_DIM), lambda b, h: (b, 0, 2 * heads + h)),
                  pl.BlockSpec((None, nt, GRID_W, 2 * GRID_W), lambda b, h: (h, 0, 0, 0))],
        out_specs=pl.BlockSpec((None, L, HEAD_DIM), lambda b, h: (b, 0, h)),
        out_shape=jax.ShapeDtypeStruct((bsz, L, heads * HEAD_DIM), BF16),
        scratch_shapes=[pltpu.VMEM((lp, HEAD_DIM), BF16), pltpu.VMEM((lp, HEAD_DIM), BF16),
                        pltpu.VMEM((NA_GROUP * GRID_W, NA_BLOCK * GRID_W), BF16)],
        compiler_params=_params("parallel", "parallel"),
        name="neighborhood_attention",
    )(p, p, p, tiles)


DIFF_SUB_ROWS = 256


def _rope(x, cos, sin_signed, half):
    lane = lax.broadcasted_iota(jnp.int32, x.shape, 1)
    width = x.shape[1]
    first = (lane % (2 * half)) < half
    partner = jnp.where(first, pltpu.roll(x, width - half, 1), pltpu.roll(x, half, 1))
    return x * cos + partner * sin_signed


def _diff_attn_kernel(q_ref, k_ref, v_ref, cq_ref, sq_ref, ck_ref, sk_ref, lam_ref, sub_ref, o_ref,
                      kb_ref, vb_ref, *, lam_init, scale, half, n_sub):
    qi = pl.program_id(2)

    @pl.when(qi == 0)
    def _():
        kb_ref[...] = _rope(k_ref[...], ck_ref[...], sk_ref[...], half).astype(BF16)
        vb_ref[...] = v_ref[...].astype(BF16)

    lp = lam_ref[...]
    lam = (jnp.exp(jnp.sum(lp[0:1] * lp[1:2], axis=-1, keepdims=True))
           - jnp.exp(jnp.sum(lp[2:3] * lp[3:4], axis=-1, keepdims=True)) + lam_init)
    kb = kb_ref[...]
    tq = q_ref.shape[0]
    sub = tq // n_sub
    for sb in range(n_sub):
        rs = slice(sb * sub, (sb + 1) * sub)
        q = _rope(q_ref[rs, :], cq_ref[rs, :], sq_ref[rs, :], half) * (scale * math.log2(math.e))
        lane = lax.broadcasted_iota(jnp.int32, q.shape, 1)

        def softmax_map(in_map, weight):
            qm = jnp.where(in_map, q, 0.0).astype(BF16)
            s = lax.dot_general(qm, kb, (((1,), (1,)), ((), ())), preferred_element_type=F32)
            e = jnp.exp2(s - jnp.max(s, axis=-1, keepdims=True))
            return e * (weight / jnp.sum(e, axis=-1, keepdims=True))

        a = softmax_map(lane < 2 * half, 1.0) - softmax_map(lane >= 2 * half, lam)
        o = jnp.dot(a.astype(BF16), vb_ref[...], preferred_element_type=F32)
        ms = jnp.mean(o * o, axis=-1, keepdims=True)
        o_ref[rs, :] = (o * lax.rsqrt(ms + EPS) * sub_ref[...] * (1.0 - lam_init)).astype(o_ref.dtype)


def _rope_tables(L, half):
    inv = (1.0 / (np.float32(ROPE_THETA) ** (np.arange(half, dtype=np.float32) * np.float32(2.0)
                                             / np.float32(2 * half)))).astype(np.float32)
    ang = (np.arange(L, dtype=np.float32)[:, None] * inv[None, :]).astype(np.float64)
    cos, sin = np.cos(ang), np.sin(ang)
    reps = HEAD_DIM // (2 * half)
    cos_t = np.tile(cos, (1, 2 * reps))
    sin_t = np.tile(np.concatenate([-sin, sin], axis=1), (1, reps))
    return jnp.asarray(cos_t, F32), jnp.asarray(sin_t, F32)


def diff_attention(p, col0, bsz, L, heads, lam_params, subln, lam_init):
    half = HEAD_DIM // 4
    assert col0 % HEAD_DIM == 0
    c0 = col0 // HEAD_DIM
    tq = _tile(L, 2 * DIFF_SUB_ROWS)
    n_sub = tq // DIFF_SUB_ROWS
    cos_t, sin_t = _rope_tables(L, half)
    return pl.pallas_call(
        functools.partial(_diff_attn_kernel, lam_init=lam_init, scale=(2 * half) ** -0.5, half=half,
                          n_sub=n_sub),
        grid=(bsz, heads, L // tq),
        in_specs=[pl.BlockSpec((None, tq, HEAD_DIM), lambda b, h, i: (b, i, c0 + h)),
                  pl.BlockSpec((None, L, HEAD_DIM), lambda b, h, i: (b, 0, c0 + heads + h)),
                  pl.BlockSpec((None, L, HEAD_DIM), lambda b, h, i: (b, 0, c0 + 2 * heads + h)),
                  pl.BlockSpec((tq, HEAD_DIM), lambda b, h, i: (i, 0)),
                  pl.BlockSpec((tq, HEAD_DIM), lambda b, h, i: (i, 0)),
                  pl.BlockSpec((L, HEAD_DIM), lambda b, h, i: (0, 0)),
                  pl.BlockSpec((L, HEAD_DIM), lambda b, h, i: (0, 0)),
                  pl.BlockSpec((4, 2 * half), lambda b, h, i: (0, 0)),
                  pl.BlockSpec((1, HEAD_DIM), lambda b, h, i: (0, 0))],
        out_specs=pl.BlockSpec((None, tq, HEAD_DIM), lambda b, h, i: (b, i, h)),
        out_shape=jax.ShapeDtypeStruct((bsz, L, heads * HEAD_DIM), BF16),
        scratch_shapes=[pltpu.VMEM((L, HEAD_DIM), BF16), pltpu.VMEM((L, HEAD_DIM), BF16)],
        compiler_params=_params("parallel", "parallel", "arbitrary"),
        name="diff_attention",
    )(p, p, p, cos_t, sin_t, cos_t, sin_t, lam_params, subln.reshape(1, HEAD_DIM))


def kernel(x, norm_mix, norm_ffn, w_out, ffn_up, ffn_conv_w, ffn_conv_b, ffn_down, final_norm,
           ab_w_in, a_vnorm, a_ws, a_bs, b_conv_w, b_conv_b, b_filt_w1, b_filt_b1, b_filt_w2,
           b_filt_b2, b_filt_w3, b_filt_freq, b_skip, cd_w_in, c_rpb, d_lambda, d_subln):
    bsz, L, d = x.shape
    depth = norm_mix.shape[0]
    m = bsz * L
    half_w = d // 2
    ff = ffn_down.shape[1]
    heads = half_w // HEAD_DIM
    assert heads == D_HEADS and half_w // A_GROUPS == CHUNK and L % GRID_W == 0
    xs = x.reshape(m, d)
    cosm, sinm = _dft_matrices(L)
    cos4, sin4 = _dft_matrices(L // 4)
    tw = _twiddles(L)
    w_out_b = w_out.astype(BF16)
    ffn_down_b = ffn_down.astype(BF16)
    h = rmsnorm(xs, norm_mix[0], BF16)
    for l in range(depth):
        i = l // 2
        if l % 2 == 0:
            pa = matmul(h, ab_w_in, i, 0, 2 * half_w, F32)
            ya = spatial_gating(pa, a_vnorm[i], a_ws[i], a_bs[i])
            pb = matmul_conv(h, ab_w_in, b_conv_w, b_conv_b, i, L, 2 * half_w, 3 * half_w, False, F32)
            pb = pb.reshape(bsz, L, 3 * half_w)
            kf = hyena_filter_spectrum(L, half_w, cosm, sinm, b_filt_w1[i], b_filt_b1[i], b_filt_w2[i],
                                       b_filt_b2[i], b_filt_w3[i], b_filt_freq[i])
            y1 = dft_forward_times_filter(pb, 0, half_w, cos4, sin4, tw, kf, 0)
            z = dft_inverse_gated(y1, cos4, sin4, tw, pb, 0, pb, half_w, b_skip[i, 0], F32)
            y2 = dft_forward_times_filter(z, 0, half_w, cos4, sin4, tw, kf, 1)
            yb = dft_inverse_gated(y2, cos4, sin4, tw, z, 0, pb, 2 * half_w, b_skip[i, 1], BF16)
            yb = yb.reshape(m, half_w)
        else:
            p = matmul(h, cd_w_in, i, 0, 6 * half_w, F32).reshape(bsz, L, 6 * half_w)
            lam_init = 0.8 - 0.6 * math.exp(-0.3 * l)
            ya = neighborhood_attention(p, bsz, L, heads, c_rpb[i]).reshape(m, half_w)
            yb = diff_attention(p, 3 * half_w, bsz, L, heads, d_lambda[i], d_subln[i], lam_init)
            yb = yb.reshape(m, half_w)
        xs, hf = outproj_norm(ya, yb, w_out_b, l, xs, norm_ffn[l])
        act = matmul_conv(hf, ffn_up, ffn_conv_w, ffn_conv_b, l, L, 0, ff, True, BF16)
        last = l == depth - 1
        xs, h = downproj_norm(act, ffn_down_b, l, xs, final_norm if last else norm_mix[l + 1],
                              F32 if last else BF16)
    return h.reshape(bsz, L, d)
```

```python
import functools
import math

import numpy as np
import jax
import jax.numpy as jnp
from jax import lax
from jax.experimental import pallas as pl
from jax.experimental.pallas import tpu as pltpu

F32 = jnp.float32
BF16 = jnp.bfloat16

EPS = 1e-6
GRID_W = 64
HEAD_DIM = 128
CHUNK = 128
A_GROUPS = 8
NA_KH = 8
NA_KW = 16
D_HEADS = 8
ROPE_THETA = 10000.0
HYENA_ORDER = 2
HYENA_EMB = 33
HYENA_BANDS = (HYENA_EMB - 1) // 2
HYENA_FFN = 64
HYENA_TARGET = 1e-2
HYENA_FAST = 0.3
HYENA_SLOW = 1.5
MASK_VALUE = -1e30

VMEM_LIMIT_BYTES = 56 * 1024 * 1024
DFT_TILE = 512
LANES = 128


def _params(*sem):
    return pltpu.CompilerParams(dimension_semantics=sem, vmem_limit_bytes=VMEM_LIMIT_BYTES)


def _tile(n, pref):
    t = min(n, pref)
    assert n % t == 0, (n, pref)
    return t


def _rmsnorm_kernel(x_ref, g_ref, o_ref):
    x = x_ref[...]
    ms = jnp.mean(x * x, axis=-1, keepdims=True)
    o_ref[...] = (x * lax.rsqrt(ms + EPS) * g_ref[...]).astype(o_ref.dtype)


def rmsnorm(x, g, out_dtype):
    m, d = x.shape
    tm = _tile(m, 512)
    return pl.pallas_call(
        _rmsnorm_kernel,
        grid=(m // tm,),
        in_specs=[pl.BlockSpec((tm, d), lambda i: (i, 0)), pl.BlockSpec((1, d), lambda i: (0, 0))],
        out_specs=pl.BlockSpec((tm, d), lambda i: (i, 0)),
        out_shape=jax.ShapeDtypeStruct((m, d), out_dtype),
        compiler_params=_params("parallel"),
        name="rmsnorm",
    )(x, g.reshape(1, d))


def _matmul_kernel(x_ref, w_ref, o_ref, wb_ref):
    @pl.when(pl.program_id(1) == 0)
    def _():
        wb_ref[...] = w_ref[...].astype(BF16)

    o_ref[...] = jnp.dot(x_ref[...], wb_ref[...], preferred_element_type=F32).astype(o_ref.dtype)


def matmul(x, w, layer, col0, ncols, out_dtype):
    m, k = x.shape
    tm = _tile(m, 1024)
    tn = _tile(ncols, 1024)
    assert col0 % tn == 0
    jb = col0 // tn
    return pl.pallas_call(
        _matmul_kernel,
        grid=(ncols // tn, m // tm),
        in_specs=[pl.BlockSpec((tm, k), lambda j, i: (i, 0)),
                  pl.BlockSpec((None, k, tn), lambda j, i: (layer, 0, jb + j))],
        out_specs=pl.BlockSpec((tm, tn), lambda j, i: (i, j)),
        out_shape=jax.ShapeDtypeStruct((m, ncols), out_dtype),
        scratch_shapes=[pltpu.VMEM((k, tn), BF16)],
        compiler_params=_params("parallel", "arbitrary"),
        name="matmul",
    )(x, w)


def _outproj_norm_kernel(xa_ref, wa_ref, xb_ref, wb_ref, res_ref, g_ref, o_ref, h_ref):
    acc = res_ref[...] + jnp.dot(xa_ref[...], wa_ref[...], preferred_element_type=F32)
    acc = acc + jnp.dot(xb_ref[...], wb_ref[...], preferred_element_type=F32)
    o_ref[...] = acc
    ms = jnp.mean(acc * acc, axis=-1, keepdims=True)
    h_ref[...] = (acc * lax.rsqrt(ms + EPS) * g_ref[...]).astype(h_ref.dtype)


def outproj_norm(xa, xb, w, layer, res, gain):
    m, n = res.shape
    k = xa.shape[1]
    tm = _tile(m, 512)
    return pl.pallas_call(
        _outproj_norm_kernel,
        grid=(m // tm,),
        in_specs=[pl.BlockSpec((tm, k), lambda i: (i, 0)),
                  pl.BlockSpec((None, k, n), lambda i: (layer, 0, 0)),
                  pl.BlockSpec((tm, k), lambda i: (i, 0)),
                  pl.BlockSpec((None, k, n), lambda i: (layer, 1, 0)),
                  pl.BlockSpec((tm, n), lambda i: (i, 0)),
                  pl.BlockSpec((1, n), lambda i: (0, 0))],
        out_specs=[pl.BlockSpec((tm, n), lambda i: (i, 0)), pl.BlockSpec((tm, n), lambda i: (i, 0))],
        out_shape=[jax.ShapeDtypeStruct((m, n), F32), jax.ShapeDtypeStruct((m, n), BF16)],
        compiler_params=_params("parallel"),
        name="outproj_norm",
    )(xa, w, xb, w, res, gain.reshape(1, n))


def _downproj_norm_kernel(x_ref, w_ref, res_ref, g_ref, o_ref, h_ref):
    acc = res_ref[...] + jnp.dot(x_ref[...], w_ref[...], preferred_element_type=F32)
    o_ref[...] = acc
    ms = jnp.mean(acc * acc, axis=-1, keepdims=True)
    h_ref[...] = (acc * lax.rsqrt(ms + EPS) * g_ref[...]).astype(h_ref.dtype)


def downproj_norm(x, w, layer, res, gain, norm_dtype):
    m, n = res.shape
    k = x.shape[1]
    tm = _tile(m, 256)
    return pl.pallas_call(
        _downproj_norm_kernel,
        grid=(m // tm,),
        in_specs=[pl.BlockSpec((tm, k), lambda i: (i, 0)),
                  pl.BlockSpec((None, k, n), lambda i: (layer, 0, 0), pipeline_mode=pl.Buffered(1)),
                  pl.BlockSpec((tm, n), lambda i: (i, 0)),
                  pl.BlockSpec((1, n), lambda i: (0, 0))],
        out_specs=[pl.BlockSpec((tm, n), lambda i: (i, 0)), pl.BlockSpec((tm, n), lambda i: (i, 0))],
        out_shape=[jax.ShapeDtypeStruct((m, n), F32), jax.ShapeDtypeStruct((m, n), norm_dtype)],
        compiler_params=_params("parallel"),
        name="downproj_norm",
    )(x, w, res, gain.reshape(1, n))


def _matmul_conv_kernel(*refs, glu, tm, seq_tiles):
    ncomp = 2 if glu else 1
    x_ref = refs[0]
    w_refs = refs[1:1 + ncomp]
    cw_refs = refs[1 + ncomp:1 + 2 * ncomp]
    b_refs = refs[1 + 2 * ncomp:1 + 3 * ncomp]
    o_ref = refs[1 + 3 * ncomp]
    wb_ref, acc_ref, carry_ref = refs[2 + 3 * ncomp:5 + 3 * ncomp]
    i = pl.program_id(1)
    slot = i % 2
    ps = 1 - slot

    tn = o_ref.shape[1]

    @pl.when(i == 0)
    def _():
        for c in range(ncomp):
            wb_ref[:, c * tn:(c + 1) * tn] = w_refs[c][...].astype(BF16)
        carry_ref[...] = jnp.zeros_like(carry_ref)
        acc_ref[1] = jnp.zeros(acc_ref.shape[1:], F32)

    acc_ref[slot] = jnp.dot(x_ref[...], wb_ref[...], preferred_element_type=F32)

    has_prev = (i + seq_tiles - 1) % seq_tiles != 0
    has_next = i % seq_tiles != 0
    outs = []
    for c in range(ncomp):
        a = acc_ref[ps, :, c * tn:(c + 1) * tn]
        rows = lax.broadcasted_iota(jnp.int32, a.shape, 0)
        prev_row = jnp.where(has_prev, carry_ref[c], 0.0)
        next_row = jnp.where(has_next, acc_ref[slot, 0:1, c * tn:(c + 1) * tn], 0.0)
        up = jnp.where(rows == 0, prev_row, pltpu.roll(a, 1, 0))
        dn = jnp.where(rows == tm - 1, next_row, pltpu.roll(a, tm - 1, 0))
        cw = cw_refs[c]
        outs.append(up * cw[0:1, :] + a * cw[1:2, :] + dn * cw[2:3, :] + b_refs[c][...])
        carry_ref[c] = a[tm - 1:tm, :]
    if glu:
        g, val = outs
        res = g * (1.0 / (1.0 + jnp.exp(-g))) * val
    else:
        res = outs[0]
    o_ref[...] = res.astype(o_ref.dtype)


def matmul_conv(x, w, cw, cb, layer, seq_len, col0, ncols, glu, out_dtype):
    m, k = x.shape
    tm = _tile(seq_len, 1024)
    tn = _tile(ncols, 512 if glu else 1024)
    nm = m // tm
    seq_tiles = seq_len // tm
    assert col0 % tn == 0
    ncomp = 2 if glu else 1
    coffs = [c * (ncols // tn) for c in range(ncomp)]
    woffs = [col0 // tn + o for o in coffs]
    cb3 = cb.reshape(cb.shape[0], 1, cb.shape[1])
    in_specs = [pl.BlockSpec((tm, k), lambda j, i: (jnp.minimum(i, nm - 1), 0))]
    in_specs += [pl.BlockSpec((None, k, tn), lambda j, i, o=o: (layer, 0, o + j)) for o in woffs]
    in_specs += [pl.BlockSpec((None, 3, tn), lambda j, i, o=o: (layer, 0, o + j)) for o in coffs]
    in_specs += [pl.BlockSpec((None, 1, tn), lambda j, i, o=o: (layer, 0, o + j)) for o in coffs]
    args = [x] + [w] * ncomp + [cw] * ncomp + [cb3] * ncomp
    return pl.pallas_call(
        functools.partial(_matmul_conv_kernel, glu=glu, tm=tm, seq_tiles=seq_tiles),
        grid=(ncols // tn, nm + 1),
        in_specs=in_specs,
        out_specs=pl.BlockSpec((tm, tn), lambda j, i: (jnp.maximum(i - 1, 0), j)),
        out_shape=jax.ShapeDtypeStruct((m, ncols), out_dtype),
        scratch_shapes=[pltpu.VMEM((k, ncomp * tn), BF16), pltpu.VMEM((2, tm, ncomp * tn), F32),
                        pltpu.VMEM((ncomp, 1, tn), F32)],
        compiler_params=_params("parallel", "arbitrary"),
        name="matmul_conv_glu" if glu else "matmul_conv",
    )(*args)


def _gmlp_kernel(p_ref, gain_ref, ws_ref, bs_ref, o_ref, *, tm, width):
    p = p_ref[...]
    g = 0.5 * p * (1.0 + lax.erf(p * (1.0 / math.sqrt(2.0))))
    u = g[:, :width]
    v = g[:, width:]
    ms = jnp.mean(v * v, axis=-1, keepdims=True)
    vb = (v * lax.rsqrt(ms + EPS) * gain_ref[...]).astype(BF16)
    gd = width // A_GROUPS
    for c in range(tm // CHUNK):
        r0 = c * CHUNK
        for gi in range(A_GROUPS):
            c0 = gi * gd
            s = jnp.dot(ws_ref[gi], vb[r0:r0 + CHUNK, c0:c0 + gd], preferred_element_type=F32)
            s = s + bs_ref[:, c0:c0 + gd]
            o_ref[r0:r0 + CHUNK, c0:c0 + gd] = (u[r0:r0 + CHUNK, c0:c0 + gd] * s).astype(o_ref.dtype)


def spatial_gating(p, v_gain, w_s, b_s):
    m, two_w = p.shape
    width = two_w // 2
    gd = width // A_GROUPS
    tm = _tile(m, 512)
    bs_full = jnp.repeat(b_s.T, gd, axis=1)
    return pl.pallas_call(
        functools.partial(_gmlp_kernel, tm=tm, width=width),
        grid=(m // tm,),
        in_specs=[pl.BlockSpec((tm, two_w), lambda i: (i, 0)),
                  pl.BlockSpec((1, width), lambda i: (0, 0)),
                  pl.BlockSpec((A_GROUPS, CHUNK, CHUNK), lambda i: (0, 0, 0)),
                  pl.BlockSpec((CHUNK, width), lambda i: (0, 0))],
        out_specs=pl.BlockSpec((tm, width), lambda i: (i, 0)),
        out_shape=jax.ShapeDtypeStruct((m, width), BF16),
        compiler_params=_params("parallel"),
        name="spatial_gating",
    )(p, v_gain.reshape(1, width), w_s.astype(BF16), bs_full)


def _dft_matrices(L):
    n = 2 * L
    r1 = 64
    r0 = L // r1
    t = np.arange(L, dtype=np.int64)[None, :]
    a1 = ((r0 * np.arange(r1, dtype=np.int64))[:, None] * t) % n
    a0 = (np.arange(r0, dtype=np.int64)[:, None] * t) % n
    ca, sa, cb, sb = lax.optimization_barrier((
        jnp.asarray(np.cos(2 * np.pi * a1 / n), F32)[:, None, :],
        jnp.asarray(np.sin(2 * np.pi * a1 / n), F32)[:, None, :],
        jnp.asarray(np.cos(2 * np.pi * a0 / n), F32)[None, :, :],
        jnp.asarray(np.sin(2 * np.pi * a0 / n), F32)[None, :, :]))
    cosm = (ca * cb - sa * sb).astype(BF16).reshape(L, L)
    sinm = (-(sa * cb + ca * sb)).astype(BF16).reshape(L, L)
    return cosm, sinm


def _filter_features(L):
    t = np.linspace(0.0, 1.0, L)[:, None]
    w = 2.0 * np.pi * np.arange(L)[:, None] / L
    bands = np.linspace(1e-4, HYENA_BANDS - 1, HYENA_BANDS)[None, :]
    z = np.concatenate([t, np.cos(w * bands), -np.sin(w * bands)], axis=-1)
    return jnp.asarray(np.pad(z, ((0, 0), (0, 128 - HYENA_EMB))), F32)


def _alternating(shape):
    rows = lax.broadcasted_iota(jnp.int32, shape, 0)
    return jnp.where(rows % 2 == 0, 1.0, -1.0)


def _twiddles(L):
    g = np.arange(L // 4)[:, None] * np.ones((1, LANES))
    a1, a2 = 2.0 * np.pi * g / L, 2.0 * np.pi * g / (2 * L)
    return jnp.asarray(np.stack([np.cos(a1), np.sin(a1), np.cos(a2), np.sin(a2)]), F32)


def _cadd(a, b):
    return a[0] + b[0], a[1] + b[1]


def _csub(a, b):
    return a[0] - b[0], a[1] - b[1]


def _conj(a):
    return a[0], -a[1]


def _cmul(a, b):
    return a[0] * b[0] - a[1] * b[1], a[0] * b[1] + a[1] * b[0]


def _rot(a, c, s):
    return a[0] * c + a[1] * s, a[1] * c - a[0] * s


def _unrot(a, c, s):
    return a[0] * c - a[1] * s, a[1] * c + a[0] * s


def _times_i(a):
    return -a[1], a[0]


def _times_minus_i(a):
    return a[1], -a[0]


def _lane_block(x, j):
    return x[:, j * LANES:(j + 1) * LANES]


def _period8_patterns():
    t = np.arange(8)[:, None] * np.ones((1, LANES))
    rows = [np.cos(np.pi * t)]
    for k in (2, 1, 3):
        rows += [np.cos(k * np.pi * t / 4), np.sin(k * np.pi * t / 4)]
    return jnp.asarray(np.round(np.stack(rows), 15), F32)


def _filter_spectrum_kernel(z_ref, w1_ref, b1_ref, w2_ref, b2_ref, fr_ref, w3f_ref, w3b_ref, dl_ref,
                            pat_ref, cos_ref, sin_ref, o_ref, rc_ref, rs_ref, nyq_ref, h_ref, taps_ref,
                            *, n):
    m = pl.program_id(2)
    hp = lax.Precision.HIGHEST

    @pl.when(jnp.logical_and(jnp.logical_and(pl.program_id(0) == 0, pl.program_id(1) == 0), m == 0))
    def _():
        h = jnp.dot(z_ref[...], w1_ref[...], precision=hp, preferred_element_type=F32) + b1_ref[...]
        h = jnp.sin(fr_ref[0:1, :] * h)
        h = jnp.dot(h, w2_ref[...], precision=hp, preferred_element_type=F32) + b2_ref[...]
        h_ref[...] = jnp.sin(fr_ref[1:2, :] * h)

    @pl.when(m == 0)
    def _():
        length = z_ref.shape[0]
        chunk = min(length, 2 * DFT_TILE)
        nrm = None
        tc = taps_ref.shape[2]
        w3 = jnp.concatenate([w3f_ref[...], w3b_ref[...]], axis=1)
        for r0 in range(0, length, chunk):
            rs = slice(r0, r0 + chunk)
            decay = jnp.exp(-z_ref[rs, 0:1] * dl_ref[...])
            hfb = jnp.dot(h_ref[rs, :], w3, precision=hp, preferred_element_type=F32)
            hf, hb = hfb[:, :tc] * decay, hfb[:, tc:] * decay
            if r0 == 0:
                hb = jnp.where(lax.broadcasted_iota(jnp.int32, hb.shape, 0) == 0, 0.0, hb)
            part = jnp.sum(jnp.abs(hf) + jnp.abs(hb), axis=0, keepdims=True)
            nrm = part if nrm is None else nrm + part
            taps_ref[0, rs, :] = hf + hb
            taps_ref[1, rs, :] = hf - hb
        inv = 1.0 / nrm
        packed = [None] * 4
        alt, hc, hs, qc, qs, q3c, q3s = (jnp.concatenate([pat_ref[i]] * (chunk // 8), axis=0)
                                         for i in range(7))
        for r0 in range(0, length, chunk):
            rs = slice(r0, r0 + chunk)
            ke = taps_ref[0, rs, :] * inv
            ko = taps_ref[1, rs, :] * inv
            for blk, (vc, vs) in enumerate(((ke, ko), (ke * alt, -ko * alt), (ke * hc, ke * hs),
                                            (ko * hs, ko * hc))):
                rc_ref[rs, blk * tc:(blk + 1) * tc] = vc.astype(BF16)
                rs_ref[rs, blk * tc:(blk + 1) * tc] = vs.astype(BF16)
            parts = (jnp.sum(ke * qc, axis=0, keepdims=True), -jnp.sum(ko * qs, axis=0, keepdims=True),
                     jnp.sum(ke * q3c, axis=0, keepdims=True), -jnp.sum(ko * q3s, axis=0, keepdims=True))
            packed = [pt if acc is None else acc + pt for acc, pt in zip(packed, parts)]
        for i in range(4):
            nyq_ref[i:i + 1, :] = packed[i] * (2.0 / n)

    tc = o_ref.shape[2]
    rc = jnp.dot(cos_ref[...], rc_ref[...], preferred_element_type=F32)
    rs = jnp.dot(sin_ref[...], rs_ref[...], preferred_element_type=F32)
    c = [rc[:, i * tc:(i + 1) * tc] for i in range(4)]
    s = [rs[:, i * tc:(i + 1) * tc] for i in range(4)]
    planes = [c[0], s[0], c[1], s[1], c[2] - s[2], -c[3] - s[3], c[2] + s[2], s[3] - c[3]]
    first = jnp.logical_and(lax.broadcasted_iota(jnp.int32, c[0].shape, 0) == 0, m == 0)
    scale = jnp.where(first, 1.0 / n, 2.0 / n)
    slots = {1: 2, 3: 3, 6: 0, 7: 1}
    for i in range(8):
        val = planes[i] * scale
        if i in slots:
            val = jnp.where(first, nyq_ref[slots[i]:slots[i] + 1, :], val)
        o_ref[i] = val


def hyena_filter_spectrum(L, width, cosm, sinm, w1, b1, w2, b2, w3, freq):
    z = _filter_features(L)
    tc = LANES
    ncb = width // tc
    max_decay = math.log(HYENA_TARGET) / HYENA_FAST
    min_decay = math.log(HYENA_TARGET) / HYENA_SLOW
    deltas = jnp.asarray(np.abs(np.linspace(min_decay, max_decay, width)), F32).reshape(1, width)
    w1p = jnp.pad(w1, ((0, 128 - HYENA_EMB), (0, 0)))
    const = lambda o, c, m: (0, 0)
    tm = _tile(L // 4, DFT_TILE)
    return pl.pallas_call(
        functools.partial(_filter_spectrum_kernel, n=2 * L),
        grid=(HYENA_ORDER, ncb, L // 4 // tm),
        in_specs=[pl.BlockSpec((L, 128), const),
                  pl.BlockSpec((128, HYENA_FFN), const), pl.BlockSpec((1, HYENA_FFN), const),
                  pl.BlockSpec((HYENA_FFN, HYENA_FFN), const), pl.BlockSpec((1, HYENA_FFN), const),
                  pl.BlockSpec((2, HYENA_FFN), const),
                  pl.BlockSpec((HYENA_FFN, tc), lambda o, c, m: (0, (2 * o) * ncb + c)),
                  pl.BlockSpec((HYENA_FFN, tc), lambda o, c, m: (0, (2 * o + 1) * ncb + c)),
                  pl.BlockSpec((1, tc), lambda o, c, m: (0, c)),
                  pl.BlockSpec((7, 8, LANES), lambda o, c, m: (0, 0, 0)),
                  pl.BlockSpec((tm, L), lambda o, c, m: (m, 0)),
                  pl.BlockSpec((tm, L), lambda o, c, m: (m, 0))],
        out_specs=pl.BlockSpec((None, 8, tm, tc), lambda o, c, m: (o, 0, m, c)),
        out_shape=jax.ShapeDtypeStruct((HYENA_ORDER, 8, L // 4, width), F32),
        scratch_shapes=[pltpu.VMEM((L, 4 * tc), BF16), pltpu.VMEM((L, 4 * tc), BF16),
                        pltpu.VMEM((4, tc), F32), pltpu.VMEM((L, HYENA_FFN), F32),
                        pltpu.VMEM((2, L, tc), F32)],
        compiler_params=_params("arbitrary", "arbitrary", "arbitrary"),
        name="hyena_filter_spectrum",
    )(z, w1p, b1.reshape(1, -1), w2, b2.reshape(1, -1), freq, w3, w3, deltas, _period8_patterns(),
      cosm, sinm)


def _dft_forward_kernel(u_ref, cos_ref, sin_ref, tw_ref, kf_ref, o_ref, ub_ref):
    q = ub_ref.shape[0]
    nyq = []
    for j in range(4):
        xj = u_ref[pl.ds(j, q, stride=4), :]
        ub_ref[:, j * LANES:(j + 1) * LANES] = xj.astype(BF16)
        nyq.append(jnp.sum(xj * _alternating(xj.shape), axis=0, keepdims=True))
    eq, oq = (nyq[0], -nyq[2]), (nyq[1], -nyq[3])
    woq = _rot(oq, math.sqrt(0.5), math.sqrt(0.5))
    u_q, u_3q = _cadd(eq, woq), _conj(_csub(eq, woq))
    chunk = min(q, DFT_TILE)
    for r0 in range(0, q, chunk):
        rs = slice(r0, r0 + chunk)
        re = jnp.dot(cos_ref[rs, :], ub_ref[...], preferred_element_type=F32)
        im = jnp.dot(sin_ref[rs, :], ub_ref[...], preferred_element_type=F32)
        x = [(_lane_block(re, j), _lane_block(im, j)) for j in range(4)]
        c1, s1, c2, s2 = tw_ref[0, rs, :], tw_ref[1, rs, :], tw_ref[2, rs, :], tw_ref[3, rs, :]
        t2, t3 = _rot(x[2], c1, s1), _rot(x[3], c1, s1)
        ea, eb = _cadd(x[0], t2), _csub(x[0], t2)
        oa, ob = _cadd(x[1], t3), _csub(x[1], t3)
        ta, tb = _rot(oa, c2, s2), _times_i(_rot(ob, c2, s2))
        p = [_cadd(ea, ta), _conj(_csub(ea, ta)), _conj(_cadd(eb, tb)), _csub(eb, tb)]
        k = [(kf_ref[2 * i, rs, :], kf_ref[2 * i + 1, rs, :]) for i in range(4)]
        if r0 == 0:
            first = lax.broadcasted_iota(jnp.int32, re.shape[:1] + (LANES,), 0) == 0
            p[3] = (jnp.where(first, u_q[0], p[3][0]), jnp.where(first, u_q[1], p[3][1]))
        y = [_cmul(p[i], k[i]) for i in range(4)]
        if r0 == 0:
            y3q = _cmul(u_3q, (k[0][1], k[1][1]))
            y[0] = (y[0][0], jnp.where(first, y3q[0], y[0][1]))
            y[1] = (y[1][0], jnp.where(first, y3q[1], y[1][1]))
        for i in range(4):
            o_ref[2 * i, rs, :] = y[i][0].astype(o_ref.dtype)
            o_ref[2 * i + 1, rs, :] = y[i][1].astype(o_ref.dtype)


def _resident(shape):
    return pl.BlockSpec(shape, lambda c, b: (0,) * len(shape), pipeline_mode=pl.Buffered(1))


def dft_forward_times_filter(u, col0, width, cos4, sin4, tw, kf, order):
    bsz, L, _ = u.shape
    q = L // 4
    assert col0 % LANES == 0
    cb0 = col0 // LANES
    return pl.pallas_call(
        _dft_forward_kernel,
        grid=(width // LANES, bsz),
        in_specs=[pl.BlockSpec((None, L, LANES), lambda c, b: (b, 0, cb0 + c)),
                  _resident((q, q)), _resident((q, q)), _resident((4, q, LANES)),
                  pl.BlockSpec((None, 8, q, LANES), lambda c, b: (order, 0, 0, c))],
        out_specs=pl.BlockSpec((None, 8, q, LANES), lambda c, b: (b, 0, 0, c)),
        out_shape=jax.ShapeDtypeStruct((bsz, 8, q, width), BF16),
        scratch_shapes=[pltpu.VMEM((q, 4 * LANES), BF16)],
        compiler_params=_params("parallel", "parallel"),
        name="dft_forward",
    )(u, cos4, sin4, tw, kf)


def _dft_inverse_kernel(y_ref, cos_ref, sin_ref, tw_ref, u_ref, g_ref, skip_ref, o_ref, x_ref, res_ref):
    q = x_ref.shape[1]

    def row0(i):
        return y_ref[i, 0:1, :].astype(F32)

    u_q, u_3q = (row0(6), row0(7)), (row0(1), row0(3))
    eq = _cadd(u_q, _conj(u_3q))
    oq = _unrot(_csub(u_q, _conj(u_3q)), math.sqrt(0.5), math.sqrt(0.5))
    nyq = jnp.concatenate([eq[0], oq[0], -eq[1], -oq[1]], axis=1)
    chunk = min(q, DFT_TILE)
    for r0 in range(0, q, chunk):
        rs = slice(r0, r0 + chunk)
        p = [(y_ref[2 * i, rs, :].astype(F32), y_ref[2 * i + 1, rs, :].astype(F32)) for i in range(4)]
        if r0 == 0:
            first = lax.broadcasted_iota(jnp.int32, p[0][0].shape, 0) == 0
            p[0] = (p[0][0], jnp.where(first, 0.0, p[0][1]))
            p[1] = (p[1][0], jnp.where(first, 0.0, p[1][1]))
            p[3] = (jnp.where(first, p[2][0], p[3][0]), jnp.where(first, p[2][1], p[3][1]))
        c1, s1, c2, s2 = tw_ref[0, rs, :], tw_ref[1, rs, :], tw_ref[2, rs, :], tw_ref[3, rs, :]
        ea, ta = _cadd(p[0], _conj(p[1])), _csub(p[0], _conj(p[1]))
        eb, tb = _cadd(_conj(p[2]), p[3]), _csub(_conj(p[2]), p[3])
        oa = _unrot(ta, c2, s2)
        ob = _times_minus_i(_unrot(tb, c2, s2))
        x = [_cadd(ea, eb), _cadd(oa, ob), _unrot(_csub(ea, eb), c1, s1), _unrot(_csub(oa, ob), c1, s1)]
        for j in range(4):
            x_ref[0, rs, j * LANES:(j + 1) * LANES] = x[j][0].astype(BF16)
            x_ref[1, rs, j * LANES:(j + 1) * LANES] = x[j][1].astype(BF16)
    skip = skip_ref[...]
    alt = _alternating((chunk, 4 * LANES))
    for r0 in range(0, q, chunk):
        rs = slice(r0, r0 + chunk)
        conv = jnp.dot(cos_ref[rs, :], x_ref[0], preferred_element_type=F32)
        conv = conv + jnp.dot(sin_ref[rs, :], x_ref[1], preferred_element_type=F32)
        conv = conv + alt * nyq
        for j in range(4):
            rows = pl.ds(4 * r0 + j, chunk, stride=4)
            res_ref[rows, :] = g_ref[rows, :] * (_lane_block(conv, j) + u_ref[rows, :] * skip)
    o_ref[...] = res_ref[...].astype(o_ref.dtype)


def dft_inverse_gated(y, cos4, sin4, tw, u, ucol0, gate, gcol0, skip, out_dtype):
    bsz, _, q, width = y.shape
    L = 4 * q
    ub0, gb0 = ucol0 // LANES, gcol0 // LANES
    return pl.pallas_call(
        _dft_inverse_kernel,
        grid=(width // LANES, bsz),
        in_specs=[pl.BlockSpec((None, 8, q, LANES), lambda c, b: (b, 0, 0, c)),
                  _resident((q, q)), _resident((q, q)), _resident((4, q, LANES)),
                  pl.BlockSpec((None, L, LANES), lambda c, b: (b, 0, ub0 + c)),
                  pl.BlockSpec((None, L, LANES), lambda c, b: (b, 0, gb0 + c)),
                  pl.BlockSpec((1, LANES), lambda c, b: (0, c))],
        out_specs=pl.BlockSpec((None, L, LANES), lambda c, b: (b, 0, c)),
        out_shape=jax.ShapeDtypeStruct((bsz, L, width), out_dtype),
        scratch_shapes=[pltpu.VMEM((2, q, 4 * LANES), BF16), pltpu.VMEM((L, LANES), F32)],
        compiler_params=_params("parallel", "parallel"),
        name="dft_inverse",
    )(y, cos4, sin4, tw, u, gate, skip.reshape(1, width))


NA_GROUP = 8
NA_BLOCK = NA_GROUP + NA_KH
NA_PAD = NA_KH // 2


def _na_plans():
    tiles = {}
    plans = []
    for variant in range(3):
        plan = []
        for rq in range(NA_GROUP):
            lo = (max(rq, NA_PAD), rq, min(rq, NA_PAD))[variant]
            a_lo, a_hi = lo // 2, (lo + NA_KH - 1) // 2 + 1
            ids = []
            for a in range(a_lo, a_hi):
                key = tuple(kr - rq + NA_KH - 1 - NA_PAD if lo <= kr < lo + NA_KH else -1
                            for kr in (2 * a, 2 * a + 1))
                ids.append(tiles.setdefault(key, len(tiles)))
            plan.append((a_lo, tuple(ids)))
        plans.append(tuple(plan))
    return tuple(plans), list(tiles)


def _na_kernel(q_ref, k_ref, v_ref, tiles_ref, o_ref, kb_ref, vb_ref, p_ref, *, rows, scale, plans):
    L = rows * GRID_W
    pad = NA_PAD * GRID_W
    gq = NA_GROUP * GRID_W
    gk = NA_BLOCK * GRID_W
    for ref, src in ((kb_ref, k_ref), (vb_ref, v_ref)):
        ref[0:pad, :] = jnp.zeros((pad, HEAD_DIM), BF16)
        ref[pad + L:, :] = jnp.zeros((pad, HEAD_DIM), BF16)
        ref[pad:pad + L, :] = src[...].astype(BF16)

    def group(q0, plan, slot):
        q = q_ref[pl.ds(q0, gq), :].astype(BF16)
        s = lax.dot_general(q, kb_ref[pl.ds(q0, gk), :], (((1,), (1,)), ((), ())),
                            preferred_element_type=F32)
        p_ref[slot] = jnp.zeros(p_ref.shape[1:], BF16)
        for rq, (a_lo, ids) in enumerate(plan):
            r0, c0, c1 = rq * GRID_W, 2 * GRID_W * a_lo, 2 * GRID_W * (a_lo + len(ids))
            bias = jnp.concatenate([tiles_ref[t] for t in ids], axis=1)
            sl = s[r0:r0 + GRID_W, c0:c1] * scale + bias
            e = jnp.exp(sl - jnp.max(sl, axis=-1, keepdims=True))
            pr = e * (1.0 / jnp.sum(e, axis=-1, keepdims=True))
            p_ref[slot, r0:r0 + GRID_W, c0:c1] = pr.astype(BF16)
        o = jnp.dot(p_ref[slot], vb_ref[pl.ds(q0, gk), :], preferred_element_type=F32)
        o_ref[pl.ds(q0, gq), :] = o.astype(o_ref.dtype)

    ngroups = rows // NA_GROUP
    nmid = ngroups - 2
    nbuf = p_ref.shape[0]
    for g in range(ngroups):
        plan = plans[0] if g == 0 else plans[2] if g == ngroups - 1 else plans[1]
        group(g * gq, plan, g % nbuf)


def _na_bias_tiles(rpb, tile_keys):
    col = np.arange(GRID_W)
    cs = np.clip(col - NA_KW // 2, 0, GRID_W - NA_KW)
    kc = np.arange(GRID_W)
    inwin = (kc[None, :] >= cs[:, None]) & (kc[None, :] < cs[:, None] + NA_KW)
    rel_col = np.clip(kc[None, :] - col[:, None] + (NA_KW - 1), 0, 2 * NA_KW - 2)
    t = jnp.where(jnp.asarray(inwin)[None, None], rpb[:, :, rel_col], MASK_VALUE)
    masked = jnp.full((rpb.shape[0], 1, GRID_W, GRID_W), MASK_VALUE, F32)
    t = jnp.concatenate([t.astype(F32), masked], axis=1)
    left = np.array([k[0] for k in tile_keys])
    right = np.array([k[1] for k in tile_keys])
    return jnp.concatenate([t[:, left], t[:, right]], axis=-1)


def neighborhood_attention(p, bsz, L, heads, rpb):
    rows = L // GRID_W
    assert rows % NA_GROUP == 0 and rows >= 2 * NA_GROUP
    plans, tile_keys = _na_plans()
    tiles = _na_bias_tiles(rpb, tile_keys)
    nt = len(tile_keys)
    lp = L + 2 * NA_PAD * GRID_W
    return pl.pallas_call(
        functools.partial(_na_kernel, rows=rows, scale=HEAD_DIM ** -0.5, plans=plans),
        grid=(bsz, heads),
        in_specs=[pl.BlockSpec((None, L, HEAD_DIM), lambda b, h: (b, 0, h)),
                  pl.BlockSpec((None, L, HEAD_DIM), lambda b, h: (b, 0, heads + h)),
                  pl.BlockSpec((None, L, HEAD_DIM), lambda b, h: (b, 0, 2 * heads + h)),
                  pl.BlockSpec((None, nt, GRID_W, 2 * GRID_W), lambda b, h: (h, 0, 0, 0))],
        out_specs=pl.BlockSpec((None, L, HEAD_DIM), lambda b, h: (b, 0, h)),
        out_shape=jax.ShapeDtypeStruct((bsz, L, heads * HEAD_DIM), BF16),
        scratch_shapes=[pltpu.VMEM((lp, HEAD_DIM), BF16), pltpu.VMEM((lp, HEAD_DIM), BF16),
                        pltpu.VMEM((4, NA_GROUP * GRID_W, NA_BLOCK * GRID_W), BF16)],
        compiler_params=_params("parallel", "parallel"),
        name="neighborhood_attention",
    )(p, p, p, tiles)


DIFF_SUB_ROWS = 256


def _rope(x, cos, sin_signed, half):
    lane = lax.broadcasted_iota(jnp.int32, x.shape, 1)
    width = x.shape[1]
    first = (lane % (2 * half)) < half
    partner = jnp.where(first, pltpu.roll(x, width - half, 1), pltpu.roll(x, half, 1))
    return x * cos + partner * sin_signed


def _diff_attn_kernel(q_ref, k_ref, v_ref, cq_ref, sq_ref, ck_ref, sk_ref, lam_ref, sub_ref, o_ref,
                      kb_ref, vb_ref, *, lam_init, scale, half, n_sub):
    qi = pl.program_id(2)

    @pl.when(qi == 0)
    def _():
        kb_ref[...] = _rope(k_ref[...], ck_ref[...], sk_ref[...], half).astype(BF16)
        vb_ref[...] = v_ref[...].astype(BF16)

    lp = lam_ref[...]
    lam = (jnp.exp(jnp.sum(lp[0:1] * lp[1:2], axis=-1, keepdims=True))
           - jnp.exp(jnp.sum(lp[2:3] * lp[3:4], axis=-1, keepdims=True)) + lam_init)
    kb = kb_ref[...]
    tq = q_ref.shape[0]
    sub = tq // n_sub
    for sb in range(n_sub):
        rs = slice(sb * sub, (sb + 1) * sub)
        q = _rope(q_ref[rs, :], cq_ref[rs, :], sq_ref[rs, :], half) * (scale * math.log2(math.e))
        lane = lax.broadcasted_iota(jnp.int32, q.shape, 1)

        def softmax_map(in_map, weight):
            qm = jnp.where(in_map, q, 0.0).astype(BF16)
            s = lax.dot_general(qm, kb, (((1,), (1,)), ((), ())), preferred_element_type=F32)
            e = jnp.exp2(s - jnp.max(s, axis=-1, keepdims=True))
            return e * (weight / jnp.sum(e, axis=-1, keepdims=True))

        a = softmax_map(lane < 2 * half, 1.0) - softmax_map(lane >= 2 * half, lam)
        o = jnp.dot(a.astype(BF16), vb_ref[...], preferred_element_type=F32)
        ms = jnp.mean(o * o, axis=-1, keepdims=True)
        o_ref[rs, :] = (o * lax.rsqrt(ms + EPS) * sub_ref[...] * (1.0 - lam_init)).astype(o_ref.dtype)


def _rope_tables(L, half):
    inv = (1.0 / (np.float32(ROPE_THETA) ** (np.arange(half, dtype=np.float32) * np.float32(2.0)
                                             / np.float32(2 * half)))).astype(np.float32)
    ang = (np.arange(L, dtype=np.float32)[:, None] * inv[None, :]).astype(np.float64)
    cos, sin = np.cos(ang), np.sin(ang)
    reps = HEAD_DIM // (2 * half)
    cos_t = np.tile(cos, (1, 2 * reps))
    sin_t = np.tile(np.concatenate([-sin, sin], axis=1), (1, reps))
    return jnp.asarray(cos_t, F32), jnp.asarray(sin_t, F32)


def diff_attention(p, col0, bsz, L, heads, lam_params, subln, lam_init):
    half = HEAD_DIM // 4
    assert col0 % HEAD_DIM == 0
    c0 = col0 // HEAD_DIM
    tq = _tile(L, 2 * DIFF_SUB_ROWS)
    n_sub = tq // DIFF_SUB_ROWS
    cos_t, sin_t = _rope_tables(L, half)
    return pl.pallas_call(
        functools.partial(_diff_attn_kernel, lam_init=lam_init, scale=(2 * half) ** -0.5, half=half,
                          n_sub=n_sub),
        grid=(bsz, heads, L // tq),
        in_specs=[pl.BlockSpec((None, tq, HEAD_DIM), lambda b, h, i: (b, i, c0 + h)),
                  pl.BlockSpec((None, L, HEAD_DIM), lambda b, h, i: (b, 0, c0 + heads + h)),
                  pl.BlockSpec((None, L, HEAD_DIM), lambda b, h, i: (b, 0, c0 + 2 * heads + h)),
                  pl.BlockSpec((tq, HEAD_DIM), lambda b, h, i: (i, 0)),
                  pl.BlockSpec((tq, HEAD_DIM), lambda b, h, i: (i, 0)),
                  pl.BlockSpec((L, HEAD_DIM), lambda b, h, i: (0, 0)),
                  pl.BlockSpec((L, HEAD_DIM), lambda b, h, i: (0, 0)),
                  pl.BlockSpec((4, 2 * half), lambda b, h, i: (0, 0)),
                  pl.BlockSpec((1, HEAD_DIM), lambda b, h, i: (0, 0))],
        out_specs=pl.BlockSpec((None, tq, HEAD_DIM), lambda b, h, i: (b, i, h)),
        out_shape=jax.ShapeDtypeStruct((bsz, L, heads * HEAD_DIM), BF16),
        scratch_shapes=[pltpu.VMEM((L, HEAD_DIM), BF16), pltpu.VMEM((L, HEAD_DIM), BF16)],
        compiler_params=_params("parallel", "parallel", "arbitrary"),
        name="diff_attention",
    )(p, p, p, cos_t, sin_t, cos_t, sin_t, lam_params, subln.reshape(1, HEAD_DIM))


def kernel(x, norm_mix, norm_ffn, w_out, ffn_up, ffn_conv_w, ffn_conv_b, ffn_down, final_norm,
           ab_w_in, a_vnorm, a_ws, a_bs, b_conv_w, b_conv_b, b_filt_w1, b_filt_b1, b_filt_w2,
           b_filt_b2, b_filt_w3, b_filt_freq, b_skip, cd_w_in, c_rpb, d_lambda, d_subln):
    bsz, L, d = x.shape
    depth = norm_mix.shape[0]
    m = bsz * L
    half_w = d // 2
    ff = ffn_down.shape[1]
    heads = half_w // HEAD_DIM
    assert heads == D_HEADS and half_w // A_GROUPS == CHUNK and L % GRID_W == 0
    xs = x.reshape(m, d)
    cosm, sinm = _dft_matrices(L)
    cos4, sin4 = _dft_matrices(L // 4)
    tw = _twiddles(L)
    w_out_b = w_out.astype(BF16)
    ffn_down_b = ffn_down.astype(BF16)
    h = rmsnorm(xs, norm_mix[0], BF16)
    for l in range(depth):
        i = l // 2
        if l % 2 == 0:
            pa = matmul(h, ab_w_in, i, 0, 2 * half_w, F32)
            ya = spatial_gating(pa, a_vnorm[i], a_ws[i], a_bs[i])
            pb = matmul_conv(h, ab_w_in, b_conv_w, b_conv_b, i, L, 2 * half_w, 3 * half_w, False, F32)
            pb = pb.reshape(bsz, L, 3 * half_w)
            kf = hyena_filter_spectrum(L, half_w, cosm, sinm, b_filt_w1[i], b_filt_b1[i], b_filt_w2[i],
                                       b_filt_b2[i], b_filt_w3[i], b_filt_freq[i])
            y1 = dft_forward_times_filter(pb, 0, half_w, cos4, sin4, tw, kf, 0)
            z = dft_inverse_gated(y1, cos4, sin4, tw, pb, 0, pb, half_w, b_skip[i, 0], F32)
            y2 = dft_forward_times_filter(z, 0, half_w, cos4, sin4, tw, kf, 1)
            yb = dft_inverse_gated(y2, cos4, sin4, tw, z, 0, pb, 2 * half_w, b_skip[i, 1], BF16)
            yb = yb.reshape(m, half_w)
        else:
            p = matmul(h, cd_w_in, i, 0, 6 * half_w, F32).reshape(bsz, L, 6 * half_w)
            lam_init = 0.8 - 0.6 * math.exp(-0.3 * l)
            ya = neighborhood_attention(p, bsz, L, heads, c_rpb[i]).reshape(m, half_w)
            yb = diff_attention(p, 3 * half_w, bsz, L, heads, d_lambda[i], d_subln[i], lam_init)
            yb = yb.reshape(m, half_w)
        xs, hf = outproj_norm(ya, yb, w_out_b, l, xs, norm_ffn[l])
        act = matmul_conv(hf, ffn_up, ffn_conv_w, ffn_conv_b, l, L, 0, ff, True, BF16)
        last = l == depth - 1
        xs, h = downproj_norm(act, ffn_down_b, l, xs, final_norm if last else norm_mix[l + 1],
                              F32 if last else BF16)
    return h.reshape(bsz, L, d)
```

```python
import functools
import math

import numpy as np
import jax
import jax.numpy as jnp
from jax import lax
from jax.experimental import pallas as pl
from jax.experimental.pallas import tpu as pltpu

F32 = jnp.float32
BF16 = jnp.bfloat16

EPS = 1e-6
GRID_W = 64
HEAD_DIM = 128
CHUNK = 128
A_GROUPS = 8
NA_KH = 8
NA_KW = 16
D_HEADS = 8
ROPE_THETA = 10000.0
HYENA_ORDER = 2
HYENA_EMB = 33
HYENA_BANDS = (HYENA_EMB - 1) // 2
HYENA_FFN = 64
HYENA_TARGET = 1e-2
HYENA_FAST = 0.3
HYENA_SLOW = 1.5
MASK_VALUE = -1e30

VMEM_LIMIT_BYTES = 56 * 1024 * 1024
DFT_TILE = 512
LANES = 128


def _params(*sem):
    return pltpu.CompilerParams(dimension_semantics=sem, vmem_limit_bytes=VMEM_LIMIT_BYTES)


def _tile(n, pref):
    t = min(n, pref)
    assert n % t == 0, (n, pref)
    return t


def _rmsnorm_kernel(x_ref, g_ref, o_ref):
    x = x_ref[...]
    ms = jnp.mean(x * x, axis=-1, keepdims=True)
    o_ref[...] = (x * lax.rsqrt(ms + EPS) * g_ref[...]).astype(o_ref.dtype)


def rmsnorm(x, g, out_dtype):
    m, d = x.shape
    tm = _tile(m, 512)
    return pl.pallas_call(
        _rmsnorm_kernel,
        grid=(m // tm,),
        in_specs=[pl.BlockSpec((tm, d), lambda i: (i, 0)), pl.BlockSpec((1, d), lambda i: (0, 0))],
        out_specs=pl.BlockSpec((tm, d), lambda i: (i, 0)),
        out_shape=jax.ShapeDtypeStruct((m, d), out_dtype),
        compiler_params=_params("parallel"),
        name="rmsnorm",
    )(x, g.reshape(1, d))


def _matmul_kernel(x_ref, w_ref, o_ref, wb_ref):
    @pl.when(pl.program_id(1) == 0)
    def _():
        wb_ref[...] = w_ref[...].astype(BF16)

    o_ref[...] = jnp.dot(x_ref[...], wb_ref[...], preferred_element_type=F32).astype(o_ref.dtype)


def matmul(x, w, layer, col0, ncols, out_dtype):
    m, k = x.shape
    tm = _tile(m, 1024)
    tn = _tile(ncols, 1024)
    assert col0 % tn == 0
    jb = col0 // tn
    return pl.pallas_call(
        _matmul_kernel,
        grid=(ncols // tn, m // tm),
        in_specs=[pl.BlockSpec((tm, k), lambda j, i: (i, 0)),
                  pl.BlockSpec((None, k, tn), lambda j, i: (layer, 0, jb + j))],
        out_specs=pl.BlockSpec((tm, tn), lambda j, i: (i, j)),
        out_shape=jax.ShapeDtypeStruct((m, ncols), out_dtype),
        scratch_shapes=[pltpu.VMEM((k, tn), BF16)],
        compiler_params=_params("parallel", "arbitrary"),
        name="matmul",
    )(x, w)


def _outproj_norm_kernel(xa_ref, wa_ref, xb_ref, wb_ref, res_ref, g_ref, o_ref, h_ref):
    acc = res_ref[...] + jnp.dot(xa_ref[...], wa_ref[...], preferred_element_type=F32)
    acc = acc + jnp.dot(xb_ref[...], wb_ref[...], preferred_element_type=F32)
    o_ref[...] = acc
    ms = jnp.mean(acc * acc, axis=-1, keepdims=True)
    h_ref[...] = (acc * lax.rsqrt(ms + EPS) * g_ref[...]).astype(h_ref.dtype)


def outproj_norm(xa, xb, w, layer, res, gain):
    m, n = res.shape
    k = xa.shape[1]
    tm = _tile(m, 512)
    return pl.pallas_call(
        _outproj_norm_kernel,
        grid=(m // tm,),
        in_specs=[pl.BlockSpec((tm, k), lambda i: (i, 0)),
                  pl.BlockSpec((None, k, n), lambda i: (layer, 0, 0)),
                  pl.BlockSpec((tm, k), lambda i: (i, 0)),
                  pl.BlockSpec((None, k, n), lambda i: (layer, 1, 0)),
                  pl.BlockSpec((tm, n), lambda i: (i, 0)),
                  pl.BlockSpec((1, n), lambda i: (0, 0))],
        out_specs=[pl.BlockSpec((tm, n), lambda i: (i, 0)), pl.BlockSpec((tm, n), lambda i: (i, 0))],
        out_shape=[jax.ShapeDtypeStruct((m, n), F32), jax.ShapeDtypeStruct((m, n), BF16)],
        compiler_params=_params("parallel"),
        name="outproj_norm",
    )(xa, w, xb, w, res, gain.reshape(1, n))


def _downproj_norm_kernel(x_ref, w_ref, res_ref, g_ref, o_ref, h_ref):
    acc = res_ref[...] + jnp.dot(x_ref[...], w_ref[...], preferred_element_type=F32)
    o_ref[...] = acc
    ms = jnp.mean(acc * acc, axis=-1, keepdims=True)
    h_ref[...] = (acc * lax.rsqrt(ms + EPS) * g_ref[...]).astype(h_ref.dtype)


def downproj_norm(x, w, layer, res, gain, norm_dtype):
    m, n = res.shape
    k = x.shape[1]
    tm = _tile(m, 256)
    return pl.pallas_call(
        _downproj_norm_kernel,
        grid=(m // tm,),
        in_specs=[pl.BlockSpec((tm, k), lambda i: (i, 0)),
                  pl.BlockSpec((None, k, n), lambda i: (layer, 0, 0), pipeline_mode=pl.Buffered(1)),
                  pl.BlockSpec((tm, n), lambda i: (i, 0)),
                  pl.BlockSpec((1, n), lambda i: (0, 0))],
        out_specs=[pl.BlockSpec((tm, n), lambda i: (i, 0)), pl.BlockSpec((tm, n), lambda i: (i, 0))],
        out_shape=[jax.ShapeDtypeStruct((m, n), F32), jax.ShapeDtypeStruct((m, n), norm_dtype)],
        compiler_params=_params("parallel"),
        name="downproj_norm",
    )(x, w, res, gain.reshape(1, n))


def _matmul_conv_kernel(*refs, glu, tm, seq_tiles):
    ncomp = 2 if glu else 1
    x_ref = refs[0]
    w_refs = refs[1:1 + ncomp]
    cw_refs = refs[1 + ncomp:1 + 2 * ncomp]
    b_refs = refs[1 + 2 * ncomp:1 + 3 * ncomp]
    o_ref = refs[1 + 3 * ncomp]
    wb_ref, acc_ref, carry_ref = refs[2 + 3 * ncomp:5 + 3 * ncomp]
    i = pl.program_id(1)
    slot = i % 2
    ps = 1 - slot

    tn = o_ref.shape[1]

    @pl.when(i == 0)
    def _():
        for c in range(ncomp):
            wb_ref[:, c * tn:(c + 1) * tn] = w_refs[c][...].astype(BF16)
        carry_ref[...] = jnp.zeros_like(carry_ref)
        acc_ref[1] = jnp.zeros(acc_ref.shape[1:], F32)

    acc_ref[slot] = jnp.dot(x_ref[...], wb_ref[...], preferred_element_type=F32)

    has_prev = (i + seq_tiles - 1) % seq_tiles != 0
    has_next = i % seq_tiles != 0
    outs = []
    for c in range(ncomp):
        a = acc_ref[ps, :, c * tn:(c + 1) * tn]
        rows = lax.broadcasted_iota(jnp.int32, a.shape, 0)
        prev_row = jnp.where(has_prev, carry_ref[c], 0.0)
        next_row = jnp.where(has_next, acc_ref[slot, 0:1, c * tn:(c + 1) * tn], 0.0)
        up = jnp.where(rows == 0, prev_row, pltpu.roll(a, 1, 0))
        dn = jnp.where(rows == tm - 1, next_row, pltpu.roll(a, tm - 1, 0))
        cw = cw_refs[c]
        outs.append(up * cw[0:1, :] + a * cw[1:2, :] + dn * cw[2:3, :] + b_refs[c][...])
        carry_ref[c] = a[tm - 1:tm, :]
    if glu:
        g, val = outs
        res = g * (1.0 / (1.0 + jnp.exp(-g))) * val
    else:
        res = outs[0]
    o_ref[...] = res.astype(o_ref.dtype)


def matmul_conv(x, w, cw, cb, layer, seq_len, col0, ncols, glu, out_dtype):
    m, k = x.shape
    tm = _tile(seq_len, 1024)
    tn = _tile(ncols, 512 if glu else 1024)
    nm = m // tm
    seq_tiles = seq_len // tm
    assert col0 % tn == 0
    ncomp = 2 if glu else 1
    coffs = [c * (ncols // tn) for c in range(ncomp)]
    woffs = [col0 // tn + o for o in coffs]
    cb3 = cb.reshape(cb.shape[0], 1, cb.shape[1])
    in_specs = [pl.BlockSpec((tm, k), lambda j, i: (jnp.minimum(i, nm - 1), 0))]
    in_specs += [pl.BlockSpec((None, k, tn), lambda j, i, o=o: (layer, 0, o + j)) for o in woffs]
    in_specs += [pl.BlockSpec((None, 3, tn), lambda j, i, o=o: (layer, 0, o + j)) for o in coffs]
    in_specs += [pl.BlockSpec((None, 1, tn), lambda j, i, o=o: (layer, 0, o + j)) for o in coffs]
    args = [x] + [w] * ncomp + [cw] * ncomp + [cb3] * ncomp
    return pl.pallas_call(
        functools.partial(_matmul_conv_kernel, glu=glu, tm=tm, seq_tiles=seq_tiles),
        grid=(ncols // tn, nm + 1),
        in_specs=in_specs,
        out_specs=pl.BlockSpec((tm, tn), lambda j, i: (jnp.maximum(i - 1, 0), j)),
        out_shape=jax.ShapeDtypeStruct((m, ncols), out_dtype),
        scratch_shapes=[pltpu.VMEM((k, ncomp * tn), BF16), pltpu.VMEM((2, tm, ncomp * tn), F32),
                        pltpu.VMEM((ncomp, 1, tn), F32)],
        compiler_params=_params("parallel", "arbitrary"),
        name="matmul_conv_glu" if glu else "matmul_conv",
    )(*args)


def _gmlp_kernel(p_ref, gain_ref, ws_ref, bs_ref, o_ref, *, tm, width):
    p = p_ref[...]
    g = 0.5 * p * (1.0 + lax.erf(p * (1.0 / math.sqrt(2.0))))
    u = g[:, :width]
    v = g[:, width:]
    ms = jnp.mean(v * v, axis=-1, keepdims=True)
    vb = (v * lax.rsqrt(ms + EPS) * gain_ref[...]).astype(BF16)
    gd = width // A_GROUPS
    for c in range(tm // CHUNK):
        r0 = c * CHUNK
        for gi in range(A_GROUPS):
            c0 = gi * gd
            s = jnp.dot(ws_ref[gi], vb[r0:r0 + CHUNK, c0:c0 + gd], preferred_element_type=F32)
            s = s + bs_ref[:, c0:c0 + gd]
            o_ref[r0:r0 + CHUNK, c0:c0 + gd] = (u[r0:r0 + CHUNK, c0:c0 + gd] * s).astype(o_ref.dtype)


def spatial_gating(p, v_gain, w_s, b_s):
    m, two_w = p.shape
    width = two_w // 2
    gd = width // A_GROUPS
    tm = _tile(m, 512)
    bs_full = jnp.repeat(b_s.T, gd, axis=1)
    return pl.pallas_call(
        functools.partial(_gmlp_kernel, tm=tm, width=width),
        grid=(m // tm,),
        in_specs=[pl.BlockSpec((tm, two_w), lambda i: (i, 0)),
                  pl.BlockSpec((1, width), lambda i: (0, 0)),
                  pl.BlockSpec((A_GROUPS, CHUNK, CHUNK), lambda i: (0, 0, 0)),
                  pl.BlockSpec((CHUNK, width), lambda i: (0, 0))],
        out_specs=pl.BlockSpec((tm, width), lambda i: (i, 0)),
        out_shape=jax.ShapeDtypeStruct((m, width), BF16),
        compiler_params=_params("parallel"),
        name="spatial_gating",
    )(p, v_gain.reshape(1, width), w_s.astype(BF16), bs_full)


def _dft_matrices(L, rows=None):
    n = 2 * L
    r0 = L // 64
    rows = L if rows is None else rows
    assert rows % r0 == 0
    r1 = rows // r0
    t = np.arange(L, dtype=np.int64)[None, :]
    a1 = ((r0 * np.arange(r1, dtype=np.int64))[:, None] * t) % n
    a0 = (np.arange(r0, dtype=np.int64)[:, None] * t) % n
    ca, sa, cb, sb = lax.optimization_barrier((
        jnp.asarray(np.cos(2 * np.pi * a1 / n), F32)[:, None, :],
        jnp.asarray(np.sin(2 * np.pi * a1 / n), F32)[:, None, :],
        jnp.asarray(np.cos(2 * np.pi * a0 / n), F32)[None, :, :],
        jnp.asarray(np.sin(2 * np.pi * a0 / n), F32)[None, :, :]))
    cosm = (ca * cb - sa * sb).astype(BF16).reshape(rows, L)
    sinm = (-(sa * cb + ca * sb)).astype(BF16).reshape(rows, L)
    return cosm, sinm


def _filter_features(L):
    t = np.linspace(0.0, 1.0, L)[:, None]
    w = 2.0 * np.pi * np.arange(L)[:, None] / L
    bands = np.linspace(1e-4, HYENA_BANDS - 1, HYENA_BANDS)[None, :]
    z = np.concatenate([t, np.cos(w * bands), -np.sin(w * bands)], axis=-1)
    return jnp.asarray(np.pad(z, ((0, 0), (0, 128 - HYENA_EMB))), F32)


def _alternating(shape):
    rows = lax.broadcasted_iota(jnp.int32, shape, 0)
    return jnp.where(rows % 2 == 0, 1.0, -1.0)


def _twiddles(L):
    g = np.arange(L // 4)[:, None] * np.ones((1, LANES))
    a1, a2 = 2.0 * np.pi * g / L, 2.0 * np.pi * g / (2 * L)
    return jnp.asarray(np.stack([np.cos(a1), np.sin(a1), np.cos(a2), np.sin(a2)]), F32)


def _cadd(a, b):
    return a[0] + b[0], a[1] + b[1]


def _csub(a, b):
    return a[0] - b[0], a[1] - b[1]


def _conj(a):
    return a[0], -a[1]


def _cmul(a, b):
    return a[0] * b[0] - a[1] * b[1], a[0] * b[1] + a[1] * b[0]


def _rot(a, c, s):
    return a[0] * c + a[1] * s, a[1] * c - a[0] * s


def _unrot(a, c, s):
    return a[0] * c - a[1] * s, a[1] * c + a[0] * s


def _times_i(a):
    return -a[1], a[0]


def _times_minus_i(a):
    return a[1], -a[0]


def _lane_block(x, j):
    return x[:, j * LANES:(j + 1) * LANES]


def _period8_patterns():
    t = np.arange(8)[:, None] * np.ones((1, LANES))
    rows = [np.cos(np.pi * t)]
    for k in (2, 1, 3):
        rows += [np.cos(k * np.pi * t / 4), np.sin(k * np.pi * t / 4)]
    return jnp.asarray(np.round(np.stack(rows), 15), F32)


def _filter_spectrum_kernel(z_ref, w1_ref, b1_ref, w2_ref, b2_ref, fr_ref, w3f_ref, w3b_ref, dl_ref,
                            pat_ref, cos_ref, sin_ref, o_ref, rc_ref, rs_ref, nyq_ref, h_ref, taps_ref,
                            *, n):
    m = pl.program_id(2)
    hp = lax.Precision.HIGHEST

    @pl.when(jnp.logical_and(jnp.logical_and(pl.program_id(0) == 0, pl.program_id(1) == 0), m == 0))
    def _():
        h = jnp.dot(z_ref[...], w1_ref[...], precision=hp, preferred_element_type=F32) + b1_ref[...]
        h = jnp.sin(fr_ref[0:1, :] * h)
        h = jnp.dot(h, w2_ref[...], precision=hp, preferred_element_type=F32) + b2_ref[...]
        h_ref[...] = jnp.sin(fr_ref[1:2, :] * h)

    @pl.when(m == 0)
    def _():
        length = z_ref.shape[0]
        chunk = min(length, 2 * DFT_TILE)
        nrm = None
        tc = taps_ref.shape[2]
        w3 = jnp.concatenate([w3f_ref[...], w3b_ref[...]], axis=1)
        for r0 in range(0, length, chunk):
            rs = slice(r0, r0 + chunk)
            decay = jnp.exp(-z_ref[rs, 0:1] * dl_ref[...])
            hfb = jnp.dot(h_ref[rs, :], w3, precision=hp, preferred_element_type=F32)
            hf, hb = hfb[:, :tc] * decay, hfb[:, tc:] * decay
            if r0 == 0:
                hb = jnp.where(lax.broadcasted_iota(jnp.int32, hb.shape, 0) == 0, 0.0, hb)
            part = jnp.sum(jnp.abs(hf) + jnp.abs(hb), axis=0, keepdims=True)
            nrm = part if nrm is None else nrm + part
            taps_ref[0, rs, :] = hf + hb
            taps_ref[1, rs, :] = hf - hb
        inv = 1.0 / nrm
        packed = [None] * 4
        alt, hc, hs, qc, qs, q3c, q3s = (jnp.concatenate([pat_ref[i]] * (chunk // 8), axis=0)
                                         for i in range(7))
        for r0 in range(0, length, chunk):
            rs = slice(r0, r0 + chunk)
            ke = taps_ref[0, rs, :] * inv
            ko = taps_ref[1, rs, :] * inv
            for blk, (vc, vs) in enumerate(((ke, ko), (ke * alt, -ko * alt), (ke * hc, ke * hs),
                                            (ko * hs, ko * hc))):
                rc_ref[rs, blk * tc:(blk + 1) * tc] = vc.astype(BF16)
                rs_ref[rs, blk * tc:(blk + 1) * tc] = vs.astype(BF16)
            parts = (jnp.sum(ke * qc, axis=0, keepdims=True), -jnp.sum(ko * qs, axis=0, keepdims=True),
                     jnp.sum(ke * q3c, axis=0, keepdims=True), -jnp.sum(ko * q3s, axis=0, keepdims=True))
            packed = [pt if acc is None else acc + pt for acc, pt in zip(packed, parts)]
        for i in range(4):
            nyq_ref[i:i + 1, :] = packed[i] * (2.0 / n)

    tc = o_ref.shape[2]
    rc = jnp.dot(cos_ref[...], rc_ref[...], preferred_element_type=F32)
    rs = jnp.dot(sin_ref[...], rs_ref[...], preferred_element_type=F32)
    c = [rc[:, i * tc:(i + 1) * tc] for i in range(4)]
    s = [rs[:, i * tc:(i + 1) * tc] for i in range(4)]
    planes = [c[0], s[0], c[1], s[1], c[2] - s[2], -c[3] - s[3], c[2] + s[2], s[3] - c[3]]
    first = jnp.logical_and(lax.broadcasted_iota(jnp.int32, c[0].shape, 0) == 0, m == 0)
    scale = jnp.where(first, 1.0 / n, 2.0 / n)
    slots = {1: 2, 3: 3, 6: 0, 7: 1}
    for i in range(8):
        val = planes[i] * scale
        if i in slots:
            val = jnp.where(first, nyq_ref[slots[i]:slots[i] + 1, :], val)
        o_ref[i] = val


def hyena_filter_spectrum(L, width, cosm, sinm, w1, b1, w2, b2, w3, freq):
    z = _filter_features(L)
    tc = LANES
    ncb = width // tc
    max_decay = math.log(HYENA_TARGET) / HYENA_FAST
    min_decay = math.log(HYENA_TARGET) / HYENA_SLOW
    deltas = jnp.asarray(np.abs(np.linspace(min_decay, max_decay, width)), F32).reshape(1, width)
    w1p = jnp.pad(w1, ((0, 128 - HYENA_EMB), (0, 0)))
    const = lambda o, c, m: (0, 0)
    tm = _tile(L // 4, DFT_TILE)
    return pl.pallas_call(
        functools.partial(_filter_spectrum_kernel, n=2 * L),
        grid=(HYENA_ORDER, ncb, L // 4 // tm),
        in_specs=[pl.BlockSpec((L, 128), const),
                  pl.BlockSpec((128, HYENA_FFN), const), pl.BlockSpec((1, HYENA_FFN), const),
                  pl.BlockSpec((HYENA_FFN, HYENA_FFN), const), pl.BlockSpec((1, HYENA_FFN), const),
                  pl.BlockSpec((2, HYENA_FFN), const),
                  pl.BlockSpec((HYENA_FFN, tc), lambda o, c, m: (0, (2 * o) * ncb + c)),
                  pl.BlockSpec((HYENA_FFN, tc), lambda o, c, m: (0, (2 * o + 1) * ncb + c)),
                  pl.BlockSpec((1, tc), lambda o, c, m: (0, c)),
                  pl.BlockSpec((7, 8, LANES), lambda o, c, m: (0, 0, 0)),
                  pl.BlockSpec((tm, L), lambda o, c, m: (m, 0)),
                  pl.BlockSpec((tm, L), lambda o, c, m: (m, 0))],
        out_specs=pl.BlockSpec((None, 8, tm, tc), lambda o, c, m: (o, 0, m, c)),
        out_shape=jax.ShapeDtypeStruct((HYENA_ORDER, 8, L // 4, width), F32),
        scratch_shapes=[pltpu.VMEM((L, 4 * tc), BF16), pltpu.VMEM((L, 4 * tc), BF16),
                        pltpu.VMEM((4, tc), F32), pltpu.VMEM((L, HYENA_FFN), F32),
                        pltpu.VMEM((2, L, tc), F32)],
        compiler_params=_params("arbitrary", "arbitrary", "arbitrary"),
        name="hyena_filter_spectrum",
    )(z, w1p, b1.reshape(1, -1), w2, b2.reshape(1, -1), freq, w3, w3, deltas, _period8_patterns(),
      cosm, sinm)


def _dft_forward_kernel(u_ref, cos_ref, sin_ref, tw_ref, kf_ref, o_ref, ub_ref):
    q = ub_ref.shape[0]
    nyq = []
    for j in range(4):
        xj = u_ref[pl.ds(j, q, stride=4), :]
        ub_ref[:, j * LANES:(j + 1) * LANES] = xj.astype(BF16)
        nyq.append(jnp.sum(xj * _alternating(xj.shape), axis=0, keepdims=True))
    eq, oq = (nyq[0], -nyq[2]), (nyq[1], -nyq[3])
    woq = _rot(oq, math.sqrt(0.5), math.sqrt(0.5))
    u_q, u_3q = _cadd(eq, woq), _conj(_csub(eq, woq))
    chunk = min(q, DFT_TILE)
    for r0 in range(0, q, chunk):
        rs = slice(r0, r0 + chunk)
        re = jnp.dot(cos_ref[rs, :], ub_ref[...], preferred_element_type=F32)
        im = jnp.dot(sin_ref[rs, :], ub_ref[...], preferred_element_type=F32)
        x = [(_lane_block(re, j), _lane_block(im, j)) for j in range(4)]
        c1, s1, c2, s2 = tw_ref[0, rs, :], tw_ref[1, rs, :], tw_ref[2, rs, :], tw_ref[3, rs, :]
        t2, t3 = _rot(x[2], c1, s1), _rot(x[3], c1, s1)
        ea, eb = _cadd(x[0], t2), _csub(x[0], t2)
        oa, ob = _cadd(x[1], t3), _csub(x[1], t3)
        ta, tb = _rot(oa, c2, s2), _times_i(_rot(ob, c2, s2))
        p = [_cadd(ea, ta), _conj(_csub(ea, ta)), _conj(_cadd(eb, tb)), _csub(eb, tb)]
        k = [(kf_ref[2 * i, rs, :], kf_ref[2 * i + 1, rs, :]) for i in range(4)]
        if r0 == 0:
            first = lax.broadcasted_iota(jnp.int32, re.shape[:1] + (LANES,), 0) == 0
            p[3] = (jnp.where(first, u_q[0], p[3][0]), jnp.where(first, u_q[1], p[3][1]))
        y = [_cmul(p[i], k[i]) for i in range(4)]
        if r0 == 0:
            y3q = _cmul(u_3q, (k[0][1], k[1][1]))
            y[0] = (y[0][0], jnp.where(first, y3q[0], y[0][1]))
            y[1] = (y[1][0], jnp.where(first, y3q[1], y[1][1]))
        for i in range(4):
            o_ref[2 * i, rs, :] = y[i][0].astype(o_ref.dtype)
            o_ref[2 * i + 1, rs, :] = y[i][1].astype(o_ref.dtype)


def _resident(shape):
    return pl.BlockSpec(shape, lambda c, b: (0,) * len(shape), pipeline_mode=pl.Buffered(1))


def dft_forward_times_filter(u, col0, width, cos4, sin4, tw, kf, order):
    bsz, L, _ = u.shape
    q = L // 4
    assert col0 % LANES == 0
    cb0 = col0 // LANES
    return pl.pallas_call(
        _dft_forward_kernel,
        grid=(width // LANES, bsz),
        in_specs=[pl.BlockSpec((None, L, LANES), lambda c, b: (b, 0, cb0 + c)),
                  _resident((q, q)), _resident((q, q)), _resident((4, q, LANES)),
                  pl.BlockSpec((None, 8, q, LANES), lambda c, b: (order, 0, 0, c))],
        out_specs=pl.BlockSpec((None, 8, q, LANES), lambda c, b: (b, 0, 0, c)),
        out_shape=jax.ShapeDtypeStruct((bsz, 8, q, width), BF16),
        scratch_shapes=[pltpu.VMEM((q, 4 * LANES), BF16)],
        compiler_params=_params("parallel", "parallel"),
        name="dft_forward",
    )(u, cos4, sin4, tw, kf)


def _dft_inverse_kernel(y_ref, cos_ref, sin_ref, tw_ref, u_ref, g_ref, skip_ref, o_ref, x_ref, res_ref):
    q = x_ref.shape[1]

    def row0(i):
        return y_ref[i, 0:1, :].astype(F32)

    u_q, u_3q = (row0(6), row0(7)), (row0(1), row0(3))
    eq = _cadd(u_q, _conj(u_3q))
    oq = _unrot(_csub(u_q, _conj(u_3q)), math.sqrt(0.5), math.sqrt(0.5))
    nyq = jnp.concatenate([eq[0], oq[0], -eq[1], -oq[1]], axis=1)
    chunk = min(q, DFT_TILE)
    for r0 in range(0, q, chunk):
        rs = slice(r0, r0 + chunk)
        p = [(y_ref[2 * i, rs, :].astype(F32), y_ref[2 * i + 1, rs, :].astype(F32)) for i in range(4)]
        if r0 == 0:
            first = lax.broadcasted_iota(jnp.int32, p[0][0].shape, 0) == 0
            p[0] = (p[0][0], jnp.where(first, 0.0, p[0][1]))
            p[1] = (p[1][0], jnp.where(first, 0.0, p[1][1]))
            p[3] = (jnp.where(first, p[2][0], p[3][0]), jnp.where(first, p[2][1], p[3][1]))
        c1, s1, c2, s2 = tw_ref[0, rs, :], tw_ref[1, rs, :], tw_ref[2, rs, :], tw_ref[3, rs, :]
        ea, ta = _cadd(p[0], _conj(p[1])), _csub(p[0], _conj(p[1]))
        eb, tb = _cadd(_conj(p[2]), p[3]), _csub(_conj(p[2]), p[3])
        oa = _unrot(ta, c2, s2)
        ob = _times_minus_i(_unrot(tb, c2, s2))
        x = [_cadd(ea, eb), _cadd(oa, ob), _unrot(_csub(ea, eb), c1, s1), _unrot(_csub(oa, ob), c1, s1)]
        for j in range(4):
            x_ref[0, rs, j * LANES:(j + 1) * LANES] = x[j][0].astype(BF16)
            x_ref[1, rs, j * LANES:(j + 1) * LANES] = x[j][1].astype(BF16)
    skip = skip_ref[...]
    alt = _alternating((chunk, 4 * LANES))
    for r0 in range(0, q, chunk):
        rs = slice(r0, r0 + chunk)
        conv = jnp.dot(cos_ref[rs, :], x_ref[0], preferred_element_type=F32)
        conv = conv + jnp.dot(sin_ref[rs, :], x_ref[1], preferred_element_type=F32)
        conv = conv + alt * nyq
        for j in range(4):
            rows = pl.ds(4 * r0 + j, chunk, stride=4)
            res_ref[rows, :] = g_ref[rows, :] * (_lane_block(conv, j) + u_ref[rows, :] * skip)
    o_ref[...] = res_ref[...].astype(o_ref.dtype)


def dft_inverse_gated(y, cos4, sin4, tw, u, ucol0, gate, gcol0, skip, out_dtype):
    bsz, _, q, width = y.shape
    L = 4 * q
    ub0, gb0 = ucol0 // LANES, gcol0 // LANES
    return pl.pallas_call(
        _dft_inverse_kernel,
        grid=(width // LANES, bsz),
        in_specs=[pl.BlockSpec((None, 8, q, LANES), lambda c, b: (b, 0, 0, c)),
                  _resident((q, q)), _resident((q, q)), _resident((4, q, LANES)),
                  pl.BlockSpec((None, L, LANES), lambda c, b: (b, 0, ub0 + c)),
                  pl.BlockSpec((None, L, LANES), lambda c, b: (b, 0, gb0 + c)),
                  pl.BlockSpec((1, LANES), lambda c, b: (0, c))],
        out_specs=pl.BlockSpec((None, L, LANES), lambda c, b: (b, 0, c)),
        out_shape=jax.ShapeDtypeStruct((bsz, L, width), out_dtype),
        scratch_shapes=[pltpu.VMEM((2, q, 4 * LANES), BF16), pltpu.VMEM((L, LANES), F32)],
        compiler_params=_params("parallel", "parallel"),
        name="dft_inverse",
    )(y, cos4, sin4, tw, u, gate, skip.reshape(1, width))


NA_GROUP = 8
NA_BLOCK = NA_GROUP + NA_KH
NA_PAD = NA_KH // 2


def _na_plans():
    tiles = {}
    plans = []
    for variant in range(3):
        plan = []
        for rq in range(NA_GROUP):
            lo = (max(rq, NA_PAD), rq, min(rq, NA_PAD))[variant]
            a_lo, a_hi = lo // 2, (lo + NA_KH - 1) // 2 + 1
            ids = []
            for a in range(a_lo, a_hi):
                key = tuple(kr - rq + NA_KH - 1 - NA_PAD if lo <= kr < lo + NA_KH else -1
                            for kr in (2 * a, 2 * a + 1))
                ids.append(tiles.setdefault(key, len(tiles)))
            plan.append((a_lo, tuple(ids)))
        plans.append(tuple(plan))
    return tuple(plans), list(tiles)


def _na_kernel(q_ref, k_ref, v_ref, tiles_ref, o_ref, kb_ref, vb_ref, p_ref, *, rows, scale, plans):
    L = rows * GRID_W
    pad = NA_PAD * GRID_W
    gq = NA_GROUP * GRID_W
    gk = NA_BLOCK * GRID_W
    for ref, src in ((kb_ref, k_ref), (vb_ref, v_ref)):
        ref[0:pad, :] = jnp.zeros((pad, HEAD_DIM), BF16)
        ref[pad + L:, :] = jnp.zeros((pad, HEAD_DIM), BF16)
        ref[pad:pad + L, :] = src[...].astype(BF16)

    def group(q0, plan, slot):
        q = q_ref[pl.ds(q0, gq), :].astype(BF16)
        s = lax.dot_general(q, kb_ref[pl.ds(q0, gk), :], (((1,), (1,)), ((), ())),
                            preferred_element_type=F32)
        p_ref[slot] = jnp.zeros(p_ref.shape[1:], BF16)
        for rq, (a_lo, ids) in enumerate(plan):
            r0, c0, c1 = rq * GRID_W, 2 * GRID_W * a_lo, 2 * GRID_W * (a_lo + len(ids))
            bias = jnp.concatenate([tiles_ref[t] for t in ids], axis=1)
            sl = s[r0:r0 + GRID_W, c0:c1] * scale + bias
            e = jnp.exp(sl - jnp.max(sl, axis=-1, keepdims=True))
            pr = e * (1.0 / jnp.sum(e, axis=-1, keepdims=True))
            p_ref[slot, r0:r0 + GRID_W, c0:c1] = pr.astype(BF16)
        o = jnp.dot(p_ref[slot], vb_ref[pl.ds(q0, gk), :], preferred_element_type=F32)
        o_ref[pl.ds(q0, gq), :] = o.astype(o_ref.dtype)

    ngroups = rows // NA_GROUP
    nbuf = p_ref.shape[0]
    for g in range(ngroups):
        plan = plans[0] if g == 0 else plans[2] if g == ngroups - 1 else plans[1]
        group(g * gq, plan, g % nbuf)


def _na_bias_tiles(rpb, tile_keys):
    col = np.arange(GRID_W)
    cs = np.clip(col - NA_KW // 2, 0, GRID_W - NA_KW)
    kc = np.arange(GRID_W)
    inwin = (kc[None, :] >= cs[:, None]) & (kc[None, :] < cs[:, None] + NA_KW)
    rel_col = np.clip(kc[None, :] - col[:, None] + (NA_KW - 1), 0, 2 * NA_KW - 2)
    t = jnp.where(jnp.asarray(inwin)[None, None], rpb[:, :, rel_col], MASK_VALUE)
    masked = jnp.full((rpb.shape[0], 1, GRID_W, GRID_W), MASK_VALUE, F32)
    t = jnp.concatenate([t.astype(F32), masked], axis=1)
    left = np.array([k[0] for k in tile_keys])
    right = np.array([k[1] for k in tile_keys])
    return jnp.concatenate([t[:, left], t[:, right]], axis=-1)


def neighborhood_attention(p, bsz, L, heads, rpb):
    rows = L // GRID_W
    assert rows % NA_GROUP == 0 and rows >= 2 * NA_GROUP
    plans, tile_keys = _na_plans()
    tiles = _na_bias_tiles(rpb, tile_keys)
    nt = len(tile_keys)
    lp = L + 2 * NA_PAD * GRID_W
    return pl.pallas_call(
        functools.partial(_na_kernel, rows=rows, scale=HEAD_DIM ** -0.5, plans=plans),
        grid=(bsz, heads),
        in_specs=[pl.BlockSpec((None, L, HEAD_DIM), lambda b, h: (b, 0, h)),
                  pl.BlockSpec((None, L, HEAD_DIM), lambda b, h: (b, 0, heads + h)),
                  pl.BlockSpec((None, L, HEAD_DIM), lambda b, h: (b, 0, 2 * heads + h)),
                  pl.BlockSpec((None, nt, GRID_W, 2 * GRID_W), lambda b, h: (h, 0, 0, 0))],
        out_specs=pl.BlockSpec((None, L, HEAD_DIM), lambda b, h: (b, 0, h)),
        out_shape=jax.ShapeDtypeStruct((bsz, L, heads * HEAD_DIM), BF16),
        scratch_shapes=[pltpu.VMEM((lp, HEAD_DIM), BF16), pltpu.VMEM((lp, HEAD_DIM), BF16),
                        pltpu.VMEM((4, NA_GROUP * GRID_W, NA_BLOCK * GRID_W), BF16)],
        compiler_params=_params("parallel", "parallel"),
        name="neighborhood_attention",
    )(p, p, p, tiles)


DIFF_SUB_ROWS = 256


def _rope(x, cos, sin_signed, half):
    lane = lax.broadcasted_iota(jnp.int32, x.shape, 1)
    width = x.shape[1]
    first = (lane % (2 * half)) < half
    partner = jnp.where(first, pltpu.roll(x, width - half, 1), pltpu.roll(x, half, 1))
    return x * cos + partner * sin_signed


def _diff_attn_kernel(q_ref, k_ref, v_ref, cq_ref, sq_ref, ck_ref, sk_ref, lam_ref, sub_ref, o_ref,
                      kb_ref, vb_ref, *, lam_init, scale, half, n_sub):
    qi = pl.program_id(2)

    @pl.when(qi == 0)
    def _():
        kb_ref[...] = _rope(k_ref[...], ck_ref[...], sk_ref[...], half).astype(BF16)
        vb_ref[...] = v_ref[...].astype(BF16)

    lp = lam_ref[...]
    lam = (jnp.exp(jnp.sum(lp[0:1] * lp[1:2], axis=-1, keepdims=True))
           - jnp.exp(jnp.sum(lp[2:3] * lp[3:4], axis=-1, keepdims=True)) + lam_init)
    kb = kb_ref[...]
    tq = q_ref.shape[0]
    sub = tq // n_sub
    for sb in range(n_sub):
        rs = slice(sb * sub, (sb + 1) * sub)
        q = _rope(q_ref[rs, :], cq_ref[rs, :], sq_ref[rs, :], half) * (scale * math.log2(math.e))
        lane = lax.broadcasted_iota(jnp.int32, q.shape, 1)

        def softmax_map(in_map, weight):
            qm = jnp.where(in_map, q, 0.0).astype(BF16)
            s = lax.dot_general(qm, kb, (((1,), (1,)), ((), ())), preferred_element_type=F32)
            e = jnp.exp2(s - jnp.max(s, axis=-1, keepdims=True))
            return e * (weight / jnp.sum(e, axis=-1, keepdims=True))

        a = softmax_map(lane < 2 * half, 1.0) - softmax_map(lane >= 2 * half, lam)
        o = jnp.dot(a.astype(BF16), vb_ref[...], preferred_element_type=F32)
        ms = jnp.mean(o * o, axis=-1, keepdims=True)
        o_ref[rs, :] = (o * lax.rsqrt(ms + EPS) * sub_ref[...] * (1.0 - lam_init)).astype(o_ref.dtype)


def _rope_tables(L, half):
    inv = (1.0 / (np.float32(ROPE_THETA) ** (np.arange(half, dtype=np.float32) * np.float32(2.0)
                                             / np.float32(2 * half)))).astype(np.float32)
    ang = (np.arange(L, dtype=np.float32)[:, None] * inv[None, :]).astype(np.float64)
    cos, sin = np.cos(ang), np.sin(ang)
    reps = HEAD_DIM // (2 * half)
    cos_t = np.tile(cos, (1, 2 * reps))
    sin_t = np.tile(np.concatenate([-sin, sin], axis=1), (1, reps))
    return jnp.asarray(cos_t, F32), jnp.asarray(sin_t, F32)


def diff_attention(p, col0, bsz, L, heads, lam_params, subln, lam_init):
    half = HEAD_DIM // 4
    assert col0 % HEAD_DIM == 0
    c0 = col0 // HEAD_DIM
    tq = _tile(L, 2 * DIFF_SUB_ROWS)
    n_sub = tq // DIFF_SUB_ROWS
    cos_t, sin_t = _rope_tables(L, half)
    return pl.pallas_call(
        functools.partial(_diff_attn_kernel, lam_init=lam_init, scale=(2 * half) ** -0.5, half=half,
                          n_sub=n_sub),
        grid=(bsz, heads, L // tq),
        in_specs=[pl.BlockSpec((None, tq, HEAD_DIM), lambda b, h, i: (b, i, c0 + h)),
                  pl.BlockSpec((None, L, HEAD_DIM), lambda b, h, i: (b, 0, c0 + heads + h)),
                  pl.BlockSpec((None, L, HEAD_DIM), lambda b, h, i: (b, 0, c0 + 2 * heads + h)),
                  pl.BlockSpec((tq, HEAD_DIM), lambda b, h, i: (i, 0)),
                  pl.BlockSpec((tq, HEAD_DIM), lambda b, h, i: (i, 0)),
                  pl.BlockSpec((L, HEAD_DIM), lambda b, h, i: (0, 0)),
                  pl.BlockSpec((L, HEAD_DIM), lambda b, h, i: (0, 0)),
                  pl.BlockSpec((4, 2 * half), lambda b, h, i: (0, 0)),
                  pl.BlockSpec((1, HEAD_DIM), lambda b, h, i: (0, 0))],
        out_specs=pl.BlockSpec((None, tq, HEAD_DIM), lambda b, h, i: (b, i, h)),
        out_shape=jax.ShapeDtypeStruct((bsz, L, heads * HEAD_DIM), BF16),
        scratch_shapes=[pltpu.VMEM((L, HEAD_DIM), BF16), pltpu.VMEM((L, HEAD_DIM), BF16)],
        compiler_params=_params("parallel", "parallel", "arbitrary"),
        name="diff_attention",
    )(p, p, p, cos_t, sin_t, cos_t, sin_t, lam_params, subln.reshape(1, HEAD_DIM))


def kernel(x, norm_mix, norm_ffn, w_out, ffn_up, ffn_conv_w, ffn_conv_b, ffn_down, final_norm,
           ab_w_in, a_vnorm, a_ws, a_bs, b_conv_w, b_conv_b, b_filt_w1, b_filt_b1, b_filt_w2,
           b_filt_b2, b_filt_w3, b_filt_freq, b_skip, cd_w_in, c_rpb, d_lambda, d_subln):
    bsz, L, d = x.shape
    depth = norm_mix.shape[0]
    m = bsz * L
    half_w = d // 2
    ff = ffn_down.shape[1]
    heads = half_w // HEAD_DIM
    assert heads == D_HEADS and half_w // A_GROUPS == CHUNK and L % GRID_W == 0
    xs = x.reshape(m, d)
    cosm, sinm = _dft_matrices(L, rows=L // 4)
    cos4, sin4 = _dft_matrices(L // 4)
    tw = _twiddles(L)
    w_out_b = w_out.astype(BF16)
    ffn_down_b = ffn_down.astype(BF16)
    h = rmsnorm(xs, norm_mix[0], BF16)
    for l in range(depth):
        i = l // 2
        if l % 2 == 0:
            pa = matmul(h, ab_w_in, i, 0, 2 * half_w, F32)
            ya = spatial_gating(pa, a_vnorm[i], a_ws[i], a_bs[i])
            pb = matmul_conv(h, ab_w_in, b_conv_w, b_conv_b, i, L, 2 * half_w, 3 * half_w, False, F32)
            pb = pb.reshape(bsz, L, 3 * half_w)
            kf = hyena_filter_spectrum(L, half_w, cosm, sinm, b_filt_w1[i], b_filt_b1[i], b_filt_w2[i],
                                       b_filt_b2[i], b_filt_w3[i], b_filt_freq[i])
            y1 = dft_forward_times_filter(pb, 0, half_w, cos4, sin4, tw, kf, 0)
            z = dft_inverse_gated(y1, cos4, sin4, tw, pb, 0, pb, half_w, b_skip[i, 0], F32)
            y2 = dft_forward_times_filter(z, 0, half_w, cos4, sin4, tw, kf, 1)
            yb = dft_inverse_gated(y2, cos4, sin4, tw, z, 0, pb, 2 * half_w, b_skip[i, 1], BF16)
            yb = yb.reshape(m, half_w)
        else:
            p = matmul(h, cd_w_in, i, 0, 6 * half_w, F32).reshape(bsz, L, 6 * half_w)
            lam_init = 0.8 - 0.6 * math.exp(-0.3 * l)
            ya = neighborhood_attention(p, bsz, L, heads, c_rpb[i]).reshape(m, half_w)
            yb = diff_attention(p, 3 * half_w, bsz, L, heads, d_lambda[i], d_subln[i], lam_init)
            yb = yb.reshape(m, half_w)
        xs, hf = outproj_norm(ya, yb, w_out_b, l, xs, norm_ffn[l])
        act = matmul_conv(hf, ffn_up, ffn_conv_w, ffn_conv_b, l, L, 0, ff, True, BF16)
        last = l == depth - 1
        xs, h = downproj_norm(act, ffn_down_b, l, xs, final_norm if last else norm_mix[l + 1],
                              F32 if last else BF16)
    return h.reshape(bsz, L, d)
```

```python
import functools
import math

import numpy as np
import jax
import jax.numpy as jnp
from jax import lax
from jax.experimental import pallas as pl
from jax.experimental.pallas import tpu as pltpu

F32 = jnp.float32
BF16 = jnp.bfloat16

EPS = 1e-6
GRID_W = 64
HEAD_DIM = 128
CHUNK = 128
A_GROUPS = 8
NA_KH = 8
NA_KW = 16
D_HEADS = 8
ROPE_THETA = 10000.0
HYENA_ORDER = 2
HYENA_EMB = 33
HYENA_BANDS = (HYENA_EMB - 1) // 2
HYENA_FFN = 64
HYENA_TARGET = 1e-2
HYENA_FAST = 0.3
HYENA_SLOW = 1.5
MASK_VALUE = -1e30

VMEM_LIMIT_BYTES = 56 * 1024 * 1024
DFT_TILE = 512
LANES = 128


def _params(*sem):
    return pltpu.CompilerParams(dimension_semantics=sem, vmem_limit_bytes=VMEM_LIMIT_BYTES)


def _tile(n, pref):
    t = min(n, pref)
    assert n % t == 0, (n, pref)
    return t


def _rmsnorm_kernel(x_ref, g_ref, o_ref):
    x = x_ref[...]
    ms = jnp.mean(x * x, axis=-1, keepdims=True)
    o_ref[...] = (x * lax.rsqrt(ms + EPS) * g_ref[...]).astype(o_ref.dtype)


def rmsnorm(x, g, out_dtype):
    m, d = x.shape
    tm = _tile(m, 512)
    return pl.pallas_call(
        _rmsnorm_kernel,
        grid=(m // tm,),
        in_specs=[pl.BlockSpec((tm, d), lambda i: (i, 0)), pl.BlockSpec((1, d), lambda i: (0, 0))],
        out_specs=pl.BlockSpec((tm, d), lambda i: (i, 0)),
        out_shape=jax.ShapeDtypeStruct((m, d), out_dtype),
        compiler_params=_params("parallel"),
        name="rmsnorm",
    )(x, g.reshape(1, d))


def _matmul_kernel(x_ref, w_ref, o_ref, wb_ref):
    @pl.when(pl.program_id(1) == 0)
    def _():
        wb_ref[...] = w_ref[...].astype(BF16)

    o_ref[...] = jnp.dot(x_ref[...], wb_ref[...], preferred_element_type=F32).astype(o_ref.dtype)


def matmul(x, w, layer, col0, ncols, out_dtype):
    m, k = x.shape
    tm = _tile(m, 1024)
    tn = _tile(ncols, 1024)
    assert col0 % tn == 0
    jb = col0 // tn
    return pl.pallas_call(
        _matmul_kernel,
        grid=(ncols // tn, m // tm),
        in_specs=[pl.BlockSpec((tm, k), lambda j, i: (i, 0)),
                  pl.BlockSpec((None, k, tn), lambda j, i: (layer, 0, jb + j))],
        out_specs=pl.BlockSpec((tm, tn), lambda j, i: (i, j)),
        out_shape=jax.ShapeDtypeStruct((m, ncols), out_dtype),
        scratch_shapes=[pltpu.VMEM((k, tn), BF16)],
        compiler_params=_params("parallel", "arbitrary"),
        name="matmul",
    )(x, w)


def _outproj_norm_kernel(xa_ref, wa_ref, xb_ref, wb_ref, res_ref, g_ref, o_ref, h_ref):
    acc = res_ref[...] + jnp.dot(xa_ref[...], wa_ref[...], preferred_element_type=F32)
    acc = acc + jnp.dot(xb_ref[...], wb_ref[...], preferred_element_type=F32)
    o_ref[...] = acc
    ms = jnp.mean(acc * acc, axis=-1, keepdims=True)
    h_ref[...] = (acc * lax.rsqrt(ms + EPS) * g_ref[...]).astype(h_ref.dtype)


def outproj_norm(xa, xb, w, layer, res, gain):
    m, n = res.shape
    k = xa.shape[1]
    tm = _tile(m, 512)
    return pl.pallas_call(
        _outproj_norm_kernel,
        grid=(m // tm,),
        in_specs=[pl.BlockSpec((tm, k), lambda i: (i, 0)),
                  pl.BlockSpec((None, k, n), lambda i: (layer, 0, 0)),
                  pl.BlockSpec((tm, k), lambda i: (i, 0)),
                  pl.BlockSpec((None, k, n), lambda i: (layer, 1, 0)),
                  pl.BlockSpec((tm, n), lambda i: (i, 0)),
                  pl.BlockSpec((1, n), lambda i: (0, 0))],
        out_specs=[pl.BlockSpec((tm, n), lambda i: (i, 0)), pl.BlockSpec((tm, n), lambda i: (i, 0))],
        out_shape=[jax.ShapeDtypeStruct((m, n), F32), jax.ShapeDtypeStruct((m, n), BF16)],
        compiler_params=_params("parallel"),
        name="outproj_norm",
    )(xa, w, xb, w, res, gain.reshape(1, n))


def _downproj_norm_kernel(x_ref, w_ref, res_ref, g_ref, o_ref, h_ref):
    acc = res_ref[...] + jnp.dot(x_ref[...], w_ref[...], preferred_element_type=F32)
    o_ref[...] = acc
    ms = jnp.mean(acc * acc, axis=-1, keepdims=True)
    h_ref[...] = (acc * lax.rsqrt(ms + EPS) * g_ref[...]).astype(h_ref.dtype)


def downproj_norm(x, w, layer, res, gain, norm_dtype):
    m, n = res.shape
    k = x.shape[1]
    tm = _tile(m, 256)
    return pl.pallas_call(
        _downproj_norm_kernel,
        grid=(m // tm,),
        in_specs=[pl.BlockSpec((tm, k), lambda i: (i, 0)),
                  pl.BlockSpec((None, k, n), lambda i: (layer, 0, 0), pipeline_mode=pl.Buffered(1)),
                  pl.BlockSpec((tm, n), lambda i: (i, 0)),
                  pl.BlockSpec((1, n), lambda i: (0, 0))],
        out_specs=[pl.BlockSpec((tm, n), lambda i: (i, 0)), pl.BlockSpec((tm, n), lambda i: (i, 0))],
        out_shape=[jax.ShapeDtypeStruct((m, n), F32), jax.ShapeDtypeStruct((m, n), norm_dtype)],
        compiler_params=_params("parallel"),
        name="downproj_norm",
    )(x, w, res, gain.reshape(1, n))


def _matmul_conv_kernel(*refs, glu, tm, nm, seq_tiles):
    ncomp = 2 if glu else 1
    x_ref = refs[0]
    w_refs = refs[1:1 + ncomp]
    cw_refs = refs[1 + ncomp:1 + 2 * ncomp]
    b_refs = refs[1 + 2 * ncomp:1 + 3 * ncomp]
    o_ref = refs[1 + 3 * ncomp]
    wb_ref, acc_ref, carry_ref = refs[2 + 3 * ncomp:5 + 3 * ncomp]
    i = pl.program_id(1)
    slot = i % 2
    ps = 1 - slot

    tn = o_ref.shape[1]

    @pl.when(i == 0)
    def _():
        for c in range(ncomp):
            wb_ref[:, c * tn:(c + 1) * tn] = w_refs[c][...].astype(BF16)
        carry_ref[...] = jnp.zeros_like(carry_ref)
        acc_ref[1] = jnp.zeros(acc_ref.shape[1:], F32)

    def finish_previous(tile_follows):
        has_prev = (i + seq_tiles - 1) % seq_tiles != 0
        has_next = i % seq_tiles != 0
        outs = []
        for c in range(ncomp):
            a = acc_ref[ps, :, c * tn:(c + 1) * tn]
            rows = lax.broadcasted_iota(jnp.int32, a.shape, 0)
            prev_row = jnp.where(has_prev, carry_ref[c], 0.0)
            if tile_follows:
                next_row = jnp.where(has_next, acc_ref[slot, 0:1, c * tn:(c + 1) * tn], 0.0)
            else:
                next_row = jnp.zeros((1, tn), F32)
            up = jnp.where(rows == 0, prev_row, pltpu.roll(a, 1, 0))
            dn = jnp.where(rows == tm - 1, next_row, pltpu.roll(a, tm - 1, 0))
            cw = cw_refs[c]
            outs.append(up * cw[0:1, :] + a * cw[1:2, :] + dn * cw[2:3, :] + b_refs[c][...])
            carry_ref[c] = a[tm - 1:tm, :]
        if glu:
            g, val = outs
            res = g * (1.0 / (1.0 + jnp.exp(-g))) * val
        else:
            res = outs[0]
        o_ref[...] = res.astype(o_ref.dtype)

    @pl.when(i < nm)
    def _():
        acc_ref[slot] = jnp.dot(x_ref[...], wb_ref[...], preferred_element_type=F32)
        finish_previous(True)

    @pl.when(i == nm)
    def _():
        finish_previous(False)


def matmul_conv(x, w, cw, cb, layer, seq_len, col0, ncols, glu, out_dtype):
    m, k = x.shape
    tm = _tile(seq_len, 1024)
    tn = _tile(ncols, 512 if glu else 1024)
    nm = m // tm
    seq_tiles = seq_len // tm
    assert col0 % tn == 0
    ncomp = 2 if glu else 1
    coffs = [c * (ncols // tn) for c in range(ncomp)]
    woffs = [col0 // tn + o for o in coffs]
    cb3 = cb.reshape(cb.shape[0], 1, cb.shape[1])
    in_specs = [pl.BlockSpec((tm, k), lambda j, i: (jnp.minimum(i, nm - 1), 0))]
    in_specs += [pl.BlockSpec((None, k, tn), lambda j, i, o=o: (layer, 0, o + j)) for o in woffs]
    in_specs += [pl.BlockSpec((None, 3, tn), lambda j, i, o=o: (layer, 0, o + j)) for o in coffs]
    in_specs += [pl.BlockSpec((None, 1, tn), lambda j, i, o=o: (layer, 0, o + j)) for o in coffs]
    args = [x] + [w] * ncomp + [cw] * ncomp + [cb3] * ncomp
    return pl.pallas_call(
        functools.partial(_matmul_conv_kernel, glu=glu, tm=tm, nm=nm, seq_tiles=seq_tiles),
        grid=(ncols // tn, nm + 1),
        in_specs=in_specs,
        out_specs=pl.BlockSpec((tm, tn), lambda j, i: (jnp.maximum(i - 1, 0), j)),
        out_shape=jax.ShapeDtypeStruct((m, ncols), out_dtype),
        scratch_shapes=[pltpu.VMEM((k, ncomp * tn), BF16), pltpu.VMEM((2, tm, ncomp * tn), F32),
                        pltpu.VMEM((ncomp, 1, tn), F32)],
        compiler_params=_params("parallel", "arbitrary"),
        name="matmul_conv_glu" if glu else "matmul_conv",
    )(*args)


def _gmlp_kernel(p_ref, gain_ref, ws_ref, bs_ref, o_ref, *, tm, width):
    p = p_ref[...]
    g = 0.5 * p * (1.0 + lax.erf(p * (1.0 / math.sqrt(2.0))))
    u = g[:, :width]
    v = g[:, width:]
    ms = jnp.mean(v * v, axis=-1, keepdims=True)
    vb = (v * lax.rsqrt(ms + EPS) * gain_ref[...]).astype(BF16)
    gd = width // A_GROUPS
    for c in range(tm // CHUNK):
        r0 = c * CHUNK
        for gi in range(A_GROUPS):
            c0 = gi * gd
            s = jnp.dot(ws_ref[gi], vb[r0:r0 + CHUNK, c0:c0 + gd], preferred_element_type=F32)
            s = s + bs_ref[:, c0:c0 + gd]
            o_ref[r0:r0 + CHUNK, c0:c0 + gd] = (u[r0:r0 + CHUNK, c0:c0 + gd] * s).astype(o_ref.dtype)


def spatial_gating(p, v_gain, w_s, b_s):
    m, two_w = p.shape
    width = two_w // 2
    gd = width // A_GROUPS
    tm = _tile(m, 512)
    bs_full = jnp.repeat(b_s.T, gd, axis=1)
    return pl.pallas_call(
        functools.partial(_gmlp_kernel, tm=tm, width=width),
        grid=(m // tm,),
        in_specs=[pl.BlockSpec((tm, two_w), lambda i: (i, 0)),
                  pl.BlockSpec((1, width), lambda i: (0, 0)),
                  pl.BlockSpec((A_GROUPS, CHUNK, CHUNK), lambda i: (0, 0, 0)),
                  pl.BlockSpec((CHUNK, width), lambda i: (0, 0))],
        out_specs=pl.BlockSpec((tm, width), lambda i: (i, 0)),
        out_shape=jax.ShapeDtypeStruct((m, width), BF16),
        compiler_params=_params("parallel"),
        name="spatial_gating",
    )(p, v_gain.reshape(1, width), w_s.astype(BF16), bs_full)


def _dft_matrices(L, rows=None):
    n = 2 * L
    r0 = L // 64
    rows = L if rows is None else rows
    assert rows % r0 == 0
    r1 = rows // r0
    t = np.arange(L, dtype=np.int64)[None, :]
    a1 = ((r0 * np.arange(r1, dtype=np.int64))[:, None] * t) % n
    a0 = (np.arange(r0, dtype=np.int64)[:, None] * t) % n
    ca, sa, cb, sb = lax.optimization_barrier((
        jnp.asarray(np.cos(2 * np.pi * a1 / n), F32)[:, None, :],
        jnp.asarray(np.sin(2 * np.pi * a1 / n), F32)[:, None, :],
        jnp.asarray(np.cos(2 * np.pi * a0 / n), F32)[None, :, :],
        jnp.asarray(np.sin(2 * np.pi * a0 / n), F32)[None, :, :]))
    cosm = (ca * cb - sa * sb).astype(BF16).reshape(rows, L)
    sinm = (-(sa * cb + ca * sb)).astype(BF16).reshape(rows, L)
    return cosm, sinm


def _filter_features(L):
    t = np.linspace(0.0, 1.0, L)[:, None]
    w = 2.0 * np.pi * np.arange(L)[:, None] / L
    bands = np.linspace(1e-4, HYENA_BANDS - 1, HYENA_BANDS)[None, :]
    z = np.concatenate([t, np.cos(w * bands), -np.sin(w * bands)], axis=-1)
    return jnp.asarray(np.pad(z, ((0, 0), (0, 128 - HYENA_EMB))), F32)


def _alternating(shape):
    rows = lax.broadcasted_iota(jnp.int32, shape, 0)
    return jnp.where(rows % 2 == 0, 1.0, -1.0)


def _twiddles(L):
    g = np.arange(L // 4)[:, None] * np.ones((1, LANES))
    a1, a2 = 2.0 * np.pi * g / L, 2.0 * np.pi * g / (2 * L)
    return jnp.asarray(np.stack([np.cos(a1), np.sin(a1), np.cos(a2), np.sin(a2)]), F32)


def _cadd(a, b):
    return a[0] + b[0], a[1] + b[1]


def _csub(a, b):
    return a[0] - b[0], a[1] - b[1]


def _conj(a):
    return a[0], -a[1]


def _cmul(a, b):
    return a[0] * b[0] - a[1] * b[1], a[0] * b[1] + a[1] * b[0]


def _rot(a, c, s):
    return a[0] * c + a[1] * s, a[1] * c - a[0] * s


def _unrot(a, c, s):
    return a[0] * c - a[1] * s, a[1] * c + a[0] * s


def _times_i(a):
    return -a[1], a[0]


def _times_minus_i(a):
    return a[1], -a[0]


def _lane_block(x, j):
    return x[:, j * LANES:(j + 1) * LANES]


def _period8_patterns():
    t = np.arange(8)[:, None] * np.ones((1, LANES))
    rows = [np.cos(np.pi * t)]
    for k in (2, 1, 3):
        rows += [np.cos(k * np.pi * t / 4), np.sin(k * np.pi * t / 4)]
    return jnp.asarray(np.round(np.stack(rows), 15), F32)


def _filter_spectrum_kernel(z_ref, w1_ref, b1_ref, w2_ref, b2_ref, fr_ref, w3f_ref, w3b_ref, dl_ref,
                            pat_ref, cos_ref, sin_ref, o_ref, rc_ref, rs_ref, nyq_ref, h_ref, taps_ref,
                            *, n):
    m = pl.program_id(2)
    hp = lax.Precision.HIGHEST

    @pl.when(jnp.logical_and(jnp.logical_and(pl.program_id(0) == 0, pl.program_id(1) == 0), m == 0))
    def _():
        h = jnp.dot(z_ref[...], w1_ref[...], precision=hp, preferred_element_type=F32) + b1_ref[...]
        h = jnp.sin(fr_ref[0:1, :] * h)
        h = jnp.dot(h, w2_ref[...], precision=hp, preferred_element_type=F32) + b2_ref[...]
        h_ref[...] = jnp.sin(fr_ref[1:2, :] * h)

    @pl.when(m == 0)
    def _():
        length = z_ref.shape[0]
        chunk = min(length, 2 * DFT_TILE)
        nrm = None
        tc = taps_ref.shape[2]
        w3 = jnp.concatenate([w3f_ref[...], w3b_ref[...]], axis=1)
        for r0 in range(0, length, chunk):
            rs = slice(r0, r0 + chunk)
            decay = jnp.exp(-z_ref[rs, 0:1] * dl_ref[...])
            hfb = jnp.dot(h_ref[rs, :], w3, precision=hp, preferred_element_type=F32)
            hf, hb = hfb[:, :tc] * decay, hfb[:, tc:] * decay
            if r0 == 0:
                hb = jnp.where(lax.broadcasted_iota(jnp.int32, hb.shape, 0) == 0, 0.0, hb)
            part = jnp.sum(jnp.abs(hf) + jnp.abs(hb), axis=0, keepdims=True)
            nrm = part if nrm is None else nrm + part
            taps_ref[0, rs, :] = hf + hb
            taps_ref[1, rs, :] = hf - hb
        inv = 1.0 / nrm
        packed = [None] * 4
        alt, hc, hs, qc, qs, q3c, q3s = (jnp.concatenate([pat_ref[i]] * (chunk // 8), axis=0)
                                         for i in range(7))
        for r0 in range(0, length, chunk):
            rs = slice(r0, r0 + chunk)
            ke = taps_ref[0, rs, :] * inv
            ko = taps_ref[1, rs, :] * inv
            for blk, (vc, vs) in enumerate(((ke, ko), (ke * alt, -ko * alt), (ke * hc, ke * hs),
                                            (ko * hs, ko * hc))):
                rc_ref[rs, blk * tc:(blk + 1) * tc] = vc.astype(BF16)
                rs_ref[rs, blk * tc:(blk + 1) * tc] = vs.astype(BF16)
            parts = (jnp.sum(ke * qc, axis=0, keepdims=True), -jnp.sum(ko * qs, axis=0, keepdims=True),
                     jnp.sum(ke * q3c, axis=0, keepdims=True), -jnp.sum(ko * q3s, axis=0, keepdims=True))
            packed = [pt if acc is None else acc + pt for acc, pt in zip(packed, parts)]
        for i in range(4):
            nyq_ref[i:i + 1, :] = packed[i] * (2.0 / n)

    tc = o_ref.shape[2]
    rc = jnp.dot(cos_ref[...], rc_ref[...], preferred_element_type=F32)
    rs = jnp.dot(sin_ref[...], rs_ref[...], preferred_element_type=F32)
    c = [rc[:, i * tc:(i + 1) * tc] for i in range(4)]
    s = [rs[:, i * tc:(i + 1) * tc] for i in range(4)]
    planes = [c[0], s[0], c[1], s[1], c[2] - s[2], -c[3] - s[3], c[2] + s[2], s[3] - c[3]]
    first = jnp.logical_and(lax.broadcasted_iota(jnp.int32, c[0].shape, 0) == 0, m == 0)
    scale = jnp.where(first, 1.0 / n, 2.0 / n)
    slots = {1: 2, 3: 3, 6: 0, 7: 1}
    for i in range(8):
        val = planes[i] * scale
        if i in slots:
            val = jnp.where(first, nyq_ref[slots[i]:slots[i] + 1, :], val)
        o_ref[i] = val


def hyena_filter_spectrum(L, width, cosm, sinm, w1, b1, w2, b2, w3, freq):
    z = _filter_features(L)
    tc = LANES
    ncb = width // tc
    max_decay = math.log(HYENA_TARGET) / HYENA_FAST
    min_decay = math.log(HYENA_TARGET) / HYENA_SLOW
    deltas = jnp.asarray(np.abs(np.linspace(min_decay, max_decay, width)), F32).reshape(1, width)
    w1p = jnp.pad(w1, ((0, 128 - HYENA_EMB), (0, 0)))
    const = lambda o, c, m: (0, 0)
    tm = _tile(L // 4, DFT_TILE)
    return pl.pallas_call(
        functools.partial(_filter_spectrum_kernel, n=2 * L),
        grid=(HYENA_ORDER, ncb, L // 4 // tm),
        in_specs=[pl.BlockSpec((L, 128), const),
                  pl.BlockSpec((128, HYENA_FFN), const), pl.BlockSpec((1, HYENA_FFN), const),
                  pl.BlockSpec((HYENA_FFN, HYENA_FFN), const), pl.BlockSpec((1, HYENA_FFN), const),
                  pl.BlockSpec((2, HYENA_FFN), const),
                  pl.BlockSpec((HYENA_FFN, tc), lambda o, c, m: (0, (2 * o) * ncb + c)),
                  pl.BlockSpec((HYENA_FFN, tc), lambda o, c, m: (0, (2 * o + 1) * ncb + c)),
                  pl.BlockSpec((1, tc), lambda o, c, m: (0, c)),
                  pl.BlockSpec((7, 8, LANES), lambda o, c, m: (0, 0, 0)),
                  pl.BlockSpec((tm, L), lambda o, c, m: (m, 0)),
                  pl.BlockSpec((tm, L), lambda o, c, m: (m, 0))],
        out_specs=pl.BlockSpec((None, 8, tm, tc), lambda o, c, m: (o, 0, m, c)),
        out_shape=jax.ShapeDtypeStruct((HYENA_ORDER, 8, L // 4, width), F32),
        scratch_shapes=[pltpu.VMEM((L, 4 * tc), BF16), pltpu.VMEM((L, 4 * tc), BF16),
                        pltpu.VMEM((4, tc), F32), pltpu.VMEM((L, HYENA_FFN), F32),
                        pltpu.VMEM((2, L, tc), F32)],
        compiler_params=_params("arbitrary", "arbitrary", "arbitrary"),
        name="hyena_filter_spectrum",
    )(z, w1p, b1.reshape(1, -1), w2, b2.reshape(1, -1), freq, w3, w3, deltas, _period8_patterns(),
      cosm, sinm)


def _dft_forward_kernel(u_ref, cos_ref, sin_ref, tw_ref, kf_ref, o_ref, ub_ref):
    q = ub_ref.shape[0]
    nyq = []
    for j in range(4):
        xj = u_ref[pl.ds(j, q, stride=4), :]
        ub_ref[:, j * LANES:(j + 1) * LANES] = xj.astype(BF16)
        nyq.append(jnp.sum(xj * _alternating(xj.shape), axis=0, keepdims=True))
    eq, oq = (nyq[0], -nyq[2]), (nyq[1], -nyq[3])
    woq = _rot(oq, math.sqrt(0.5), math.sqrt(0.5))
    u_q, u_3q = _cadd(eq, woq), _conj(_csub(eq, woq))
    chunk = min(q, DFT_TILE)
    for r0 in range(0, q, chunk):
        rs = slice(r0, r0 + chunk)
        re = jnp.dot(cos_ref[rs, :], ub_ref[...], preferred_element_type=F32)
        im = jnp.dot(sin_ref[rs, :], ub_ref[...], preferred_element_type=F32)
        x = [(_lane_block(re, j), _lane_block(im, j)) for j in range(4)]
        c1, s1, c2, s2 = tw_ref[0, rs, :], tw_ref[1, rs, :], tw_ref[2, rs, :], tw_ref[3, rs, :]
        t2, t3 = _rot(x[2], c1, s1), _rot(x[3], c1, s1)
        ea, eb = _cadd(x[0], t2), _csub(x[0], t2)
        oa, ob = _cadd(x[1], t3), _csub(x[1], t3)
        ta, tb = _rot(oa, c2, s2), _times_i(_rot(ob, c2, s2))
        p = [_cadd(ea, ta), _conj(_csub(ea, ta)), _conj(_cadd(eb, tb)), _csub(eb, tb)]
        k = [(kf_ref[2 * i, rs, :], kf_ref[2 * i + 1, rs, :]) for i in range(4)]
        if r0 == 0:
            first = lax.broadcasted_iota(jnp.int32, re.shape[:1] + (LANES,), 0) == 0
            p[3] = (jnp.where(first, u_q[0], p[3][0]), jnp.where(first, u_q[1], p[3][1]))
        y = [_cmul(p[i], k[i]) for i in range(4)]
        if r0 == 0:
            y3q = _cmul(u_3q, (k[0][1], k[1][1]))
            y[0] = (y[0][0], jnp.where(first, y3q[0], y[0][1]))
            y[1] = (y[1][0], jnp.where(first, y3q[1], y[1][1]))
        for i in range(4):
            o_ref[2 * i, rs, :] = y[i][0].astype(o_ref.dtype)
            o_ref[2 * i + 1, rs, :] = y[i][1].astype(o_ref.dtype)


def _resident(shape):
    return pl.BlockSpec(shape, lambda c, b: (0,) * len(shape), pipeline_mode=pl.Buffered(1))


def dft_forward_times_filter(u, col0, width, cos4, sin4, tw, kf, order):
    bsz, L, _ = u.shape
    q = L // 4
    assert col0 % LANES == 0
    cb0 = col0 // LANES
    return pl.pallas_call(
        _dft_forward_kernel,
        grid=(width // LANES, bsz),
        in_specs=[pl.BlockSpec((None, L, LANES), lambda c, b: (b, 0, cb0 + c)),
                  _resident((q, q)), _resident((q, q)), _resident((4, q, LANES)),
                  pl.BlockSpec((None, 8, q, LANES), lambda c, b: (order, 0, 0, c))],
        out_specs=pl.BlockSpec((None, 8, q, LANES), lambda c, b: (b, 0, 0, c)),
        out_shape=jax.ShapeDtypeStruct((bsz, 8, q, width), BF16),
        scratch_shapes=[pltpu.VMEM((q, 4 * LANES), BF16)],
        compiler_params=_params("parallel", "parallel"),
        name="dft_forward",
    )(u, cos4, sin4, tw, kf)


def _dft_inverse_kernel(y_ref, cos_ref, sin_ref, tw_ref, u_ref, g_ref, skip_ref, o_ref, x_ref, res_ref):
    q = x_ref.shape[1]

    def row0(i):
        return y_ref[i, 0:1, :].astype(F32)

    u_q, u_3q = (row0(6), row0(7)), (row0(1), row0(3))
    eq = _cadd(u_q, _conj(u_3q))
    oq = _unrot(_csub(u_q, _conj(u_3q)), math.sqrt(0.5), math.sqrt(0.5))
    nyq = jnp.concatenate([eq[0], oq[0], -eq[1], -oq[1]], axis=1)
    chunk = min(q, DFT_TILE)
    for r0 in range(0, q, chunk):
        rs = slice(r0, r0 + chunk)
        p = [(y_ref[2 * i, rs, :].astype(F32), y_ref[2 * i + 1, rs, :].astype(F32)) for i in range(4)]
        if r0 == 0:
            first = lax.broadcasted_iota(jnp.int32, p[0][0].shape, 0) == 0
            p[0] = (p[0][0], jnp.where(first, 0.0, p[0][1]))
            p[1] = (p[1][0], jnp.where(first, 0.0, p[1][1]))
            p[3] = (jnp.where(first, p[2][0], p[3][0]), jnp.where(first, p[2][1], p[3][1]))
        c1, s1, c2, s2 = tw_ref[0, rs, :], tw_ref[1, rs, :], tw_ref[2, rs, :], tw_ref[3, rs, :]
        ea, ta = _cadd(p[0], _conj(p[1])), _csub(p[0], _conj(p[1]))
        eb, tb = _cadd(_conj(p[2]), p[3]), _csub(_conj(p[2]), p[3])
        oa = _unrot(ta, c2, s2)
        ob = _times_minus_i(_unrot(tb, c2, s2))
        x = [_cadd(ea, eb), _cadd(oa, ob), _unrot(_csub(ea, eb), c1, s1), _unrot(_csub(oa, ob), c1, s1)]
        for j in range(4):
            x_ref[0, rs, j * LANES:(j + 1) * LANES] = x[j][0].astype(BF16)
            x_ref[1, rs, j * LANES:(j + 1) * LANES] = x[j][1].astype(BF16)
    skip = skip_ref[...]
    alt = _alternating((chunk, 4 * LANES))
    for r0 in range(0, q, chunk):
        rs = slice(r0, r0 + chunk)
        conv = jnp.dot(cos_ref[rs, :], x_ref[0], preferred_element_type=F32)
        conv = conv + jnp.dot(sin_ref[rs, :], x_ref[1], preferred_element_type=F32)
        conv = conv + alt * nyq
        for j in range(4):
            rows = pl.ds(4 * r0 + j, chunk, stride=4)
            res_ref[rows, :] = g_ref[rows, :] * (_lane_block(conv, j) + u_ref[rows, :] * skip)
    o_ref[...] = res_ref[...].astype(o_ref.dtype)


def dft_inverse_gated(y, cos4, sin4, tw, u, ucol0, gate, gcol0, skip, out_dtype):
    bsz, _, q, width = y.shape
    L = 4 * q
    ub0, gb0 = ucol0 // LANES, gcol0 // LANES
    return pl.pallas_call(
        _dft_inverse_kernel,
        grid=(width // LANES, bsz),
        in_specs=[pl.BlockSpec((None, 8, q, LANES), lambda c, b: (b, 0, 0, c)),
                  _resident((q, q)), _resident((q, q)), _resident((4, q, LANES)),
                  pl.BlockSpec((None, L, LANES), lambda c, b: (b, 0, ub0 + c)),
                  pl.BlockSpec((None, L, LANES), lambda c, b: (b, 0, gb0 + c)),
                  pl.BlockSpec((1, LANES), lambda c, b: (0, c))],
        out_specs=pl.BlockSpec((None, L, LANES), lambda c, b: (b, 0, c)),
        out_shape=jax.ShapeDtypeStruct((bsz, L, width), out_dtype),
        scratch_shapes=[pltpu.VMEM((2, q, 4 * LANES), BF16), pltpu.VMEM((L, LANES), F32)],
        compiler_params=_params("parallel", "parallel"),
        name="dft_inverse",
    )(y, cos4, sin4, tw, u, gate, skip.reshape(1, width))


NA_GROUP = 8
NA_BLOCK = NA_GROUP + NA_KH
NA_PAD = NA_KH // 2


def _na_plans():
    tiles = {}
    plans = []
    for variant in range(3):
        plan = []
        for rq in range(NA_GROUP):
            lo = (max(rq, NA_PAD), rq, min(rq, NA_PAD))[variant]
            a_lo, a_hi = lo // 2, (lo + NA_KH - 1) // 2 + 1
            ids = []
            for a in range(a_lo, a_hi):
                key = tuple(kr - rq + NA_KH - 1 - NA_PAD if lo <= kr < lo + NA_KH else -1
                            for kr in (2 * a, 2 * a + 1))
                ids.append(tiles.setdefault(key, len(tiles)))
            plan.append((a_lo, tuple(ids)))
        plans.append(tuple(plan))
    return tuple(plans), list(tiles)


def _na_kernel(q_ref, k_ref, v_ref, tiles_ref, o_ref, kb_ref, vb_ref, p_ref, *, rows, scale, plans):
    L = rows * GRID_W
    pad = NA_PAD * GRID_W
    gq = NA_GROUP * GRID_W
    gk = NA_BLOCK * GRID_W
    for ref, src in ((kb_ref, k_ref), (vb_ref, v_ref)):
        ref[0:pad, :] = jnp.zeros((pad, HEAD_DIM), BF16)
        ref[pad + L:, :] = jnp.zeros((pad, HEAD_DIM), BF16)
        ref[pad:pad + L, :] = src[...].astype(BF16)

    def group(q0, plan, slot):
        q = q_ref[pl.ds(q0, gq), :].astype(BF16)
        s = lax.dot_general(q, kb_ref[pl.ds(q0, gk), :], (((1,), (1,)), ((), ())),
                            preferred_element_type=F32)
        p_ref[slot] = jnp.zeros(p_ref.shape[1:], BF16)
        for rq, (a_lo, ids) in enumerate(plan):
            r0, c0, c1 = rq * GRID_W, 2 * GRID_W * a_lo, 2 * GRID_W * (a_lo + len(ids))
            bias = jnp.concatenate([tiles_ref[t] for t in ids], axis=1)
            sl = s[r0:r0 + GRID_W, c0:c1] * scale + bias
            e = jnp.exp(sl - jnp.max(sl, axis=-1, keepdims=True))
            pr = e * (1.0 / jnp.sum(e, axis=-1, keepdims=True))
            p_ref[slot, r0:r0 + GRID_W, c0:c1] = pr.astype(BF16)
        o = jnp.dot(p_ref[slot], vb_ref[pl.ds(q0, gk), :], preferred_element_type=F32)
        o_ref[pl.ds(q0, gq), :] = o.astype(o_ref.dtype)

    ngroups = rows // NA_GROUP
    nbuf = p_ref.shape[0]
    for g in range(ngroups):
        plan = plans[0] if g == 0 else plans[2] if g == ngroups - 1 else plans[1]
        group(g * gq, plan, g % nbuf)


def _na_bias_tiles(rpb, tile_keys):
    col = np.arange(GRID_W)
    cs = np.clip(col - NA_KW // 2, 0, GRID_W - NA_KW)
    kc = np.arange(GRID_W)
    inwin = (kc[None, :] >= cs[:, None]) & (kc[None, :] < cs[:, None] + NA_KW)
    rel_col = np.clip(kc[None, :] - col[:, None] + (NA_KW - 1), 0, 2 * NA_KW - 2)
    t = jnp.where(jnp.asarray(inwin)[None, None], rpb[:, :, rel_col], MASK_VALUE)
    masked = jnp.full((rpb.shape[0], 1, GRID_W, GRID_W), MASK_VALUE, F32)
    t = jnp.concatenate([t.astype(F32), masked], axis=1)
    left = np.array([k[0] for k in tile_keys])
    right = np.array([k[1] for k in tile_keys])
    return jnp.concatenate([t[:, left], t[:, right]], axis=-1)


def neighborhood_attention(p, bsz, L, heads, rpb):
    rows = L // GRID_W
    assert rows % NA_GROUP == 0 and rows >= 2 * NA_GROUP
    plans, tile_keys = _na_plans()
    tiles = _na_bias_tiles(rpb, tile_keys)
    nt = len(tile_keys)
    lp = L + 2 * NA_PAD * GRID_W
    return pl.pallas_call(
        functools.partial(_na_kernel, rows=rows, scale=HEAD_DIM ** -0.5, plans=plans),
        grid=(bsz, heads),
        in_specs=[pl.BlockSpec((None, L, HEAD_DIM), lambda b, h: (b, 0, h)),
                  pl.BlockSpec((None, L, HEAD_DIM), lambda b, h: (b, 0, heads + h)),
                  pl.BlockSpec((None, L, HEAD_DIM), lambda b, h: (b, 0, 2 * heads + h)),
                  pl.BlockSpec((None, nt, GRID_W, 2 * GRID_W), lambda b, h: (h, 0, 0, 0))],
        out_specs=pl.BlockSpec((None, L, HEAD_DIM), lambda b, h: (b, 0, h)),
        out_shape=jax.ShapeDtypeStruct((bsz, L, heads * HEAD_DIM), BF16),
        scratch_shapes=[pltpu.VMEM((lp, HEAD_DIM), BF16), pltpu.VMEM((lp, HEAD_DIM), BF16),
                        pltpu.VMEM((4, NA_GROUP * GRID_W, NA_BLOCK * GRID_W), BF16)],
        compiler_params=_params("parallel", "parallel"),
        name="neighborhood_attention",
    )(p, p, p, tiles)


DIFF_SUB_ROWS = 256


def _rope(x, cos, sin_signed, half):
    lane = lax.broadcasted_iota(jnp.int32, x.shape, 1)
    width = x.shape[1]
    first = (lane % (2 * half)) < half
    partner = jnp.where(first, pltpu.roll(x, width - half, 1), pltpu.roll(x, half, 1))
    return x * cos + partner * sin_signed


def _diff_attn_kernel(q_ref, k_ref, v_ref, cq_ref, sq_ref, ck_ref, sk_ref, lam_ref, sub_ref, o_ref,
                      kb_ref, vb_ref, *, lam_init, scale, half, n_sub):
    qi = pl.program_id(2)

    @pl.when(qi == 0)
    def _():
        kb_ref[...] = _rope(k_ref[...], ck_ref[...], sk_ref[...], half).astype(BF16)
        vb_ref[...] = v_ref[...].astype(BF16)

    lp = lam_ref[...]
    lam = (jnp.exp(jnp.sum(lp[0:1] * lp[1:2], axis=-1, keepdims=True))
           - jnp.exp(jnp.sum(lp[2:3] * lp[3:4], axis=-1, keepdims=True)) + lam_init)
    kb = kb_ref[...]
    tq = q_ref.shape[0]
    sub = tq // n_sub
    for sb in range(n_sub):
        rs = slice(sb * sub, (sb + 1) * sub)
        q = _rope(q_ref[rs, :], cq_ref[rs, :], sq_ref[rs, :], half) * (scale * math.log2(math.e))
        lane = lax.broadcasted_iota(jnp.int32, q.shape, 1)

        def softmax_map(in_map, weight):
            qm = jnp.where(in_map, q, 0.0).astype(BF16)
            s = lax.dot_general(qm, kb, (((1,), (1,)), ((), ())), preferred_element_type=F32)
            e = jnp.exp2(s - jnp.max(s, axis=-1, keepdims=True))
            return e * (weight / jnp.sum(e, axis=-1, keepdims=True))

        a = softmax_map(lane < 2 * half, 1.0) - softmax_map(lane >= 2 * half, lam)
        o = jnp.dot(a.astype(BF16), vb_ref[...], preferred_element_type=F32)
        ms = jnp.mean(o * o, axis=-1, keepdims=True)
        o_ref[rs, :] = (o * lax.rsqrt(ms + EPS) * sub_ref[...] * (1.0 - lam_init)).astype(o_ref.dtype)


def _rope_tables(L, half):
    inv = (1.0 / (np.float32(ROPE_THETA) ** (np.arange(half, dtype=np.float32) * np.float32(2.0)
                                             / np.float32(2 * half)))).astype(np.float32)
    ang = (np.arange(L, dtype=np.float32)[:, None] * inv[None, :]).astype(np.float64)
    cos, sin = np.cos(ang), np.sin(ang)
    reps = HEAD_DIM // (2 * half)
    cos_t = np.tile(cos, (1, 2 * reps))
    sin_t = np.tile(np.concatenate([-sin, sin], axis=1), (1, reps))
    return jnp.asarray(cos_t, F32), jnp.asarray(sin_t, F32)


def diff_attention(p, col0, bsz, L, heads, lam_params, subln, lam_init):
    half = HEAD_DIM // 4
    assert col0 % HEAD_DIM == 0
    c0 = col0 // HEAD_DIM
    tq = _tile(L, 2 * DIFF_SUB_ROWS)
    n_sub = tq // DIFF_SUB_ROWS
    cos_t, sin_t = _rope_tables(L, half)
    return pl.pallas_call(
        functools.partial(_diff_attn_kernel, lam_init=lam_init, scale=(2 * half) ** -0.5, half=half,
                          n_sub=n_sub),
        grid=(bsz, heads, L // tq),
        in_specs=[pl.BlockSpec((None, tq, HEAD_DIM), lambda b, h, i: (b, i, c0 + h)),
                  pl.BlockSpec((None, L, HEAD_DIM), lambda b, h, i: (b, 0, c0 + heads + h)),
                  pl.BlockSpec((None, L, HEAD_DIM), lambda b, h, i: (b, 0, c0 + 2 * heads + h)),
                  pl.BlockSpec((tq, HEAD_DIM), lambda b, h, i: (i, 0)),
                  pl.BlockSpec((tq, HEAD_DIM), lambda b, h, i: (i, 0)),
                  pl.BlockSpec((L, HEAD_DIM), lambda b, h, i: (0, 0)),
                  pl.BlockSpec((L, HEAD_DIM), lambda b, h, i: (0, 0)),
                  pl.BlockSpec((4, 2 * half), lambda b, h, i: (0, 0)),
                  pl.BlockSpec((1, HEAD_DIM), lambda b, h, i: (0, 0))],
        out_specs=pl.BlockSpec((None, tq, HEAD_DIM), lambda b, h, i: (b, i, h)),
        out_shape=jax.ShapeDtypeStruct((bsz, L, heads * HEAD_DIM), BF16),
        scratch_shapes=[pltpu.VMEM((L, HEAD_DIM), BF16), pltpu.VMEM((L, HEAD_DIM), BF16)],
        compiler_params=_params("parallel", "parallel", "arbitrary"),
        name="diff_attention",
    )(p, p, p, cos_t, sin_t, cos_t, sin_t, lam_params, subln.reshape(1, HEAD_DIM))


def kernel(x, norm_mix, norm_ffn, w_out, ffn_up, ffn_conv_w, ffn_conv_b, ffn_down, final_norm,
           ab_w_in, a_vnorm, a_ws, a_bs, b_conv_w, b_conv_b, b_filt_w1, b_filt_b1, b_filt_w2,
           b_filt_b2, b_filt_w3, b_filt_freq, b_skip, cd_w_in, c_rpb, d_lambda, d_subln):
    bsz, L, d = x.shape
    depth = norm_mix.shape[0]
    m = bsz * L
    half_w = d // 2
    ff = ffn_down.shape[1]
    heads = half_w // HEAD_DIM
    assert heads == D_HEADS and half_w // A_GROUPS == CHUNK and L % GRID_W == 0
    xs = x.reshape(m, d)
    cosm, sinm = _dft_matrices(L, rows=L // 4)
    cos4, sin4 = _dft_matrices(L // 4)
    tw = _twiddles(L)
    w_out_b = w_out.astype(BF16)
    ffn_down_b = ffn_down.astype(BF16)
    h = rmsnorm(xs, norm_mix[0], BF16)
    for l in range(depth):
        i = l // 2
        if l % 2 == 0:
            pa = matmul(h, ab_w_in, i, 0, 2 * half_w, F32)
            ya = spatial_gating(pa, a_vnorm[i], a_ws[i], a_bs[i])
            pb = matmul_conv(h, ab_w_in, b_conv_w, b_conv_b, i, L, 2 * half_w, 3 * half_w, False, F32)
            pb = pb.reshape(bsz, L, 3 * half_w)
            kf = hyena_filter_spectrum(L, half_w, cosm, sinm, b_filt_w1[i], b_filt_b1[i], b_filt_w2[i],
                                       b_filt_b2[i], b_filt_w3[i], b_filt_freq[i])
            y1 = dft_forward_times_filter(pb, 0, half_w, cos4, sin4, tw, kf, 0)
            z = dft_inverse_gated(y1, cos4, sin4, tw, pb, 0, pb, half_w, b_skip[i, 0], F32)
            y2 = dft_forward_times_filter(z, 0, half_w, cos4, sin4, tw, kf, 1)
            yb = dft_inverse_gated(y2, cos4, sin4, tw, z, 0, pb, 2 * half_w, b_skip[i, 1], BF16)
            yb = yb.reshape(m, half_w)
        else:
            p = matmul(h, cd_w_in, i, 0, 6 * half_w, F32).reshape(bsz, L, 6 * half_w)
            lam_init = 0.8 - 0.6 * math.exp(-0.3 * l)
            ya = neighborhood_attention(p, bsz, L, heads, c_rpb[i]).reshape(m, half_w)
            yb = diff_attention(p, 3 * half_w, bsz, L, heads, d_lambda[i], d_subln[i], lam_init)
            yb = yb.reshape(m, half_w)
        xs, hf = outproj_norm(ya, yb, w_out_b, l, xs, norm_ffn[l])
        act = matmul_conv(hf, ffn_up, ffn_conv_w, ffn_conv_b, l, L, 0, ff, True, BF16)
        last = l == depth - 1
        xs, h = downproj_norm(act, ffn_down_b, l, xs, final_norm if last else norm_mix[l + 1],
                              F32 if last else BF16)
    return h.reshape(bsz, L, d)
```

```python
import functools
import math

import numpy as np
import jax
import jax.numpy as jnp
from jax import lax
from jax.experimental import pallas as pl
from jax.experimental.pallas import tpu as pltpu

F32 = jnp.float32
BF16 = jnp.bfloat16

EPS = 1e-6
GRID_W = 64
HEAD_DIM = 128
CHUNK = 128
A_GROUPS = 8
NA_KH = 8
NA_KW = 16
D_HEADS = 8
ROPE_THETA = 10000.0
HYENA_ORDER = 2
HYENA_EMB = 33
HYENA_BANDS = (HYENA_EMB - 1) // 2
HYENA_FFN = 64
HYENA_TARGET = 1e-2
HYENA_FAST = 0.3
HYENA_SLOW = 1.5
MASK_VALUE = -1e30

VMEM_LIMIT_BYTES = 56 * 1024 * 1024
DFT_TILE = 512
LANES = 128


def _params(*sem):
    return pltpu.CompilerParams(dimension_semantics=sem, vmem_limit_bytes=VMEM_LIMIT_BYTES)


def _tile(n, pref):
    t = min(n, pref)
    assert n % t == 0, (n, pref)
    return t


def _rmsnorm_kernel(x_ref, g_ref, o_ref):
    x = x_ref[...]
    ms = jnp.mean(x * x, axis=-1, keepdims=True)
    o_ref[...] = (x * lax.rsqrt(ms + EPS) * g_ref[...]).astype(o_ref.dtype)


def rmsnorm(x, g, out_dtype):
    m, d = x.shape
    tm = _tile(m, 512)
    return pl.pallas_call(
        _rmsnorm_kernel,
        grid=(m // tm,),
        in_specs=[pl.BlockSpec((tm, d), lambda i: (i, 0)), pl.BlockSpec((1, d), lambda i: (0, 0))],
        out_specs=pl.BlockSpec((tm, d), lambda i: (i, 0)),
        out_shape=jax.ShapeDtypeStruct((m, d), out_dtype),
        compiler_params=_params("parallel"),
        name="rmsnorm",
    )(x, g.reshape(1, d))


def _matmul_kernel(x_ref, w_ref, o_ref, wb_ref):
    @pl.when(pl.program_id(1) == 0)
    def _():
        wb_ref[...] = w_ref[...].astype(BF16)

    o_ref[...] = jnp.dot(x_ref[...], wb_ref[...], preferred_element_type=F32).astype(o_ref.dtype)


def matmul(x, w, layer, col0, ncols, out_dtype):
    m, k = x.shape
    tm = _tile(m, 1024)
    tn = _tile(ncols, 1024)
    assert col0 % tn == 0
    jb = col0 // tn
    return pl.pallas_call(
        _matmul_kernel,
        grid=(ncols // tn, m // tm),
        in_specs=[pl.BlockSpec((tm, k), lambda j, i: (i, 0)),
                  pl.BlockSpec((None, k, tn), lambda j, i: (layer, 0, jb + j))],
        out_specs=pl.BlockSpec((tm, tn), lambda j, i: (i, j)),
        out_shape=jax.ShapeDtypeStruct((m, ncols), out_dtype),
        scratch_shapes=[pltpu.VMEM((k, tn), BF16)],
        compiler_params=_params("parallel", "arbitrary"),
        name="matmul",
    )(x, w)


def _outproj_norm_kernel(xa_ref, wa_ref, xb_ref, wb_ref, res_ref, g_ref, o_ref, h_ref):
    acc = res_ref[...] + jnp.dot(xa_ref[...], wa_ref[...], preferred_element_type=F32)
    acc = acc + jnp.dot(xb_ref[...], wb_ref[...], preferred_element_type=F32)
    o_ref[...] = acc
    ms = jnp.mean(acc * acc, axis=-1, keepdims=True)
    h_ref[...] = (acc * lax.rsqrt(ms + EPS) * g_ref[...]).astype(h_ref.dtype)


def outproj_norm(xa, xb, w, layer, res, gain):
    m, n = res.shape
    k = xa.shape[1]
    tm = _tile(m, 512)
    return pl.pallas_call(
        _outproj_norm_kernel,
        grid=(m // tm,),
        in_specs=[pl.BlockSpec((tm, k), lambda i: (i, 0)),
                  pl.BlockSpec((None, k, n), lambda i: (layer, 0, 0)),
                  pl.BlockSpec((tm, k), lambda i: (i, 0)),
                  pl.BlockSpec((None, k, n), lambda i: (layer, 1, 0)),
                  pl.BlockSpec((tm, n), lambda i: (i, 0)),
                  pl.BlockSpec((1, n), lambda i: (0, 0))],
        out_specs=[pl.BlockSpec((tm, n), lambda i: (i, 0)), pl.BlockSpec((tm, n), lambda i: (i, 0))],
        out_shape=[jax.ShapeDtypeStruct((m, n), F32), jax.ShapeDtypeStruct((m, n), BF16)],
        compiler_params=_params("parallel"),
        name="outproj_norm",
    )(xa, w, xb, w, res, gain.reshape(1, n))


def _downproj_norm_kernel(x_ref, w_ref, res_ref, g_ref, o_ref, h_ref):
    acc = res_ref[...] + jnp.dot(x_ref[...], w_ref[...], preferred_element_type=F32)
    o_ref[...] = acc
    ms = jnp.mean(acc * acc, axis=-1, keepdims=True)
    h_ref[...] = (acc * lax.rsqrt(ms + EPS) * g_ref[...]).astype(h_ref.dtype)


def downproj_norm(x, w, layer, res, gain, norm_dtype):
    m, n = res.shape
    k = x.shape[1]
    tm = _tile(m, 256)
    return pl.pallas_call(
        _downproj_norm_kernel,
        grid=(m // tm,),
        in_specs=[pl.BlockSpec((tm, k), lambda i: (i, 0)),
                  pl.BlockSpec((None, k, n), lambda i: (layer, 0, 0), pipeline_mode=pl.Buffered(1)),
                  pl.BlockSpec((tm, n), lambda i: (i, 0)),
                  pl.BlockSpec((1, n), lambda i: (0, 0))],
        out_specs=[pl.BlockSpec((tm, n), lambda i: (i, 0)), pl.BlockSpec((tm, n), lambda i: (i, 0))],
        out_shape=[jax.ShapeDtypeStruct((m, n), F32), jax.ShapeDtypeStruct((m, n), norm_dtype)],
        compiler_params=_params("parallel"),
        name="downproj_norm",
    )(x, w, res, gain.reshape(1, n))


def _matmul_conv_kernel(*refs, glu, tm, nm, seq_tiles, layer, woffs, nj):
    ncomp = 2 if glu else 1
    x_ref, w_hbm = refs[0], refs[1]
    cw_refs = refs[2:2 + ncomp]
    b_refs = refs[2 + ncomp:2 + 2 * ncomp]
    o_ref = refs[2 + 2 * ncomp]
    wb_ref, acc_ref, carry_ref, wf_ref, sem = refs[3 + 2 * ncomp:8 + 2 * ncomp]
    j = pl.program_id(0)
    i = pl.program_id(1)
    slot = i % 2
    ps = 1 - slot

    tn = o_ref.shape[1]

    def weight_copy(jj, c):
        ws = jj % 2
        col = pl.multiple_of((woffs[c] + jj) * tn, tn)
        return pltpu.make_async_copy(w_hbm.at[layer, :, pl.ds(col, tn)], wf_ref.at[ws, c], sem.at[ws, c])

    @pl.when(i == 0)
    def _():
        @pl.when(j == 0)
        def _():
            for c in range(ncomp):
                weight_copy(j, c).start()

        @pl.when(j + 1 < nj)
        def _():
            for c in range(ncomp):
                weight_copy(j + 1, c).start()

        for c in range(ncomp):
            weight_copy(j, c).wait()
            wb_ref[:, c * tn:(c + 1) * tn] = wf_ref[j % 2, c].astype(BF16)
        carry_ref[...] = jnp.zeros_like(carry_ref)
        acc_ref[1] = jnp.zeros(acc_ref.shape[1:], F32)

    def finish_previous(tile_follows):
        has_prev = (i + seq_tiles - 1) % seq_tiles != 0
        has_next = i % seq_tiles != 0
        outs = []
        for c in range(ncomp):
            a = acc_ref[ps, :, c * tn:(c + 1) * tn]
            rows = lax.broadcasted_iota(jnp.int32, a.shape, 0)
            prev_row = jnp.where(has_prev, carry_ref[c], 0.0)
            if tile_follows:
                next_row = jnp.where(has_next, acc_ref[slot, 0:1, c * tn:(c + 1) * tn], 0.0)
            else:
                next_row = jnp.zeros((1, tn), F32)
            up = jnp.where(rows == 0, prev_row, pltpu.roll(a, 1, 0))
            dn = jnp.where(rows == tm - 1, next_row, pltpu.roll(a, tm - 1, 0))
            cw = cw_refs[c]
            outs.append(up * cw[0:1, :] + a * cw[1:2, :] + dn * cw[2:3, :] + b_refs[c][...])
            carry_ref[c] = a[tm - 1:tm, :]
        if glu:
            g, val = outs
            res = g * (1.0 / (1.0 + jnp.exp(-g))) * val
        else:
            res = outs[0]
        o_ref[...] = res.astype(o_ref.dtype)

    @pl.when(i < nm)
    def _():
        acc_ref[slot] = jnp.dot(x_ref[...], wb_ref[...], preferred_element_type=F32)
        finish_previous(True)

    @pl.when(i == nm)
    def _():
        finish_previous(False)


def matmul_conv(x, w, cw, cb, layer, seq_len, col0, ncols, glu, out_dtype):
    m, k = x.shape
    tm = _tile(seq_len, 1024)
    tn = _tile(ncols, 512 if glu else 1024)
    nm = m // tm
    seq_tiles = seq_len // tm
    assert col0 % tn == 0
    ncomp = 2 if glu else 1
    coffs = [c * (ncols // tn) for c in range(ncomp)]
    woffs = [col0 // tn + o for o in coffs]
    cb3 = cb.reshape(cb.shape[0], 1, cb.shape[1])
    in_specs = [pl.BlockSpec((tm, k), lambda j, i: (jnp.minimum(i, nm - 1), 0))]
    in_specs.append(pl.BlockSpec(memory_space=pl.ANY))
    in_specs += [pl.BlockSpec((None, 3, tn), lambda j, i, o=o: (layer, 0, o + j)) for o in coffs]
    in_specs += [pl.BlockSpec((None, 1, tn), lambda j, i, o=o: (layer, 0, o + j)) for o in coffs]
    args = [x, w] + [cw] * ncomp + [cb3] * ncomp
    nj = ncols // tn
    return pl.pallas_call(
        functools.partial(_matmul_conv_kernel, glu=glu, tm=tm, nm=nm, seq_tiles=seq_tiles, layer=layer,
                          woffs=tuple(woffs), nj=nj),
        grid=(nj, nm + 1),
        in_specs=in_specs,
        out_specs=pl.BlockSpec((tm, tn), lambda j, i: (jnp.maximum(i - 1, 0), j)),
        out_shape=jax.ShapeDtypeStruct((m, ncols), out_dtype),
        scratch_shapes=[pltpu.VMEM((k, ncomp * tn), BF16), pltpu.VMEM((2, tm, ncomp * tn), F32),
                        pltpu.VMEM((ncomp, 1, tn), F32), pltpu.VMEM((2, ncomp, k, tn), F32),
                        pltpu.SemaphoreType.DMA((2, ncomp))],
        compiler_params=_params("arbitrary", "arbitrary"),
        name="matmul_conv_glu" if glu else "matmul_conv",
    )(*args)


def _gmlp_kernel(p_ref, gain_ref, ws_ref, bs_ref, o_ref, *, tm, width):
    p = p_ref[...]
    g = 0.5 * p * (1.0 + lax.erf(p * (1.0 / math.sqrt(2.0))))
    u = g[:, :width]
    v = g[:, width:]
    ms = jnp.mean(v * v, axis=-1, keepdims=True)
    vb = (v * lax.rsqrt(ms + EPS) * gain_ref[...]).astype(BF16)
    gd = width // A_GROUPS
    for c in range(tm // CHUNK):
        r0 = c * CHUNK
        for gi in range(A_GROUPS):
            c0 = gi * gd
            s = jnp.dot(ws_ref[gi], vb[r0:r0 + CHUNK, c0:c0 + gd], preferred_element_type=F32)
            s = s + bs_ref[:, c0:c0 + gd]
            o_ref[r0:r0 + CHUNK, c0:c0 + gd] = (u[r0:r0 + CHUNK, c0:c0 + gd] * s).astype(o_ref.dtype)


def spatial_gating(p, v_gain, w_s, b_s):
    m, two_w = p.shape
    width = two_w // 2
    gd = width // A_GROUPS
    tm = _tile(m, 512)
    bs_full = jnp.repeat(b_s.T, gd, axis=1)
    return pl.pallas_call(
        functools.partial(_gmlp_kernel, tm=tm, width=width),
        grid=(m // tm,),
        in_specs=[pl.BlockSpec((tm, two_w), lambda i: (i, 0)),
                  pl.BlockSpec((1, width), lambda i: (0, 0)),
                  pl.BlockSpec((A_GROUPS, CHUNK, CHUNK), lambda i: (0, 0, 0)),
                  pl.BlockSpec((CHUNK, width), lambda i: (0, 0))],
        out_specs=pl.BlockSpec((tm, width), lambda i: (i, 0)),
        out_shape=jax.ShapeDtypeStruct((m, width), BF16),
        compiler_params=_params("parallel"),
        name="spatial_gating",
    )(p, v_gain.reshape(1, width), w_s.astype(BF16), bs_full)


def _dft_matrices(L, rows=None):
    n = 2 * L
    r0 = L // 64
    rows = L if rows is None else rows
    assert rows % r0 == 0
    r1 = rows // r0
    t = np.arange(L, dtype=np.int64)[None, :]
    a1 = ((r0 * np.arange(r1, dtype=np.int64))[:, None] * t) % n
    a0 = (np.arange(r0, dtype=np.int64)[:, None] * t) % n
    ca, sa, cb, sb = lax.optimization_barrier((
        jnp.asarray(np.cos(2 * np.pi * a1 / n), F32)[:, None, :],
        jnp.asarray(np.sin(2 * np.pi * a1 / n), F32)[:, None, :],
        jnp.asarray(np.cos(2 * np.pi * a0 / n), F32)[None, :, :],
        jnp.asarray(np.sin(2 * np.pi * a0 / n), F32)[None, :, :]))
    cosm = (ca * cb - sa * sb).astype(BF16).reshape(rows, L)
    sinm = (-(sa * cb + ca * sb)).astype(BF16).reshape(rows, L)
    return cosm, sinm


def _filter_features(L):
    t = np.linspace(0.0, 1.0, L)[:, None]
    w = 2.0 * np.pi * np.arange(L)[:, None] / L
    bands = np.linspace(1e-4, HYENA_BANDS - 1, HYENA_BANDS)[None, :]
    z = np.concatenate([t, np.cos(w * bands), -np.sin(w * bands)], axis=-1)
    return jnp.asarray(np.pad(z, ((0, 0), (0, 128 - HYENA_EMB))), F32)


def _alternating(shape):
    rows = lax.broadcasted_iota(jnp.int32, shape, 0)
    return jnp.where(rows % 2 == 0, 1.0, -1.0)


def _twiddles(L):
    g = np.arange(L // 4)[:, None] * np.ones((1, LANES))
    a1, a2 = 2.0 * np.pi * g / L, 2.0 * np.pi * g / (2 * L)
    return jnp.asarray(np.stack([np.cos(a1), np.sin(a1), np.cos(a2), np.sin(a2)]), F32)


def _cadd(a, b):
    return a[0] + b[0], a[1] + b[1]


def _csub(a, b):
    return a[0] - b[0], a[1] - b[1]


def _conj(a):
    return a[0], -a[1]


def _cmul(a, b):
    return a[0] * b[0] - a[1] * b[1], a[0] * b[1] + a[1] * b[0]


def _rot(a, c, s):
    return a[0] * c + a[1] * s, a[1] * c - a[0] * s


def _unrot(a, c, s):
    return a[0] * c - a[1] * s, a[1] * c + a[0] * s


def _times_i(a):
    return -a[1], a[0]


def _times_minus_i(a):
    return a[1], -a[0]


def _lane_block(x, j):
    return x[:, j * LANES:(j + 1) * LANES]


def _period8_patterns():
    t = np.arange(8)[:, None] * np.ones((1, LANES))
    rows = [np.cos(np.pi * t)]
    for k in (2, 1, 3):
        rows += [np.cos(k * np.pi * t / 4), np.sin(k * np.pi * t / 4)]
    return jnp.asarray(np.round(np.stack(rows), 15), F32)


def _filter_spectrum_kernel(z_ref, w1_ref, b1_ref, w2_ref, b2_ref, fr_ref, w3f_ref, w3b_ref, dl_ref,
                            pat_ref, cos_ref, sin_ref, o_ref, rc_ref, rs_ref, nyq_ref, h_ref, taps_ref,
                            *, n):
    m = pl.program_id(2)
    hp = lax.Precision.HIGHEST

    @pl.when(jnp.logical_and(jnp.logical_and(pl.program_id(0) == 0, pl.program_id(1) == 0), m == 0))
    def _():
        h = jnp.dot(z_ref[...], w1_ref[...], precision=hp, preferred_element_type=F32) + b1_ref[...]
        h = jnp.sin(fr_ref[0:1, :] * h)
        h = jnp.dot(h, w2_ref[...], precision=hp, preferred_element_type=F32) + b2_ref[...]
        h_ref[...] = jnp.sin(fr_ref[1:2, :] * h)

    @pl.when(m == 0)
    def _():
        length = z_ref.shape[0]
        chunk = min(length, 2 * DFT_TILE)
        nrm = None
        tc = taps_ref.shape[2]
        w3 = jnp.concatenate([w3f_ref[...], w3b_ref[...]], axis=1)
        for r0 in range(0, length, chunk):
            rs = slice(r0, r0 + chunk)
            decay = jnp.exp(-z_ref[rs, 0:1] * dl_ref[...])
            hfb = jnp.dot(h_ref[rs, :], w3, precision=hp, preferred_element_type=F32)
            hf, hb = hfb[:, :tc] * decay, hfb[:, tc:] * decay
            if r0 == 0:
                hb = jnp.where(lax.broadcasted_iota(jnp.int32, hb.shape, 0) == 0, 0.0, hb)
            part = jnp.sum(jnp.abs(hf) + jnp.abs(hb), axis=0, keepdims=True)
            nrm = part if nrm is None else nrm + part
            taps_ref[0, rs, :] = hf + hb
            taps_ref[1, rs, :] = hf - hb
        inv = 1.0 / nrm
        packed = [None] * 4
        alt, hc, hs, qc, qs, q3c, q3s = (jnp.concatenate([pat_ref[i]] * (chunk // 8), axis=0)
                                         for i in range(7))
        for r0 in range(0, length, chunk):
            rs = slice(r0, r0 + chunk)
            ke = taps_ref[0, rs, :] * inv
            ko = taps_ref[1, rs, :] * inv
            for blk, (vc, vs) in enumerate(((ke, ko), (ke * alt, -ko * alt), (ke * hc, ke * hs),
                                            (ko * hs, ko * hc))):
                rc_ref[rs, blk * tc:(blk + 1) * tc] = vc.astype(BF16)
                rs_ref[rs, blk * tc:(blk + 1) * tc] = vs.astype(BF16)
            parts = (jnp.sum(ke * qc, axis=0, keepdims=True), -jnp.sum(ko * qs, axis=0, keepdims=True),
                     jnp.sum(ke * q3c, axis=0, keepdims=True), -jnp.sum(ko * q3s, axis=0, keepdims=True))
            packed = [pt if acc is None else acc + pt for acc, pt in zip(packed, parts)]
        for i in range(4):
            nyq_ref[i:i + 1, :] = packed[i] * (2.0 / n)

    tc = o_ref.shape[2]
    rc = jnp.dot(cos_ref[...], rc_ref[...], preferred_element_type=F32)
    rs = jnp.dot(sin_ref[...], rs_ref[...], preferred_element_type=F32)
    c = [rc[:, i * tc:(i + 1) * tc] for i in range(4)]
    s = [rs[:, i * tc:(i + 1) * tc] for i in range(4)]
    planes = [c[0], s[0], c[1], s[1], c[2] - s[2], -c[3] - s[3], c[2] + s[2], s[3] - c[3]]
    first = jnp.logical_and(lax.broadcasted_iota(jnp.int32, c[0].shape, 0) == 0, m == 0)
    scale = jnp.where(first, 1.0 / n, 2.0 / n)
    slots = {1: 2, 3: 3, 6: 0, 7: 1}
    for i in range(8):
        val = planes[i] * scale
        if i in slots:
            val = jnp.where(first, nyq_ref[slots[i]:slots[i] + 1, :], val)
        o_ref[i] = val


def hyena_filter_spectrum(L, width, cosm, sinm, w1, b1, w2, b2, w3, freq):
    z = _filter_features(L)
    tc = LANES
    ncb = width // tc
    max_decay = math.log(HYENA_TARGET) / HYENA_FAST
    min_decay = math.log(HYENA_TARGET) / HYENA_SLOW
    deltas = jnp.asarray(np.abs(np.linspace(min_decay, max_decay, width)), F32).reshape(1, width)
    w1p = jnp.pad(w1, ((0, 128 - HYENA_EMB), (0, 0)))
    const = lambda o, c, m: (0, 0)
    tm = _tile(L // 4, DFT_TILE)
    return pl.pallas_call(
        functools.partial(_filter_spectrum_kernel, n=2 * L),
        grid=(HYENA_ORDER, ncb, L // 4 // tm),
        in_specs=[pl.BlockSpec((L, 128), const),
                  pl.BlockSpec((128, HYENA_FFN), const), pl.BlockSpec((1, HYENA_FFN), const),
                  pl.BlockSpec((HYENA_FFN, HYENA_FFN), const), pl.BlockSpec((1, HYENA_FFN), const),
                  pl.BlockSpec((2, HYENA_FFN), const),
                  pl.BlockSpec((HYENA_FFN, tc), lambda o, c, m: (0, (2 * o) * ncb + c)),
                  pl.BlockSpec((HYENA_FFN, tc), lambda o, c, m: (0, (2 * o + 1) * ncb + c)),
                  pl.BlockSpec((1, tc), lambda o, c, m: (0, c)),
                  pl.BlockSpec((7, 8, LANES), lambda o, c, m: (0, 0, 0)),
                  pl.BlockSpec((tm, L), lambda o, c, m: (m, 0)),
                  pl.BlockSpec((tm, L), lambda o, c, m: (m, 0))],
        out_specs=pl.BlockSpec((None, 8, tm, tc), lambda o, c, m: (o, 0, m, c)),
        out_shape=jax.ShapeDtypeStruct((HYENA_ORDER, 8, L // 4, width), F32),
        scratch_shapes=[pltpu.VMEM((L, 4 * tc), BF16), pltpu.VMEM((L, 4 * tc), BF16),
                        pltpu.VMEM((4, tc), F32), pltpu.VMEM((L, HYENA_FFN), F32),
                        pltpu.VMEM((2, L, tc), F32)],
        compiler_params=_params("arbitrary", "arbitrary", "arbitrary"),
        name="hyena_filter_spectrum",
    )(z, w1p, b1.reshape(1, -1), w2, b2.reshape(1, -1), freq, w3, w3, deltas, _period8_patterns(),
      cosm, sinm)


def _dft_forward_kernel(u_ref, cos_ref, sin_ref, tw_ref, kf_ref, o_ref, ub_ref):
    q = ub_ref.shape[0]
    nyq = []
    for j in range(4):
        xj = u_ref[pl.ds(j, q, stride=4), :]
        ub_ref[:, j * LANES:(j + 1) * LANES] = xj.astype(BF16)
        nyq.append(jnp.sum(xj * _alternating(xj.shape), axis=0, keepdims=True))
    eq, oq = (nyq[0], -nyq[2]), (nyq[1], -nyq[3])
    woq = _rot(oq, math.sqrt(0.5), math.sqrt(0.5))
    u_q, u_3q = _cadd(eq, woq), _conj(_csub(eq, woq))
    chunk = min(q, DFT_TILE)
    for r0 in range(0, q, chunk):
        rs = slice(r0, r0 + chunk)
        re = jnp.dot(cos_ref[rs, :], ub_ref[...], preferred_element_type=F32)
        im = jnp.dot(sin_ref[rs, :], ub_ref[...], preferred_element_type=F32)
        x = [(_lane_block(re, j), _lane_block(im, j)) for j in range(4)]
        c1, s1, c2, s2 = tw_ref[0, rs, :], tw_ref[1, rs, :], tw_ref[2, rs, :], tw_ref[3, rs, :]
        t2, t3 = _rot(x[2], c1, s1), _rot(x[3], c1, s1)
        ea, eb = _cadd(x[0], t2), _csub(x[0], t2)
        oa, ob = _cadd(x[1], t3), _csub(x[1], t3)
        ta, tb = _rot(oa, c2, s2), _times_i(_rot(ob, c2, s2))
        p = [_cadd(ea, ta), _conj(_csub(ea, ta)), _conj(_cadd(eb, tb)), _csub(eb, tb)]
        k = [(kf_ref[2 * i, rs, :], kf_ref[2 * i + 1, rs, :]) for i in range(4)]
        if r0 == 0:
            first = lax.broadcasted_iota(jnp.int32, re.shape[:1] + (LANES,), 0) == 0
            p[3] = (jnp.where(first, u_q[0], p[3][0]), jnp.where(first, u_q[1], p[3][1]))
        y = [_cmul(p[i], k[i]) for i in range(4)]
        if r0 == 0:
            y3q = _cmul(u_3q, (k[0][1], k[1][1]))
            y[0] = (y[0][0], jnp.where(first, y3q[0], y[0][1]))
            y[1] = (y[1][0], jnp.where(first, y3q[1], y[1][1]))
        for i in range(4):
            o_ref[2 * i, rs, :] = y[i][0].astype(o_ref.dtype)
            o_ref[2 * i + 1, rs, :] = y[i][1].astype(o_ref.dtype)


def _resident(shape):
    return pl.BlockSpec(shape, lambda c, b: (0,) * len(shape), pipeline_mode=pl.Buffered(1))


def dft_forward_times_filter(u, col0, width, cos4, sin4, tw, kf, order):
    bsz, L, _ = u.shape
    q = L // 4
    assert col0 % LANES == 0
    cb0 = col0 // LANES
    return pl.pallas_call(
        _dft_forward_kernel,
        grid=(width // LANES, bsz),
        in_specs=[pl.BlockSpec((None, L, LANES), lambda c, b: (b, 0, cb0 + c)),
                  _resident((q, q)), _resident((q, q)), _resident((4, q, LANES)),
                  pl.BlockSpec((None, 8, q, LANES), lambda c, b: (order, 0, 0, c))],
        out_specs=pl.BlockSpec((None, 8, q, LANES), lambda c, b: (b, 0, 0, c)),
        out_shape=jax.ShapeDtypeStruct((bsz, 8, q, width), BF16),
        scratch_shapes=[pltpu.VMEM((q, 4 * LANES), BF16)],
        compiler_params=_params("parallel", "parallel"),
        name="dft_forward",
    )(u, cos4, sin4, tw, kf)


def _dft_inverse_kernel(y_ref, cos_ref, sin_ref, tw_ref, u_ref, g_ref, skip_ref, o_ref, x_ref, res_ref):
    q = x_ref.shape[1]

    def row0(i):
        return y_ref[i, 0:1, :].astype(F32)

    u_q, u_3q = (row0(6), row0(7)), (row0(1), row0(3))
    eq = _cadd(u_q, _conj(u_3q))
    oq = _unrot(_csub(u_q, _conj(u_3q)), math.sqrt(0.5), math.sqrt(0.5))
    nyq = jnp.concatenate([eq[0], oq[0], -eq[1], -oq[1]], axis=1)
    chunk = min(q, DFT_TILE)
    for r0 in range(0, q, chunk):
        rs = slice(r0, r0 + chunk)
        p = [(y_ref[2 * i, rs, :].astype(F32), y_ref[2 * i + 1, rs, :].astype(F32)) for i in range(4)]
        if r0 == 0:
            first = lax.broadcasted_iota(jnp.int32, p[0][0].shape, 0) == 0
            p[0] = (p[0][0], jnp.where(first, 0.0, p[0][1]))
            p[1] = (p[1][0], jnp.where(first, 0.0, p[1][1]))
            p[3] = (jnp.where(first, p[2][0], p[3][0]), jnp.where(first, p[2][1], p[3][1]))
        c1, s1, c2, s2 = tw_ref[0, rs, :], tw_ref[1, rs, :], tw_ref[2, rs, :], tw_ref[3, rs, :]
        ea, ta = _cadd(p[0], _conj(p[1])), _csub(p[0], _conj(p[1]))
        eb, tb = _cadd(_conj(p[2]), p[3]), _csub(_conj(p[2]), p[3])
        oa = _unrot(ta, c2, s2)
        ob = _times_minus_i(_unrot(tb, c2, s2))
        x = [_cadd(ea, eb), _cadd(oa, ob), _unrot(_csub(ea, eb), c1, s1), _unrot(_csub(oa, ob), c1, s1)]
        for j in range(4):
            x_ref[0, rs, j * LANES:(j + 1) * LANES] = x[j][0].astype(BF16)
            x_ref[1, rs, j * LANES:(j + 1) * LANES] = x[j][1].astype(BF16)
    skip = skip_ref[...]
    alt = _alternating((chunk, 4 * LANES))
    for r0 in range(0, q, chunk):
        rs = slice(r0, r0 + chunk)
        conv = jnp.dot(cos_ref[rs, :], x_ref[0], preferred_element_type=F32)
        conv = conv + jnp.dot(sin_ref[rs, :], x_ref[1], preferred_element_type=F32)
        conv = conv + alt * nyq
        for j in range(4):
            rows = pl.ds(4 * r0 + j, chunk, stride=4)
            res_ref[rows, :] = g_ref[rows, :] * (_lane_block(conv, j) + u_ref[rows, :] * skip)
    o_ref[...] = res_ref[...].astype(o_ref.dtype)


def dft_inverse_gated(y, cos4, sin4, tw, u, ucol0, gate, gcol0, skip, out_dtype):
    bsz, _, q, width = y.shape
    L = 4 * q
    ub0, gb0 = ucol0 // LANES, gcol0 // LANES
    return pl.pallas_call(
        _dft_inverse_kernel,
        grid=(width // LANES, bsz),
        in_specs=[pl.BlockSpec((None, 8, q, LANES), lambda c, b: (b, 0, 0, c)),
                  _resident((q, q)), _resident((q, q)), _resident((4, q, LANES)),
                  pl.BlockSpec((None, L, LANES), lambda c, b: (b, 0, ub0 + c)),
                  pl.BlockSpec((None, L, LANES), lambda c, b: (b, 0, gb0 + c)),
                  pl.BlockSpec((1, LANES), lambda c, b: (0, c))],
        out_specs=pl.BlockSpec((None, L, LANES), lambda c, b: (b, 0, c)),
        out_shape=jax.ShapeDtypeStruct((bsz, L, width), out_dtype),
        scratch_shapes=[pltpu.VMEM((2, q, 4 * LANES), BF16), pltpu.VMEM((L, LANES), F32)],
        compiler_params=_params("parallel", "parallel"),
        name="dft_inverse",
    )(y, cos4, sin4, tw, u, gate, skip.reshape(1, width))


NA_GROUP = 8
NA_BLOCK = NA_GROUP + NA_KH
NA_PAD = NA_KH // 2


def _na_plans():
    tiles = {}
    plans = []
    for variant in range(3):
        plan = []
        for rq in range(NA_GROUP):
            lo = (max(rq, NA_PAD), rq, min(rq, NA_PAD))[variant]
            a_lo, a_hi = lo // 2, (lo + NA_KH - 1) // 2 + 1
            ids = []
            for a in range(a_lo, a_hi):
                key = tuple(kr - rq + NA_KH - 1 - NA_PAD if lo <= kr < lo + NA_KH else -1
                            for kr in (2 * a, 2 * a + 1))
                ids.append(tiles.setdefault(key, len(tiles)))
            plan.append((a_lo, tuple(ids)))
        plans.append(tuple(plan))
    return tuple(plans), list(tiles)


def _na_kernel(q_ref, k_ref, v_ref, tiles_ref, o_ref, kb_ref, vb_ref, p_ref, *, rows, scale, plans):
    L = rows * GRID_W
    pad = NA_PAD * GRID_W
    gq = NA_GROUP * GRID_W
    gk = NA_BLOCK * GRID_W
    for ref, src in ((kb_ref, k_ref), (vb_ref, v_ref)):
        ref[0:pad, :] = jnp.zeros((pad, HEAD_DIM), BF16)
        ref[pad + L:, :] = jnp.zeros((pad, HEAD_DIM), BF16)
        ref[pad:pad + L, :] = src[...].astype(BF16)

    def group(q0, plan, slot):
        q = q_ref[pl.ds(q0, gq), :].astype(BF16)
        s = lax.dot_general(q, kb_ref[pl.ds(q0, gk), :], (((1,), (1,)), ((), ())),
                            preferred_element_type=F32)
        p_ref[slot] = jnp.zeros(p_ref.shape[1:], BF16)
        for rq, (a_lo, ids) in enumerate(plan):
            r0, c0, c1 = rq * GRID_W, 2 * GRID_W * a_lo, 2 * GRID_W * (a_lo + len(ids))
            bias = jnp.concatenate([tiles_ref[t] for t in ids], axis=1)
            sl = s[r0:r0 + GRID_W, c0:c1] * scale + bias
            e = jnp.exp(sl - jnp.max(sl, axis=-1, keepdims=True))
            pr = e * (1.0 / jnp.sum(e, axis=-1, keepdims=True))
            p_ref[slot, r0:r0 + GRID_W, c0:c1] = pr.astype(BF16)
        o = jnp.dot(p_ref[slot], vb_ref[pl.ds(q0, gk), :], preferred_element_type=F32)
        o_ref[pl.ds(q0, gq), :] = o.astype(o_ref.dtype)

    ngroups = rows // NA_GROUP
    nbuf = p_ref.shape[0]
    for g in range(ngroups):
        plan = plans[0] if g == 0 else plans[2] if g == ngroups - 1 else plans[1]
        group(g * gq, plan, g % nbuf)


def _na_bias_tiles(rpb, tile_keys):
    col = np.arange(GRID_W)
    cs = np.clip(col - NA_KW // 2, 0, GRID_W - NA_KW)
    kc = np.arange(GRID_W)
    inwin = (kc[None, :] >= cs[:, None]) & (kc[None, :] < cs[:, None] + NA_KW)
    rel_col = np.clip(kc[None, :] - col[:, None] + (NA_KW - 1), 0, 2 * NA_KW - 2)
    t = jnp.where(jnp.asarray(inwin)[None, None], rpb[:, :, rel_col], MASK_VALUE)
    masked = jnp.full((rpb.shape[0], 1, GRID_W, GRID_W), MASK_VALUE, F32)
    t = jnp.concatenate([t.astype(F32), masked], axis=1)
    left = np.array([k[0] for k in tile_keys])
    right = np.array([k[1] for k in tile_keys])
    return jnp.concatenate([t[:, left], t[:, right]], axis=-1)


def neighborhood_attention(p, bsz, L, heads, rpb):
    rows = L // GRID_W
    assert rows % NA_GROUP == 0 and rows >= 2 * NA_GROUP
    plans, tile_keys = _na_plans()
    tiles = _na_bias_tiles(rpb, tile_keys)
    nt = len(tile_keys)
    lp = L + 2 * NA_PAD * GRID_W
    return pl.pallas_call(
        functools.partial(_na_kernel, rows=rows, scale=HEAD_DIM ** -0.5, plans=plans),
        grid=(bsz, heads),
        in_specs=[pl.BlockSpec((None, L, HEAD_DIM), lambda b, h: (b, 0, h)),
                  pl.BlockSpec((None, L, HEAD_DIM), lambda b, h: (b, 0, heads + h)),
                  pl.BlockSpec((None, L, HEAD_DIM), lambda b, h: (b, 0, 2 * heads + h)),
                  pl.BlockSpec((None, nt, GRID_W, 2 * GRID_W), lambda b, h: (h, 0, 0, 0))],
        out_specs=pl.BlockSpec((None, L, HEAD_DIM), lambda b, h: (b, 0, h)),
        out_shape=jax.ShapeDtypeStruct((bsz, L, heads * HEAD_DIM), BF16),
        scratch_shapes=[pltpu.VMEM((lp, HEAD_DIM), BF16), pltpu.VMEM((lp, HEAD_DIM), BF16),
                        pltpu.VMEM((4, NA_GROUP * GRID_W, NA_BLOCK * GRID_W), BF16)],
        compiler_params=_params("parallel", "parallel"),
        name="neighborhood_attention",
    )(p, p, p, tiles)


DIFF_SUB_ROWS = 256


def _rope(x, cos, sin_signed, half):
    lane = lax.broadcasted_iota(jnp.int32, x.shape, 1)
    width = x.shape[1]
    first = (lane % (2 * half)) < half
    partner = jnp.where(first, pltpu.roll(x, width - half, 1), pltpu.roll(x, half, 1))
    return x * cos + partner * sin_signed


def _diff_attn_kernel(q_ref, k_ref, v_ref, cq_ref, sq_ref, ck_ref, sk_ref, lam_ref, sub_ref, o_ref,
                      kb_ref, vb_ref, *, lam_init, scale, half, n_sub):
    qi = pl.program_id(2)

    @pl.when(qi == 0)
    def _():
        kb_ref[...] = _rope(k_ref[...], ck_ref[...], sk_ref[...], half).astype(BF16)
        vb_ref[...] = v_ref[...].astype(BF16)

    lp = lam_ref[...]
    lam = (jnp.exp(jnp.sum(lp[0:1] * lp[1:2], axis=-1, keepdims=True))
           - jnp.exp(jnp.sum(lp[2:3] * lp[3:4], axis=-1, keepdims=True)) + lam_init)
    kb = kb_ref[...]
    tq = q_ref.shape[0]
    sub = tq // n_sub
    for sb in range(n_sub):
        rs = slice(sb * sub, (sb + 1) * sub)
        q = _rope(q_ref[rs, :], cq_ref[rs, :], sq_ref[rs, :], half) * (scale * math.log2(math.e))
        lane = lax.broadcasted_iota(jnp.int32, q.shape, 1)

        def softmax_map(in_map, weight):
            qm = jnp.where(in_map, q, 0.0).astype(BF16)
            s = lax.dot_general(qm, kb, (((1,), (1,)), ((), ())), preferred_element_type=F32)
            e = jnp.exp2(s - jnp.max(s, axis=-1, keepdims=True))
            return e * (weight / jnp.sum(e, axis=-1, keepdims=True))

        a = softmax_map(lane < 2 * half, 1.0) - softmax_map(lane >= 2 * half, lam)
        o = jnp.dot(a.astype(BF16), vb_ref[...], preferred_element_type=F32)
        ms = jnp.mean(o * o, axis=-1, keepdims=True)
        o_ref[rs, :] = (o * lax.rsqrt(ms + EPS) * sub_ref[...] * (1.0 - lam_init)).astype(o_ref.dtype)


def _rope_tables(L, half):
    inv = (1.0 / (np.float32(ROPE_THETA) ** (np.arange(half, dtype=np.float32) * np.float32(2.0)
                                             / np.float32(2 * half)))).astype(np.float32)
    ang = (np.arange(L, dtype=np.float32)[:, None] * inv[None, :]).astype(np.float64)
    cos, sin = np.cos(ang), np.sin(ang)
    reps = HEAD_DIM // (2 * half)
    cos_t = np.tile(cos, (1, 2 * reps))
    sin_t = np.tile(np.concatenate([-sin, sin], axis=1), (1, reps))
    return jnp.asarray(cos_t, F32), jnp.asarray(sin_t, F32)


def diff_attention(p, col0, bsz, L, heads, lam_params, subln, lam_init):
    half = HEAD_DIM // 4
    assert col0 % HEAD_DIM == 0
    c0 = col0 // HEAD_DIM
    tq = _tile(L, 2 * DIFF_SUB_ROWS)
    n_sub = tq // DIFF_SUB_ROWS
    cos_t, sin_t = _rope_tables(L, half)
    return pl.pallas_call(
        functools.partial(_diff_attn_kernel, lam_init=lam_init, scale=(2 * half) ** -0.5, half=half,
                          n_sub=n_sub),
        grid=(bsz, heads, L // tq),
        in_specs=[pl.BlockSpec((None, tq, HEAD_DIM), lambda b, h, i: (b, i, c0 + h)),
                  pl.BlockSpec((None, L, HEAD_DIM), lambda b, h, i: (b, 0, c0 + heads + h)),
                  pl.BlockSpec((None, L, HEAD_DIM), lambda b, h, i: (b, 0, c0 + 2 * heads + h)),
                  pl.BlockSpec((tq, HEAD_DIM), lambda b, h, i: (i, 0)),
                  pl.BlockSpec((tq, HEAD_DIM), lambda b, h, i: (i, 0)),
                  pl.BlockSpec((L, HEAD_DIM), lambda b, h, i: (0, 0)),
                  pl.BlockSpec((L, HEAD_DIM), lambda b, h, i: (0, 0)),
                  pl.BlockSpec((4, 2 * half), lambda b, h, i: (0, 0)),
                  pl.BlockSpec((1, HEAD_DIM), lambda b, h, i: (0, 0))],
        out_specs=pl.BlockSpec((None, tq, HEAD_DIM), lambda b, h, i: (b, i, h)),
        out_shape=jax.ShapeDtypeStruct((bsz, L, heads * HEAD_DIM), BF16),
        scratch_shapes=[pltpu.VMEM((L, HEAD_DIM), BF16), pltpu.VMEM((L, HEAD_DIM), BF16)],
        compiler_params=_params("parallel", "parallel", "arbitrary"),
        name="diff_attention",
    )(p, p, p, cos_t, sin_t, cos_t, sin_t, lam_params, subln.reshape(1, HEAD_DIM))


def kernel(x, norm_mix, norm_ffn, w_out, ffn_up, ffn_conv_w, ffn_conv_b, ffn_down, final_norm,
           ab_w_in, a_vnorm, a_ws, a_bs, b_conv_w, b_conv_b, b_filt_w1, b_filt_b1, b_filt_w2,
           b_filt_b2, b_filt_w3, b_filt_freq, b_skip, cd_w_in, c_rpb, d_lambda, d_subln):
    bsz, L, d = x.shape
    depth = norm_mix.shape[0]
    m = bsz * L
    half_w = d // 2
    ff = ffn_down.shape[1]
    heads = half_w // HEAD_DIM
    assert heads == D_HEADS and half_w // A_GROUPS == CHUNK and L % GRID_W == 0
    xs = x.reshape(m, d)
    cosm, sinm = _dft_matrices(L, rows=L // 4)
    cos4, sin4 = _dft_matrices(L // 4)
    tw = _twiddles(L)
    w_out_b = w_out.astype(BF16)
    ffn_down_b = ffn_down.astype(BF16)
    h = rmsnorm(xs, norm_mix[0], BF16)
    for l in range(depth):
        i = l // 2
        if l % 2 == 0:
            pa = matmul(h, ab_w_in, i, 0, 2 * half_w, F32)
            ya = spatial_gating(pa, a_vnorm[i], a_ws[i], a_bs[i])
            pb = matmul_conv(h, ab_w_in, b_conv_w, b_conv_b, i, L, 2 * half_w, 3 * half_w, False, F32)
            pb = pb.reshape(bsz, L, 3 * half_w)
            kf = hyena_filter_spectrum(L, half_w, cosm, sinm, b_filt_w1[i], b_filt_b1[i], b_filt_w2[i],
                                       b_filt_b2[i], b_filt_w3[i], b_filt_freq[i])
            y1 = dft_forward_times_filter(pb, 0, half_w, cos4, sin4, tw, kf, 0)
            z = dft_inverse_gated(y1, cos4, sin4, tw, pb, 0, pb, half_w, b_skip[i, 0], F32)
            y2 = dft_forward_times_filter(z, 0, half_w, cos4, sin4, tw, kf, 1)
            yb = dft_inverse_gated(y2, cos4, sin4, tw, z, 0, pb, 2 * half_w, b_skip[i, 1], BF16)
            yb = yb.reshape(m, half_w)
        else:
            p = matmul(h, cd_w_in, i, 0, 6 * half_w, F32).reshape(bsz, L, 6 * half_w)
            lam_init = 0.8 - 0.6 * math.exp(-0.3 * l)
            ya = neighborhood_attention(p, bsz, L, heads, c_rpb[i]).reshape(m, half_w)
            yb = diff_attention(p, 3 * half_w, bsz, L, heads, d_lambda[i], d_subln[i], lam_init)
            yb = yb.reshape(m, half_w)
        xs, hf = outproj_norm(ya, yb, w_out_b, l, xs, norm_ffn[l])
        act = matmul_conv(hf, ffn_up, ffn_conv_w, ffn_conv_b, l, L, 0, ff, True, BF16)
        last = l == depth - 1
        xs, h = downproj_norm(act, ffn_down_b, l, xs, final_norm if last else norm_mix[l + 1],
                              F32 if last else BF16)
    return h.reshape(bsz, L, d)
```
